```python
import jax, jax.numpy as jnp
from jax import lax
import numpy as np

D_MODEL = 1024
BATCH = 16
SEQ = 2048
DEPTH = 1

CTX_LEN = 256
GRID_W = 64
RET_HEADS = 4
RET_DIM = 128
RET_WIDTH = RET_HEADS * RET_DIM
RET_CHUNK = 128
NA_HEADS = 8
NA_DIM = 64
NA_WIDTH = NA_HEADS * NA_DIM
NA_KH = 8
NA_KW = 16
MIX_WIDTH = RET_WIDTH + NA_WIDTH
IN_SPLITS = (RET_WIDTH, RET_WIDTH, RET_WIDTH, RET_WIDTH, NA_WIDTH, NA_WIDTH, NA_WIDTH)
IN_WIDTH = 4 * RET_WIDTH + 3 * NA_WIDTH
D_FF = 4 * D_MODEL
ROPE_BASE = 10000.0
NORM_EPS = 1e-6
N_MOD = 6
NEG_INF = -1e30

kernel_name = 'hybrid_retention_natten_dit_block'


def rmsnorm(x, g):
    xf = x.astype(jnp.float32)
    y = xf * lax.rsqrt(jnp.mean(xf * xf, axis=-1, keepdims=True) + NORM_EPS)
    return (y * g.astype(jnp.float32)).astype(x.dtype)


def modulations(cvec, w_ada, b_ada):
    return jnp.split(jax.nn.silu(cvec) @ w_ada + b_ada, N_MOD, axis=-1)


def to_heads(t, n_heads):
    b, l, _ = t.shape
    return t.reshape(b, l, n_heads, -1).transpose(0, 2, 1, 3)


def axial_rope(x, pos_r, pos_c):
    d_axis = x.shape[-1] // 2
    n_freq = d_axis // 2
    inv = ROPE_BASE ** (-jnp.arange(n_freq, dtype=jnp.float32) / n_freq)

    def rot(seg, pos):
        ang = pos[:, None] * inv[None, :]
        cos, sin = jnp.cos(ang), jnp.sin(ang)
        s1 = seg[..., :n_freq].astype(jnp.float32)
        s2 = seg[..., n_freq:].astype(jnp.float32)
        return jnp.concatenate([s1 * cos - s2 * sin, s1 * sin + s2 * cos], axis=-1)

    out = jnp.concatenate([rot(x[..., :d_axis], pos_r), rot(x[..., d_axis:], pos_c)], axis=-1)
    return out.astype(x.dtype)


def retention_scan(q, k, v, log_gamma, state0, strict):
    b, h, l, dk = q.shape
    dv = v.shape[-1]
    nc = l // RET_CHUNK
    qc = (q * dk ** -0.5).reshape(b, h, nc, RET_CHUNK, dk)
    kc = k.reshape(b, h, nc, RET_CHUNK, dk)
    vc = v.reshape(b, h, nc, RET_CHUNK, dv)
    lg = log_gamma.astype(jnp.float32)[:, None]
    i = jnp.arange(RET_CHUNK, dtype=jnp.float32)
    diff = i[:, None] - i[None, :]
    mask = (diff > 0) if strict else (diff >= 0)
    decay = jnp.where(mask, jnp.exp(lg[:, :, None] * jnp.where(mask, diff, 0.0)), 0.0)
    scores = jnp.einsum('bhnid,bhnjd->bhnij', qc, kc) * decay[:, None]
    inner = jnp.einsum('bhnij,bhnje->bhnie', scores, vc)
    k_w = kc * jnp.exp(lg * (RET_CHUNK - 1.0 - i))[:, None, :, None]
    upd = jnp.einsum('bhnjd,bhnje->nbhde', k_w, vc).astype(jnp.float32)
    chunk_decay = jnp.exp(lg * RET_CHUNK)[:, :, None]

    def step(state, u):
        return chunk_decay * state + u, state

    state_final, state_prev = lax.scan(step, state0.astype(jnp.float32), upd)
    q_w = qc * jnp.exp(lg * (i + 1.0))[:, None, :, None]
    cross = jnp.einsum('bhnid,nbhde->bhnie', q_w, state_prev)
    out = (inner + cross).reshape(b, h, l, dv).astype(v.dtype)
    return out, state_final


def head_layernorm(o, w):
    of = o.astype(jnp.float32)
    mu = jnp.mean(of, axis=-1, keepdims=True)
    var = jnp.mean(jnp.square(of - mu), axis=-1, keepdims=True)
    y = (of - mu) * lax.rsqrt(var + NORM_EPS)
    b, h, l, dv = o.shape
    y = y.transpose(0, 2, 1, 3).reshape(b, l, h * dv)
    return (y * w.astype(jnp.float32)).astype(o.dtype)


def retention_mixer(q, k, v, g, qc, kc, vc, gc, log_gammas, gn_w, with_ctx_out):
    n = q.shape[1]
    tok = jnp.arange(n)
    pos_r = (tok // GRID_W).astype(jnp.float32)
    pos_c = (tok % GRID_W).astype(jnp.float32)
    q = axial_rope(to_heads(q, RET_HEADS), pos_r, pos_c)
    k = axial_rope(to_heads(k, RET_HEADS), pos_r, pos_c)
    v = to_heads(v, RET_HEADS)
    qc, kc, vc = to_heads(qc, RET_HEADS), to_heads(kc, RET_HEADS), to_heads(vc, RET_HEADS)
    b = q.shape[0]
    zero = jnp.zeros((b, RET_HEADS, RET_DIM, RET_DIM), jnp.float32)

    def flip(t):
        return jnp.flip(t, axis=2)

    ctx_f, s_f = retention_scan(qc, kc, vc, log_gammas[0], zero, False)
    lat_f, _ = retention_scan(q, k, v, log_gammas[0], s_f, False)
    ctx_b, s_b = retention_scan(flip(qc), flip(kc), flip(vc), log_gammas[1], zero, True)
    lat_b, _ = retention_scan(flip(q), flip(k), flip(v), log_gammas[1], s_b, True)
    lat = head_layernorm(lat_f + flip(lat_b), gn_w) * jax.nn.silu(g)
    ctx_out = None
    if with_ctx_out:
        ctx_out = head_layernorm(ctx_f + flip(ctx_b), gn_w) * jax.nn.silu(gc)
    return lat, ctx_out


def neighbourhood_attention(q, k, v, kc, vc, rpb):
    b, n, _ = q.shape
    rows = n // GRID_W
    kh = min(NA_KH, rows)

    def grid(t):
        return t.reshape(b, rows, GRID_W, NA_HEADS, NA_DIM).transpose(0, 3, 1, 2, 4)

    qg = grid(q) * NA_DIM ** -0.5
    kg, vg = grid(k), grid(v)
    kc, vc = to_heads(kc, NA_HEADS), to_heads(vc, NA_HEADS)
    r = jnp.arange(rows)
    row_idx = jnp.clip(r - kh // 2, 0, rows - kh)[:, None] + jnp.arange(kh)[None, :]
    nk = kh * GRID_W
    k_band = kg[:, :, row_idx].reshape(b, NA_HEADS, rows, nk, NA_DIM)
    v_band = vg[:, :, row_idx].reshape(b, NA_HEADS, rows, nk, NA_DIM)
    col = jnp.arange(GRID_W)
    col_start = jnp.clip(col - NA_KW // 2, 0, GRID_W - NA_KW)
    key_col = jnp.tile(col, kh)
    key_row = jnp.repeat(row_idx, GRID_W, axis=1)
    valid = (key_col[None, :] >= col_start[:, None]) & (key_col[None, :] < col_start[:, None] + NA_KW)
    dr = key_row - r[:, None] + (NA_KH - 1)
    dc = jnp.clip(key_col[None, :] - col[:, None] + (NA_KW - 1), 0, 2 * NA_KW - 2)
    bias = rpb[:, dr[:, None, :], dc[None, :, :]].astype(jnp.float32)
    bias = jnp.where(valid[None, None], bias, NEG_INF)
    s_loc = jnp.einsum('bhrqd,bhrkd->bhrqk', qg, k_band).astype(jnp.float32) + bias
    s_ctx = jnp.einsum('bhrqd,bhkd->bhrqk', qg, kc).astype(jnp.float32)
    p = jax.nn.softmax(jnp.concatenate([s_loc, s_ctx], axis=-1), axis=-1).astype(v.dtype)
    out = (jnp.einsum('bhrqk,bhrkd->bhrqd', p[..., :nk], v_band)
           + jnp.einsum('bhrqk,bhkd->bhrqd', p[..., nk:], vc))
    return out.transpose(0, 2, 3, 1, 4).reshape(b, n, NA_WIDTH)


def context_attention(qc, kc, vc):
    b, l, _ = qc.shape
    q, k, v = to_heads(qc, NA_HEADS), to_heads(kc, NA_HEADS), to_heads(vc, NA_HEADS)
    s = jnp.einsum('bhqd,bhkd->bhqk', q * NA_DIM ** -0.5, k).astype(jnp.float32)
    p = jax.nn.softmax(s, axis=-1).astype(v.dtype)
    o = jnp.einsum('bhqk,bhkd->bhqd', p, v)
    return o.transpose(0, 2, 1, 3).reshape(b, l, NA_WIDTH)


def squared_relu_mlp(h, w1, w2):
    return jnp.square(jax.nn.relu(h @ w1)) @ w2


def hybrid_layer(x, ctx, c, c_ctx, w_ada, b_ada, g_pre_mix, g_post_mix, g_pre_mlp, g_post_mlp,
                 w_in, ret_decay, ret_gn, na_rpb, w_out, w_mlp1, w_mlp2, update_ctx):
    sh1, sc1, gt1, sh2, sc2, gt2 = modulations(c[:, None, :], w_ada, b_ada)
    csh1, csc1, cgt1, csh2, csc2, cgt2 = modulations(c_ctx, w_ada, b_ada)
    split_at = [int(s) for s in np.cumsum(IN_SPLITS)[:-1]]
    h = rmsnorm(x, g_pre_mix) * (1.0 + sc1) + sh1
    hc = rmsnorm(ctx, g_pre_mix) * (1.0 + csc1) + csh1
    rq, rk, rv, rg, nq, nk, nv = jnp.split(h @ w_in, split_at, axis=-1)
    crq, crk, crv, crg, cnq, cnk, cnv = jnp.split(hc @ w_in, split_at, axis=-1)
    log_gammas = jax.nn.log_sigmoid(ret_decay.astype(jnp.float32))
    ret_lat, ret_ctx = retention_mixer(rq, rk, rv, rg, crq, crk, crv, crg, log_gammas, ret_gn, update_ctx)
    na_lat = neighbourhood_attention(nq, nk, nv, cnk, cnv, na_rpb)
    mix = jnp.concatenate([ret_lat, na_lat], axis=-1) @ w_out
    x = x + gt1 * rmsnorm(mix, g_post_mix)
    h2 = rmsnorm(x, g_pre_mlp) * (1.0 + sc2) + sh2
    x = x + gt2 * rmsnorm(squared_relu_mlp(h2, w_mlp1, w_mlp2), g_post_mlp)
    if update_ctx:
        na_ctx = context_attention(cnq, cnk, cnv)
        mix_c = jnp.concatenate([ret_ctx, na_ctx], axis=-1) @ w_out
        ctx = ctx + cgt1 * rmsnorm(mix_c, g_post_mix)
        hc2 = rmsnorm(ctx, g_pre_mlp) * (1.0 + csc2) + csh2
        ctx = ctx + cgt2 * rmsnorm(squared_relu_mlp(hc2, w_mlp1, w_mlp2), g_post_mlp)
    return x, ctx


def setup_inputs(seed: int = 0) -> dict:
    key = jax.random.key(seed)
    ks = jax.random.split(key, 17)

    def nrm(k, shape, s):
        return jax.random.normal(k, shape, jnp.float32) * s

    base_logit = jnp.log(2.0 ** (5.0 + jnp.arange(RET_HEADS, dtype=jnp.float32)) - 1.0)
    return {
        'x': nrm(ks[0], (BATCH, SEQ, D_MODEL), 1.0),
        'c': nrm(ks[1], (BATCH, D_MODEL), 1.0),
        'ctx': nrm(ks[2], (BATCH, CTX_LEN, D_MODEL), 1.0),
        'c_ctx': nrm(ks[3], (D_MODEL,), 1.0),
        'w_ada': nrm(ks[4], (DEPTH, D_MODEL, N_MOD * D_MODEL), D_MODEL ** -0.5),
        'b_ada': nrm(ks[5], (DEPTH, N_MOD * D_MODEL), 0.02),
        'g_pre_mix': 1.0 + nrm(ks[6], (DEPTH, D_MODEL), 0.02),
        'g_post_mix': 1.0 + nrm(ks[7], (DEPTH, D_MODEL), 0.02),
        'g_pre_mlp': 1.0 + nrm(ks[8], (DEPTH, D_MODEL), 0.02),
        'g_post_mlp': 1.0 + nrm(ks[9], (DEPTH, D_MODEL), 0.02),
        'w_in': nrm(ks[10], (DEPTH, D_MODEL, IN_WIDTH), D_MODEL ** -0.5),
        'ret_decay': base_logit[None, None, :] + nrm(ks[11], (DEPTH, 2, RET_HEADS), 0.1),
        'ret_gn': 1.0 + nrm(ks[12], (DEPTH, RET_WIDTH), 0.02),
        'na_rpb': nrm(ks[13], (DEPTH, NA_HEADS, 2 * NA_KH - 1, 2 * NA_KW - 1), 0.1),
        'w_out': nrm(ks[14], (DEPTH, MIX_WIDTH, D_MODEL), MIX_WIDTH ** -0.5),
        'w_mlp1': nrm(ks[15], (DEPTH, D_MODEL, D_FF), D_MODEL ** -0.5),
        'w_mlp2': nrm(ks[16], (DEPTH, D_FF, D_MODEL), D_FF ** -0.5),
    }


def reference(x, c, ctx, c_ctx, w_ada, b_ada, g_pre_mix, g_post_mix, g_pre_mlp, g_post_mlp,
              w_in, ret_decay, ret_gn, na_rpb, w_out, w_mlp1, w_mlp2):
    for layer in range(DEPTH):
        x, ctx = hybrid_layer(x, ctx, c, c_ctx, w_ada[layer], b_ada[layer], g_pre_mix[layer],
                              g_post_mix[layer], g_pre_mlp[layer], g_post_mlp[layer], w_in[layer],
                              ret_decay[layer], ret_gn[layer], na_rpb[layer], w_out[layer],
                              w_mlp1[layer], w_mlp2[layer], update_ctx=(layer + 1 < DEPTH))
    return x
```

```python
import functools

import numpy as np
import jax
import jax.numpy as jnp
from jax import lax
from jax.experimental import pallas as pl
from jax.experimental.pallas import tpu as pltpu

D_MODEL = 1024
SEQ = 2048
CTX_LEN = 256
GRID_W = 64
GRID_ROWS = SEQ // GRID_W
RET_HEADS = 4
RET_DIM = 128
RET_WIDTH = RET_HEADS * RET_DIM
NA_HEADS = 8
NA_DIM = 64
NA_WIDTH = NA_HEADS * NA_DIM
NA_KH = 8
NA_KW = 16
N_GROUPS = 7
GROUP_W = 512
D_FF = 4 * D_MODEL
ROPE_BASE = 10000.0
NORM_EPS = 1e-6
N_MOD = 6
NEG_INF = -1e30

LANES = 128
ROW_TILE = 512
FF_CHUNK = 1024
RET_CHUNK = 256
NA_BAND = NA_KH * GRID_W
N_PATTERNS = 8
VMEM_LIMIT = 56 * 1024 * 1024

F32 = jnp.float32
BF16 = jnp.bfloat16
_NT = (((1,), (1,)), ((), ()))
_TN = (((0,), (0,)), ((), ()))


def _silu(x):
    return x * (1.0 / (1.0 + jnp.exp(-x)))


def _rms(x, g):
    return x * lax.rsqrt(jnp.mean(x * x, axis=-1, keepdims=True) + NORM_EPS) * g


def _params(n_axes):
    return pltpu.CompilerParams(dimension_semantics=("arbitrary",) * n_axes,
                                vmem_limit_bytes=VMEM_LIMIT)


def _resident(shape):
    nd = len(shape)
    return pl.BlockSpec(shape, lambda *_: (0,) * nd, pipeline_mode=pl.Buffered(1))


def _mod_kernel(c_ref, w_ref, b_ref, o_ref):
    a = _silu(c_ref[...]).astype(BF16)
    o_ref[...] = jnp.dot(a, w_ref[...].astype(BF16), preferred_element_type=F32) + b_ref[...]


def _modulations(cc, w_ada, b_ada):
    rows = cc.shape[0]
    return pl.pallas_call(
        _mod_kernel,
        grid=(N_MOD,),
        in_specs=[pl.BlockSpec((rows, D_MODEL), lambda j: (0, 0)),
                  pl.BlockSpec((D_MODEL, D_MODEL), lambda j: (0, j)),
                  pl.BlockSpec((1, D_MODEL), lambda j: (0, j))],
        out_specs=pl.BlockSpec((rows, D_MODEL), lambda j: (0, j)),
        out_shape=jax.ShapeDtypeStruct((rows, N_MOD * D_MODEL), F32),
        compiler_params=_params(1),
        name="mod",
    )(cc, w_ada, b_ada)


def _prenorm(x, g, shift, scale):
    return (_rms(x, g) * (1.0 + scale) + shift).astype(BF16)


def _rope(blk, cos, sin_signed, first_half):
    partner = jnp.where(first_half, pltpu.roll(blk, LANES - 32, 1), pltpu.roll(blk, 32, 1))
    return blk * cos + partner * sin_signed


def _in_lat_kernel(x_ref, mod_ref, g_ref, w_ref, cq_ref, sq_ref, ck_ref, sk_ref, *out_refs):
    h = _prenorm(x_ref[0], g_ref[...], mod_ref[0, 0:1, :], mod_ref[0, 1:2, :])
    lane = lax.broadcasted_iota(jnp.int32, (1, LANES), 1)
    first_half = (lane & 32) == 0
    for gi, o_ref in enumerate(out_refs):
        acc = jnp.dot(h, w_ref[:, gi * GROUP_W:(gi + 1) * GROUP_W], preferred_element_type=F32)
        if gi in (0, 1):
            cos, sin = (cq_ref[...], sq_ref[...]) if gi == 0 else (ck_ref[...], sk_ref[...])
            for hh in range(RET_HEADS):
                blk = acc[:, hh * LANES:(hh + 1) * LANES]
                o_ref[0, :, hh * LANES:(hh + 1) * LANES] = _rope(blk, cos, sin, first_half).astype(BF16)
        elif gi == 4:
            o_ref[0] = (acc * NA_DIM ** -0.5).astype(BF16)
        else:
            o_ref[0] = acc.astype(BF16)


_CTX_GROUPS = (1, 2, 5, 6)


def _in_ctx_kernel(x_ref, mod_ref, g_ref, w_ref, *out_refs):
    h = _prenorm(x_ref[0], g_ref[...], mod_ref[0, 0:1, :], mod_ref[0, 1:2, :])
    for gi, o_ref in zip(_CTX_GROUPS, out_refs):
        acc = jnp.dot(h, w_ref[:, gi * GROUP_W:(gi + 1) * GROUP_W], preferred_element_type=F32)
        o_ref[0] = acc.astype(BF16)


def _rope_tables():
    tok = np.arange(SEQ)
    n_freq = RET_DIM // 4
    inv = ROPE_BASE ** (-np.arange(n_freq, dtype=np.float64) / n_freq)
    ang_r = (tok // GRID_W)[:, None] * inv[None, :]
    ang_c = (tok % GRID_W)[:, None] * inv[None, :]
    cos = np.concatenate([np.cos(ang_r), np.cos(ang_r), np.cos(ang_c), np.cos(ang_c)], axis=-1)
    sin = np.concatenate([-np.sin(ang_r), np.sin(ang_r), -np.sin(ang_c), np.sin(ang_c)], axis=-1)
    qs = RET_DIM ** -0.5
    return [jnp.asarray(t, F32) for t in (cos * qs, sin * qs, cos, sin)]


def _in_proj_lat(x, mods, g_pre, w_in):
    batch = x.shape[0]
    tok = lambda b, i: (b, i, 0)
    tab = pl.BlockSpec((ROW_TILE, LANES), lambda b, i: (i, 0))
    out = jax.ShapeDtypeStruct((batch, SEQ, GROUP_W), BF16)
    return pl.pallas_call(
        _in_lat_kernel,
        grid=(batch, SEQ // ROW_TILE),
        in_specs=[pl.BlockSpec((1, ROW_TILE, D_MODEL), tok),
                  pl.BlockSpec((1, N_MOD, D_MODEL), lambda b, i: (b, 0, 0)),
                  pl.BlockSpec((1, D_MODEL), lambda b, i: (0, 0)),
                  _resident((D_MODEL, N_GROUPS * GROUP_W)),
                  tab, tab, tab, tab],
        out_specs=[pl.BlockSpec((1, ROW_TILE, GROUP_W), tok)] * N_GROUPS,
        out_shape=[out] * N_GROUPS,
        compiler_params=_params(2),
        name="in_lat",
    )(x, mods, g_pre, w_in, *_rope_tables())


def _in_proj_ctx(ctx, mods_ctx, g_pre, w_in):
    batch = ctx.shape[0]
    tok = lambda b: (b, 0, 0)
    out = jax.ShapeDtypeStruct((batch, CTX_LEN, GROUP_W), BF16)
    return pl.pallas_call(
        _in_ctx_kernel,
        grid=(batch,),
        in_specs=[pl.BlockSpec((1, CTX_LEN, D_MODEL), tok),
                  pl.BlockSpec((1, N_MOD, D_MODEL), lambda b: (0, 0, 0)),
                  pl.BlockSpec((1, D_MODEL), lambda b: (0, 0)),
                  _resident((D_MODEL, N_GROUPS * GROUP_W))],
        out_specs=[pl.BlockSpec((1, CTX_LEN, GROUP_W), tok)] * len(_CTX_GROUPS),
        out_shape=[out] * len(_CTX_GROUPS),
        compiler_params=_params(1),
        name="in_ctx",
    )(ctx, mods_ctx, g_pre, w_in)


def _ret_kernel(lg_ref, q_ref, k_ref, v_ref, g_ref, kc_ref, vc_ref, gn_ref, o_ref,
                acc_ref, dec_ref, wt_ref):
    head = pl.program_id(1)
    lgf = lg_ref[0, head]
    lgb = lg_ref[1, head]
    c = RET_CHUNK
    ii = lax.broadcasted_iota(jnp.int32, (c, c), 0)
    jj = lax.broadcasted_iota(jnp.int32, (c, c), 1)
    diff = (ii - jj).astype(F32)
    dec_ref[...] = jnp.exp(jnp.where(diff >= 0, lgf * diff, -lgb * diff))
    ri = lax.broadcasted_iota(jnp.int32, (c, LANES), 0).astype(F32)
    wt_ref[0] = jnp.exp(lgf * (ri + 1.0))
    wt_ref[1] = jnp.exp(lgb * (c - ri))
    wt_ref[2] = jnp.exp(lgf * (c - 1.0 - ri))
    wt_ref[3] = jnp.exp(lgb * ri)
    ones = jnp.ones((1, LANES), F32)
    chunk_f = jnp.exp(ones * (lgf * c))
    chunk_b = jnp.exp(ones * (lgb * c))

    def weighted(a, w):
        return (a.astype(F32) * wt_ref[w]).astype(BF16)

    def state_update(k, v, w):
        return lax.dot_general(weighted(k, w), v, _TN, preferred_element_type=F32)

    kc = kc_ref[0]
    vc = vc_ref[0]
    state_f = state_update(kc, vc, 2)
    state_b = state_update(kc, vc, 3)

    def fwd(n, state):
        rows = pl.ds(pl.multiple_of(n * c, c), c)
        q, k, v = q_ref[0, rows, :], k_ref[0, rows, :], v_ref[0, rows, :]
        s = lax.dot_general(q, k, _NT, preferred_element_type=F32) * dec_ref[...]
        inner = jnp.dot(s.astype(BF16), v, preferred_element_type=F32)
        cross = jnp.dot(weighted(q, 0), state.astype(BF16), preferred_element_type=F32)
        acc_ref[rows, :] = inner + cross
        return chunk_f * state + state_update(k, v, 2)

    lax.fori_loop(0, SEQ // c, fwd, state_f)

    gn = gn_ref[...]

    def bwd(t, state):
        n = SEQ // c - 1 - t
        rows = pl.ds(pl.multiple_of(n * c, c), c)
        q, k, v = q_ref[0, rows, :], k_ref[0, rows, :], v_ref[0, rows, :]
        o = acc_ref[rows, :] + jnp.dot(weighted(q, 1), state.astype(BF16), preferred_element_type=F32)
        d = o - jnp.mean(o, axis=-1, keepdims=True)
        y = d * lax.rsqrt(jnp.mean(d * d, axis=-1, keepdims=True) + NORM_EPS) * gn
        o_ref[0, rows, :] = (y * _silu(g_ref[0, rows, :].astype(F32))).astype(BF16)
        return chunk_b * state + state_update(k, v, 3)

    lax.fori_loop(0, SEQ // c, bwd, state_b)


def _retention(log_gammas, rq, rk, rv, rg, crk, crv, gn_w):
    batch = rq.shape[0]
    lat = pl.BlockSpec((1, SEQ, LANES), lambda b, h: (b, 0, h))
    cx = pl.BlockSpec((1, CTX_LEN, LANES), lambda b, h: (b, 0, h))
    return pl.pallas_call(
        _ret_kernel,
        grid=(batch, RET_HEADS),
        in_specs=[pl.BlockSpec(memory_space=pltpu.SMEM), lat, lat, lat, lat, cx, cx,
                  pl.BlockSpec((1, LANES), lambda b, h: (0, h))],
        out_specs=lat,
        out_shape=jax.ShapeDtypeStruct((batch, SEQ, RET_WIDTH), BF16),
        scratch_shapes=[pltpu.VMEM((SEQ, LANES), F32),
                        pltpu.VMEM((RET_CHUNK, RET_CHUNK), F32),
                        pltpu.VMEM((4, RET_CHUNK, LANES), F32)],
        compiler_params=_params(2),
        name="ret",
    )(log_gammas, rq, rk, rv, rg, crk, crv, gn_w)


def _na_bias_table(rpb):
    rep_rows = np.array([0, 1, 2, 3, NA_KH // 2, GRID_ROWS - 3, GRID_ROWS - 2, GRID_ROWS - 1])
    band_rows = np.clip(rep_rows - NA_KH // 2, 0, GRID_ROWS - NA_KH)[:, None] + np.arange(NA_KH)[None, :]
    dr = band_rows - rep_rows[:, None] + (NA_KH - 1)
    col = np.arange(GRID_W)
    col_start = np.clip(col - NA_KW // 2, 0, GRID_W - NA_KW)
    valid = (col[None, :] >= col_start[:, None]) & (col[None, :] < col_start[:, None] + NA_KW)
    dc = np.clip(col[None, :] - col[:, None] + (NA_KW - 1), 0, 2 * NA_KW - 2)
    bias = rpb[:, dr[:, :, None, None], dc[None, None, :, :]].astype(F32)
    bias = jnp.where(valid[None, None, None], bias, NEG_INF)
    bias = bias.transpose(0, 1, 3, 2, 4).reshape(NA_HEADS // 2, 2, N_PATTERNS, GRID_W, NA_BAND)
    return bias.transpose(0, 2, 1, 3, 4).reshape(NA_HEADS // 2, N_PATTERNS, 2 * GRID_W, NA_BAND)


def _na_kernel(q_ref, k_ref, v_ref, kc_ref, vc_ref, b_ref, o_ref):
    lane = lax.broadcasted_iota(jnp.int32, (1, LANES), 1)
    low = lane < NA_DIM
    kc = kc_ref[0]
    vc = vc_ref[0]
    half = NA_KH // 2

    def row(r, carry):
        start = jnp.clip(r - half, 0, GRID_ROWS - NA_KH)
        pattern = jnp.where(r < half, r, jnp.where(r > GRID_ROWS - half, r - (GRID_ROWS - NA_KH), half))
        qrows = pl.ds(pl.multiple_of(r * GRID_W, GRID_W), GRID_W)
        band = pl.ds(pl.multiple_of(start * GRID_W, GRID_W), NA_BAND)
        q = q_ref[0, qrows, :]
        zero = jnp.zeros_like(q)
        q2 = jnp.concatenate([jnp.where(low, q, zero), jnp.where(low, zero, q)], axis=0)
        kb = k_ref[0, band, :]
        vb = v_ref[0, band, :]
        s_loc = lax.dot_general(q2, kb, _NT, preferred_element_type=F32) + b_ref[0, pattern]
        s_ctx = lax.dot_general(q2, kc, _NT, preferred_element_type=F32)
        m = jnp.maximum(jnp.max(s_loc, axis=-1, keepdims=True), jnp.max(s_ctx, axis=-1, keepdims=True))
        p_loc = jnp.exp(s_loc - m)
        p_ctx = jnp.exp(s_ctx - m)
        denom = jnp.sum(p_loc, axis=-1, keepdims=True) + jnp.sum(p_ctx, axis=-1, keepdims=True)
        o = (jnp.dot(p_loc.astype(BF16), vb, preferred_element_type=F32)
             + jnp.dot(p_ctx.astype(BF16), vc, preferred_element_type=F32)) * (1.0 / denom)
        o_ref[0, qrows, :] = jnp.where(low, o[:GRID_W], o[GRID_W:]).astype(BF16)
        return carry

    lax.fori_loop(0, GRID_ROWS, row, 0)


def _neighbourhood_attention(nq, nk, nv, cnk, cnv, bias):
    batch = nq.shape[0]
    lat = pl.BlockSpec((1, SEQ, LANES), lambda b, p: (b, 0, p))
    cx = pl.BlockSpec((1, CTX_LEN, LANES), lambda b, p: (b, 0, p))
    return pl.pallas_call(
        _na_kernel,
        grid=(batch, NA_HEADS // 2),
        in_specs=[lat, lat, lat, cx, cx,
                  pl.BlockSpec((1, N_PATTERNS, 2 * GRID_W, NA_BAND), lambda b, p: (p, 0, 0, 0))],
        out_specs=lat,
        out_shape=jax.ShapeDtypeStruct((batch, SEQ, NA_WIDTH), BF16),
        compiler_params=_params(2),
        name="na",
    )(nq, nk, nv, cnk, cnv, bias)


def _out_kernel(x_ref, ret_ref, na_ref, mod_ref, gpm_ref, gpre_ref, gpost_ref, wo_ref, w1_ref, w2_ref, o_ref):
    gt1, sh2, sc2, gt2 = (mod_ref[0, i:i + 1, :] for i in (2, 3, 4, 5))
    mix = (jnp.dot(ret_ref[0], wo_ref[0:RET_WIDTH, :], preferred_element_type=F32)
           + jnp.dot(na_ref[0], wo_ref[RET_WIDTH:, :], preferred_element_type=F32))
    x1 = x_ref[0] + gt1 * _rms(mix, gpm_ref[...])
    h2 = _prenorm(x1, gpre_ref[...], sh2, sc2)
    acc = jnp.zeros((ROW_TILE, D_MODEL), F32)
    for ci in range(D_FF // FF_CHUNK):
        cols = slice(ci * FF_CHUNK, (ci + 1) * FF_CHUNK)
        a = jnp.maximum(jnp.dot(h2, w1_ref[:, cols], preferred_element_type=F32), 0.0)
        acc = acc + jnp.dot((a * a).astype(BF16), w2_ref[cols, :], preferred_element_type=F32)
    o_ref[0] = x1 + gt2 * _rms(acc, gpost_ref[...])


def _out_proj_mlp(x, ret_lat, na_lat, mods, g_post_mix, g_pre_mlp, g_post_mlp, w_out, w_mlp1, w_mlp2):
    batch = x.shape[0]
    tok = lambda b, i: (b, i, 0)
    gain = pl.BlockSpec((1, D_MODEL), lambda b, i: (0, 0))
    return pl.pallas_call(
        _out_kernel,
        grid=(batch, SEQ // ROW_TILE),
        in_specs=[pl.BlockSpec((1, ROW_TILE, D_MODEL), tok),
                  pl.BlockSpec((1, ROW_TILE, RET_WIDTH), tok),
                  pl.BlockSpec((1, ROW_TILE, NA_WIDTH), tok),
                  pl.BlockSpec((1, N_MOD, D_MODEL), lambda b, i: (b, 0, 0)),
                  gain, gain, gain,
                  _resident((RET_WIDTH + NA_WIDTH, D_MODEL)),
                  _resident((D_MODEL, D_FF)),
                  _resident((D_FF, D_MODEL))],
        out_specs=pl.BlockSpec((1, ROW_TILE, D_MODEL), tok),
        out_shape=jax.ShapeDtypeStruct(x.shape, F32),
        compiler_params=_params(2),
        name="out_mlp",
    )(x, ret_lat, na_lat, mods, g_post_mix, g_pre_mlp, g_post_mlp, w_out, w_mlp1, w_mlp2)


def kernel(x, c, ctx, c_ctx, w_ada, b_ada, g_pre_mix, g_post_mix, g_pre_mlp, g_post_mlp,
           w_in, ret_decay, ret_gn, na_rpb, w_out, w_mlp1, w_mlp2):
    assert w_in.shape[0] == 1, "single-layer block: the context stream is never updated"
    batch = x.shape[0]
    pad = (-(batch + 1)) % 8
    cc = jnp.concatenate([c, c_ctx[None, :], jnp.zeros((pad, D_MODEL), F32)], axis=0)
    mods = _modulations(cc, w_ada[0], b_ada)
    mods_lat = mods[:batch].reshape(batch, N_MOD, D_MODEL)
    mods_ctx = mods[batch:batch + 1].reshape(1, N_MOD, D_MODEL)

    w_in_b = w_in[0].astype(BF16)
    rq, rk, rv, rg, nq, nk, nv = _in_proj_lat(x, mods_lat, g_pre_mix, w_in_b)
    crk, crv, cnk, cnv = _in_proj_ctx(ctx, mods_ctx, g_pre_mix, w_in_b)

    log_gammas = jax.nn.log_sigmoid(ret_decay[0].astype(F32))
    ret_lat = _retention(log_gammas, rq, rk, rv, rg, crk, crv, ret_gn)
    na_lat = _neighbourhood_attention(nq, nk, nv, cnk, cnv, _na_bias_table(na_rpb[0]))

    return _out_proj_mlp(x, ret_lat, na_lat, mods_lat, g_post_mix, g_pre_mlp, g_post_mlp,
                         w_out[0].astype(BF16), w_mlp1[0].astype(BF16), w_mlp2[0].astype(BF16))
```

```python
import numpy as np
import jax
import jax.numpy as jnp
from jax import lax
from jax.experimental import pallas as pl
from jax.experimental.pallas import tpu as pltpu

D_MODEL = 1024
SEQ = 2048
CTX_LEN = 256
GRID_W = 64
GRID_ROWS = SEQ // GRID_W
RET_HEADS = 4
RET_DIM = 128
RET_WIDTH = RET_HEADS * RET_DIM
NA_HEADS = 8
NA_DIM = 64
NA_WIDTH = NA_HEADS * NA_DIM
NA_KH = 8
NA_KW = 16
N_GROUPS = 7
GROUP_W = 512
D_FF = 4 * D_MODEL
ROPE_BASE = 10000.0
NORM_EPS = 1e-6
N_MOD = 6
NEG_INF = -1e30

LANES = 128
ROW_TILE = 512
FF_CHUNK = 1024
RET_CHUNK = 256
NA_BAND = NA_KH * GRID_W
N_PATTERNS = 8
VMEM_LIMIT = 56 * 1024 * 1024

F32 = jnp.float32
BF16 = jnp.bfloat16
_NT = (((1,), (1,)), ((), ()))
_TN = (((0,), (0,)), ((), ()))


def _silu(x):
    return x * (1.0 / (1.0 + jnp.exp(-x)))


def _rms(x, g):
    return x * lax.rsqrt(jnp.mean(x * x, axis=-1, keepdims=True) + NORM_EPS) * g


def _params(n_axes):
    return pltpu.CompilerParams(dimension_semantics=("arbitrary",) * n_axes,
                                vmem_limit_bytes=VMEM_LIMIT)


def _resident(shape):
    nd = len(shape)
    return pl.BlockSpec(shape, lambda *_: (0,) * nd, pipeline_mode=pl.Buffered(1))


def _mod_kernel(c_ref, w_ref, b_ref, o_ref):
    a = _silu(c_ref[...]).astype(BF16)
    o_ref[...] = jnp.dot(a, w_ref[...].astype(BF16), preferred_element_type=F32) + b_ref[...]


def _modulations(cc, w_ada, b_ada):
    rows = cc.shape[0]
    return pl.pallas_call(
        _mod_kernel,
        grid=(N_MOD,),
        in_specs=[pl.BlockSpec((rows, D_MODEL), lambda j: (0, 0)),
                  pl.BlockSpec((D_MODEL, D_MODEL), lambda j: (0, j)),
                  pl.BlockSpec((1, D_MODEL), lambda j: (0, j))],
        out_specs=pl.BlockSpec((rows, D_MODEL), lambda j: (0, j)),
        out_shape=jax.ShapeDtypeStruct((rows, N_MOD * D_MODEL), F32),
        compiler_params=_params(1),
        name="mod",
    )(cc, w_ada, b_ada)


def _prenorm(x, g, shift, scale):
    return (_rms(x, g) * (1.0 + scale) + shift).astype(BF16)


def _rope(blk, cos, sin_signed, first_half):
    partner = jnp.where(first_half, pltpu.roll(blk, LANES - 32, 1), pltpu.roll(blk, 32, 1))
    return blk * cos + partner * sin_signed


def _in_lat_kernel(x_ref, mod_ref, g_ref, w_ref, cq_ref, sq_ref, ck_ref, sk_ref, *out_refs):
    h = _prenorm(x_ref[0], g_ref[...], mod_ref[0, 0:1, :], mod_ref[0, 1:2, :])
    lane = lax.broadcasted_iota(jnp.int32, (1, LANES), 1)
    first_half = (lane & 32) == 0
    for gi, o_ref in enumerate(out_refs):
        acc = jnp.dot(h, w_ref[:, gi * GROUP_W:(gi + 1) * GROUP_W], preferred_element_type=F32)
        if gi in (0, 1):
            cos, sin = (cq_ref[...], sq_ref[...]) if gi == 0 else (ck_ref[...], sk_ref[...])
            for hh in range(RET_HEADS):
                blk = acc[:, hh * LANES:(hh + 1) * LANES]
                o_ref[0, :, hh * LANES:(hh + 1) * LANES] = _rope(blk, cos, sin, first_half).astype(BF16)
        elif gi == 4:
            o_ref[0] = (acc * NA_DIM ** -0.5).astype(BF16)
        else:
            o_ref[0] = acc.astype(BF16)


_CTX_GROUPS = (1, 2, 5, 6)


def _in_ctx_kernel(x_ref, mod_ref, g_ref, w_ref, *out_refs):
    h = _prenorm(x_ref[0], g_ref[...], mod_ref[0, 0:1, :], mod_ref[0, 1:2, :])
    for gi, o_ref in zip(_CTX_GROUPS, out_refs):
        acc = jnp.dot(h, w_ref[:, gi * GROUP_W:(gi + 1) * GROUP_W], preferred_element_type=F32)
        o_ref[0] = acc.astype(BF16)


def _rope_tables():
    tok = np.arange(SEQ)
    n_freq = RET_DIM // 4
    inv = ROPE_BASE ** (-np.arange(n_freq, dtype=np.float64) / n_freq)
    ang_r = (tok // GRID_W)[:, None] * inv[None, :]
    ang_c = (tok % GRID_W)[:, None] * inv[None, :]
    cos = np.concatenate([np.cos(ang_r), np.cos(ang_r), np.cos(ang_c), np.cos(ang_c)], axis=-1)
    sin = np.concatenate([-np.sin(ang_r), np.sin(ang_r), -np.sin(ang_c), np.sin(ang_c)], axis=-1)
    qs = RET_DIM ** -0.5
    return [jnp.asarray(t, F32) for t in (cos * qs, sin * qs, cos, sin)]


def _in_proj_lat(x, mods, g_pre, w_in):
    batch = x.shape[0]
    tok = lambda b, i: (b, i, 0)
    tab = pl.BlockSpec((ROW_TILE, LANES), lambda b, i: (i, 0))
    out = jax.ShapeDtypeStruct((batch, SEQ, GROUP_W), BF16)
    return pl.pallas_call(
        _in_lat_kernel,
        grid=(batch, SEQ // ROW_TILE),
        in_specs=[pl.BlockSpec((1, ROW_TILE, D_MODEL), tok),
                  pl.BlockSpec((1, N_MOD, D_MODEL), lambda b, i: (b, 0, 0)),
                  pl.BlockSpec((1, D_MODEL), lambda b, i: (0, 0)),
                  _resident((D_MODEL, N_GROUPS * GROUP_W)),
                  tab, tab, tab, tab],
        out_specs=[pl.BlockSpec((1, ROW_TILE, GROUP_W), tok)] * N_GROUPS,
        out_shape=[out] * N_GROUPS,
        compiler_params=_params(2),
        name="in_lat",
    )(x, mods, g_pre, w_in, *_rope_tables())


def _in_proj_ctx(ctx, mods_ctx, g_pre, w_in):
    batch = ctx.shape[0]
    tok = lambda b: (b, 0, 0)
    out = jax.ShapeDtypeStruct((batch, CTX_LEN, GROUP_W), BF16)
    return pl.pallas_call(
        _in_ctx_kernel,
        grid=(batch,),
        in_specs=[pl.BlockSpec((1, CTX_LEN, D_MODEL), tok),
                  pl.BlockSpec((1, N_MOD, D_MODEL), lambda b: (0, 0, 0)),
                  pl.BlockSpec((1, D_MODEL), lambda b: (0, 0)),
                  _resident((D_MODEL, N_GROUPS * GROUP_W))],
        out_specs=[pl.BlockSpec((1, CTX_LEN, GROUP_W), tok)] * len(_CTX_GROUPS),
        out_shape=[out] * len(_CTX_GROUPS),
        compiler_params=_params(1),
        name="in_ctx",
    )(ctx, mods_ctx, g_pre, w_in)


def _ret_kernel(lg_ref, q_ref, k_ref, v_ref, g_ref, kc_ref, vc_ref, gn_ref, o_ref,
                dec_ref, wt_ref, upd_ref, st_ref):
    head = pl.program_id(0)
    lgf = lg_ref[0, head]
    lgb = lg_ref[1, head]
    c = RET_CHUNK
    n_chunks = SEQ // c

    @pl.when(pl.program_id(1) == 0)
    def _tables():
        ii = lax.broadcasted_iota(jnp.int32, (c, c), 0)
        jj = lax.broadcasted_iota(jnp.int32, (c, c), 1)
        diff = (ii - jj).astype(F32)
        dec_ref[...] = jnp.exp(jnp.where(diff >= 0, lgf * diff, -lgb * diff))
        ri = lax.broadcasted_iota(jnp.int32, (c, LANES), 0).astype(F32)
        wt_ref[0] = jnp.exp(lgf * (ri + 1.0))
        wt_ref[1] = jnp.exp(lgb * (c - ri))
        wt_ref[2] = jnp.exp(lgf * (c - 1.0 - ri))
        wt_ref[3] = jnp.exp(lgb * ri)

    ones = jnp.ones((1, LANES), F32)
    chunk_f = jnp.exp(ones * (lgf * c))
    chunk_b = jnp.exp(ones * (lgb * c))

    def both(a, wf, wb):
        af = a.astype(F32)
        return jnp.concatenate([(af * wt_ref[wf]).astype(BF16), (af * wt_ref[wb]).astype(BF16)], axis=1)

    def update(k, v):
        return lax.dot_general(both(k, 2, 3), v, _TN, preferred_element_type=F32)

    def chunk_rows(n):
        return pl.ds(pl.multiple_of(n * c, c), c)

    def phase1(n, carry):
        rows = chunk_rows(n)
        upd_ref[n] = update(k_ref[0, rows, :], v_ref[0, rows, :])
        return carry

    lax.fori_loop(0, n_chunks, phase1, 0, unroll=2)

    ctx_upd = update(kc_ref[0], vc_ref[0])
    state = ctx_upd[:RET_DIM]
    for n in range(n_chunks):
        st_ref[n, 0:RET_DIM, :] = state.astype(BF16)
        state = chunk_f * state + upd_ref[n, 0:RET_DIM, :]
    state = ctx_upd[RET_DIM:]
    for n in reversed(range(n_chunks)):
        st_ref[n, RET_DIM:, :] = state.astype(BF16)
        state = chunk_b * state + upd_ref[n, RET_DIM:, :]

    gn = gn_ref[...]

    def phase3(n, carry):
        rows = chunk_rows(n)
        q, k, v = q_ref[0, rows, :], k_ref[0, rows, :], v_ref[0, rows, :]
        s = lax.dot_general(q, k, _NT, preferred_element_type=F32) * dec_ref[...]
        o = (jnp.dot(s.astype(BF16), v, preferred_element_type=F32)
             + jnp.dot(both(q, 0, 1), st_ref[n], preferred_element_type=F32))
        d = o - jnp.mean(o, axis=-1, keepdims=True)
        y = d * lax.rsqrt(jnp.mean(d * d, axis=-1, keepdims=True) + NORM_EPS) * gn
        o_ref[0, rows, :] = (y * _silu(g_ref[0, rows, :].astype(F32))).astype(BF16)
        return carry

    lax.fori_loop(0, n_chunks, phase3, 0, unroll=2)


def _retention(log_gammas, rq, rk, rv, rg, crk, crv, gn_w):
    assert CTX_LEN == RET_CHUNK
    batch = rq.shape[0]
    n_chunks = SEQ // RET_CHUNK
    lat = pl.BlockSpec((1, SEQ, LANES), lambda h, b: (b, 0, h))
    cx = pl.BlockSpec((1, CTX_LEN, LANES), lambda h, b: (b, 0, h))
    return pl.pallas_call(
        _ret_kernel,
        grid=(RET_HEADS, batch),
        in_specs=[pl.BlockSpec(memory_space=pltpu.SMEM), lat, lat, lat, lat, cx, cx,
                  pl.BlockSpec((1, LANES), lambda h, b: (0, h))],
        out_specs=lat,
        out_shape=jax.ShapeDtypeStruct((batch, SEQ, RET_WIDTH), BF16),
        scratch_shapes=[pltpu.VMEM((RET_CHUNK, RET_CHUNK), F32),
                        pltpu.VMEM((4, RET_CHUNK, LANES), F32),
                        pltpu.VMEM((n_chunks, 2 * RET_DIM, RET_DIM), F32),
                        pltpu.VMEM((n_chunks, 2 * RET_DIM, RET_DIM), BF16)],
        compiler_params=_params(2),
        name="ret",
    )(log_gammas, rq, rk, rv, rg, crk, crv, gn_w)


def _na_bias_table(rpb):
    col = np.arange(GRID_W)
    col_start = np.clip(col - NA_KW // 2, 0, GRID_W - NA_KW)
    valid = (col[None, :] >= col_start[:, None]) & (col[None, :] < col_start[:, None] + NA_KW)
    code = np.where(valid, col[None, :] - col[:, None] + (NA_KW - 1), -1)
    n_dc = 2 * NA_KW - 1
    rows = jnp.stack([rpb[:, N_PATTERNS - 1 - t:N_PATTERNS - 1 - t + NA_KH, :] for t in range(N_PATTERNS)],
                     axis=1).astype(F32)
    bias = jnp.full((NA_HEADS, N_PATTERNS, NA_KH, GRID_W, GRID_W), NEG_INF, F32)
    for b in range(n_dc):
        bias = jnp.where((code == b)[None, None, None], rows[..., b][..., None, None], bias)
    bias = bias.transpose(0, 1, 3, 2, 4).reshape(NA_HEADS // 2, 2, N_PATTERNS, GRID_W, NA_BAND)
    return bias.transpose(0, 2, 1, 3, 4).reshape(NA_HEADS // 2, N_PATTERNS, 2 * GRID_W, NA_BAND)


def _na_kernel(q_ref, k_ref, v_ref, kc_ref, vc_ref, b_ref, o_ref):
    lane = lax.broadcasted_iota(jnp.int32, (1, LANES), 1)
    low = lane < NA_DIM
    kc = kc_ref[0]
    vc = vc_ref[0]
    half = NA_KH // 2

    def row(r, carry):
        start = jnp.clip(r - half, 0, GRID_ROWS - NA_KH)
        pattern = jnp.where(r < half, r, jnp.where(r > GRID_ROWS - half, r - (GRID_ROWS - NA_KH), half))
        qrows = pl.ds(pl.multiple_of(r * GRID_W, GRID_W), GRID_W)
        band = pl.ds(pl.multiple_of(start * GRID_W, GRID_W), NA_BAND)
        q = q_ref[0, qrows, :]
        zero = jnp.zeros_like(q)
        q2 = jnp.concatenate([jnp.where(low, q, zero), jnp.where(low, zero, q)], axis=0)
        kb = k_ref[0, band, :]
        vb = v_ref[0, band, :]
        s_loc = lax.dot_general(q2, kb, _NT, preferred_element_type=F32) + b_ref[0, pattern]
        s_ctx = lax.dot_general(q2, kc, _NT, preferred_element_type=F32)
        m = jnp.maximum(jnp.max(s_loc, axis=-1, keepdims=True), jnp.max(s_ctx, axis=-1, keepdims=True))
        p_loc = jnp.exp(s_loc - m)
        p_ctx = jnp.exp(s_ctx - m)
        denom = jnp.sum(p_loc, axis=-1, keepdims=True) + jnp.sum(p_ctx, axis=-1, keepdims=True)
        o = (jnp.dot(p_loc.astype(BF16), vb, preferred_element_type=F32)
             + jnp.dot(p_ctx.astype(BF16), vc, preferred_element_type=F32)) * (1.0 / denom)
        o_ref[0, qrows, :] = jnp.where(low, o[:GRID_W], o[GRID_W:]).astype(BF16)
        return carry

    lax.fori_loop(0, GRID_ROWS, row, 0, unroll=2)


def _neighbourhood_attention(nq, nk, nv, cnk, cnv, bias):
    batch = nq.shape[0]
    lat = pl.BlockSpec((1, SEQ, LANES), lambda b, p: (b, 0, p))
    cx = pl.BlockSpec((1, CTX_LEN, LANES), lambda b, p: (b, 0, p))
    return pl.pallas_call(
        _na_kernel,
        grid=(batch, NA_HEADS // 2),
        in_specs=[lat, lat, lat, cx, cx,
                  pl.BlockSpec((1, N_PATTERNS, 2 * GRID_W, NA_BAND), lambda b, p: (p, 0, 0, 0))],
        out_specs=lat,
        out_shape=jax.ShapeDtypeStruct((batch, SEQ, NA_WIDTH), BF16),
        compiler_params=_params(2),
        name="na",
    )(nq, nk, nv, cnk, cnv, bias)


def _out_kernel(x_ref, ret_ref, na_ref, mod_ref, gpm_ref, gpre_ref, gpost_ref, wo_ref, w1_ref, w2_ref, o_ref):
    gt1, sh2, sc2, gt2 = (mod_ref[0, i:i + 1, :] for i in (2, 3, 4, 5))
    mix = (jnp.dot(ret_ref[0], wo_ref[0:RET_WIDTH, :], preferred_element_type=F32)
           + jnp.dot(na_ref[0], wo_ref[RET_WIDTH:, :], preferred_element_type=F32))
    x1 = x_ref[0] + gt1 * _rms(mix, gpm_ref[...])
    h2 = _prenorm(x1, gpre_ref[...], sh2, sc2)
    acc = jnp.zeros((ROW_TILE, D_MODEL), F32)
    for ci in range(D_FF // FF_CHUNK):
        cols = slice(ci * FF_CHUNK, (ci + 1) * FF_CHUNK)
        a = jnp.maximum(jnp.dot(h2, w1_ref[:, cols], preferred_element_type=F32), 0.0)
        acc = acc + jnp.dot((a * a).astype(BF16), w2_ref[cols, :], preferred_element_type=F32)
    o_ref[0] = x1 + gt2 * _rms(acc, gpost_ref[...])


def _out_proj_mlp(x, ret_lat, na_lat, mods, g_post_mix, g_pre_mlp, g_post_mlp, w_out, w_mlp1, w_mlp2):
    batch = x.shape[0]
    tok = lambda b, i: (b, i, 0)
    gain = pl.BlockSpec((1, D_MODEL), lambda b, i: (0, 0))
    return pl.pallas_call(
        _out_kernel,
        grid=(batch, SEQ // ROW_TILE),
        in_specs=[pl.BlockSpec((1, ROW_TILE, D_MODEL), tok),
                  pl.BlockSpec((1, ROW_TILE, RET_WIDTH), tok),
                  pl.BlockSpec((1, ROW_TILE, NA_WIDTH), tok),
                  pl.BlockSpec((1, N_MOD, D_MODEL), lambda b, i: (b, 0, 0)),
                  gain, gain, gain,
                  _resident((RET_WIDTH + NA_WIDTH, D_MODEL)),
                  _resident((D_MODEL, D_FF)),
                  _resident((D_FF, D_MODEL))],
        out_specs=pl.BlockSpec((1, ROW_TILE, D_MODEL), tok),
        out_shape=jax.ShapeDtypeStruct(x.shape, F32),
        compiler_params=_params(2),
        name="out_mlp",
    )(x, ret_lat, na_lat, mods, g_post_mix, g_pre_mlp, g_post_mlp, w_out, w_mlp1, w_mlp2)


def kernel(x, c, ctx, c_ctx, w_ada, b_ada, g_pre_mix, g_post_mix, g_pre_mlp, g_post_mlp,
           w_in, ret_decay, ret_gn, na_rpb, w_out, w_mlp1, w_mlp2):
    assert w_in.shape[0] == 1, "single-layer block: the context stream is never updated"
    batch = x.shape[0]
    pad = (-(batch + 1)) % 8
    cc = jnp.concatenate([c, c_ctx[None, :], jnp.zeros((pad, D_MODEL), F32)], axis=0)
    mods = _modulations(cc, w_ada[0], b_ada)
    mods_lat = mods[:batch].reshape(batch, N_MOD, D_MODEL)
    mods_ctx = mods[batch:batch + 1].reshape(1, N_MOD, D_MODEL)

    w_in_b = w_in[0].astype(BF16)
    rq, rk, rv, rg, nq, nk, nv = _in_proj_lat(x, mods_lat, g_pre_mix, w_in_b)
    crk, crv, cnk, cnv = _in_proj_ctx(ctx, mods_ctx, g_pre_mix, w_in_b)

    log_gammas = jax.nn.log_sigmoid(ret_decay[0].astype(F32))
    ret_lat = _retention(log_gammas, rq, rk, rv, rg, crk, crv, ret_gn)
    na_lat = _neighbourhood_attention(nq, nk, nv, cnk, cnv, _na_bias_table(na_rpb[0]))

    return _out_proj_mlp(x, ret_lat, na_lat, mods_lat, g_post_mix, g_pre_mlp, g_post_mlp,
                         w_out[0].astype(BF16), w_mlp1[0].astype(BF16), w_mlp2[0].astype(BF16))
```

```python
import numpy as np
import jax
import jax.numpy as jnp
from jax import lax
from jax.experimental import pallas as pl
from jax.experimental.pallas import tpu as pltpu

D_MODEL = 1024
SEQ = 2048
CTX_LEN = 256
GRID_W = 64
GRID_ROWS = SEQ // GRID_W
RET_HEADS = 4
RET_DIM = 128
RET_WIDTH = RET_HEADS * RET_DIM
NA_HEADS = 8
NA_DIM = 64
NA_WIDTH = NA_HEADS * NA_DIM
NA_KH = 8
NA_KW = 16
N_GROUPS = 7
GROUP_W = 512
_NA_K_GROUP = 5
D_FF = 4 * D_MODEL
ROPE_BASE = 10000.0
NORM_EPS = 1e-6
N_MOD = 6
NEG_INF = -1e30

LANES = 128
ROW_TILE = 512
FF_CHUNK = 1024
RET_CHUNK = 256
NA_BAND = NA_KH * GRID_W
N_PATTERNS = 8
VMEM_LIMIT = 56 * 1024 * 1024

F32 = jnp.float32
BF16 = jnp.bfloat16
_NT = (((1,), (1,)), ((), ()))
_TN = (((0,), (0,)), ((), ()))


def _silu(x):
    return x * (1.0 / (1.0 + jnp.exp(-x)))


def _rms(x, g):
    return x * lax.rsqrt(jnp.mean(x * x, axis=-1, keepdims=True) + NORM_EPS) * g


def _params(n_axes):
    return pltpu.CompilerParams(dimension_semantics=("arbitrary",) * n_axes,
                                vmem_limit_bytes=VMEM_LIMIT)


def _resident(shape):
    nd = len(shape)
    return pl.BlockSpec(shape, lambda *_: (0,) * nd, pipeline_mode=pl.Buffered(1))


def _mod_kernel(c_ref, w_ref, b_ref, o_ref):
    a = _silu(c_ref[...]).astype(BF16)
    o_ref[...] = jnp.dot(a, w_ref[...].astype(BF16), preferred_element_type=F32) + b_ref[...]


def _modulations(cc, w_ada, b_ada):
    rows = cc.shape[0]
    return pl.pallas_call(
        _mod_kernel,
        grid=(N_MOD,),
        in_specs=[pl.BlockSpec((rows, D_MODEL), lambda j: (0, 0)),
                  pl.BlockSpec((D_MODEL, D_MODEL), lambda j: (0, j)),
                  pl.BlockSpec((1, D_MODEL), lambda j: (0, j))],
        out_specs=pl.BlockSpec((rows, D_MODEL), lambda j: (0, j)),
        out_shape=jax.ShapeDtypeStruct((rows, N_MOD * D_MODEL), F32),
        compiler_params=_params(1),
        name="mod",
    )(cc, w_ada, b_ada)


def _prenorm(x, g, shift, scale):
    return (_rms(x, g) * (1.0 + scale) + shift).astype(BF16)


def _rope(blk, cos, sin_signed, first_half):
    partner = jnp.where(first_half, pltpu.roll(blk, LANES - 32, 1), pltpu.roll(blk, 32, 1))
    return blk * cos + partner * sin_signed


def _in_lat_kernel(x_ref, mod_ref, g_ref, w_ref, cq_ref, sq_ref, ck_ref, sk_ref, *out_refs):
    h = _prenorm(x_ref[0], g_ref[...], mod_ref[0, 0:1, :], mod_ref[0, 1:2, :])
    lane = lax.broadcasted_iota(jnp.int32, (1, LANES), 1)
    first_half = (lane & 32) == 0
    for gi, o_ref in enumerate(out_refs):
        acc = jnp.dot(h, w_ref[:, gi * GROUP_W:(gi + 1) * GROUP_W], preferred_element_type=F32)
        if gi in (0, 1):
            cos, sin = (cq_ref[...], sq_ref[...]) if gi == 0 else (ck_ref[...], sk_ref[...])
            for hh in range(RET_HEADS):
                blk = acc[:, hh * LANES:(hh + 1) * LANES]
                o_ref[0, :, hh * LANES:(hh + 1) * LANES] = _rope(blk, cos, sin, first_half).astype(BF16)
        elif gi == 4:
            o_ref[0] = (acc * NA_DIM ** -0.5).astype(BF16)
        elif gi == _NA_K_GROUP:
            o_ref[0] = acc.T.astype(BF16)
        else:
            o_ref[0] = acc.astype(BF16)


_CTX_GROUPS = (1, 2, _NA_K_GROUP, 6)


def _in_ctx_kernel(x_ref, mod_ref, g_ref, w_ref, *out_refs):
    h = _prenorm(x_ref[0], g_ref[...], mod_ref[0, 0:1, :], mod_ref[0, 1:2, :])
    for gi, o_ref in zip(_CTX_GROUPS, out_refs):
        acc = jnp.dot(h, w_ref[:, gi * GROUP_W:(gi + 1) * GROUP_W], preferred_element_type=F32)
        o_ref[0] = (acc.T if gi == _NA_K_GROUP else acc).astype(BF16)


def _rope_tables():
    tok = np.arange(SEQ)
    n_freq = RET_DIM // 4
    inv = ROPE_BASE ** (-np.arange(n_freq, dtype=np.float64) / n_freq)
    ang_r = (tok // GRID_W)[:, None] * inv[None, :]
    ang_c = (tok % GRID_W)[:, None] * inv[None, :]
    cos = np.concatenate([np.cos(ang_r), np.cos(ang_r), np.cos(ang_c), np.cos(ang_c)], axis=-1)
    sin = np.concatenate([-np.sin(ang_r), np.sin(ang_r), -np.sin(ang_c), np.sin(ang_c)], axis=-1)
    qs = RET_DIM ** -0.5
    return [jnp.asarray(t, F32) for t in (cos * qs, sin * qs, cos, sin)]


def _in_proj_lat(x, mods, g_pre, w_in):
    batch = x.shape[0]
    tok = lambda b, i: (b, i, 0)
    tab = pl.BlockSpec((ROW_TILE, LANES), lambda b, i: (i, 0))
    out_shapes = [jax.ShapeDtypeStruct((batch, SEQ, GROUP_W), BF16)] * N_GROUPS
    out_specs = [pl.BlockSpec((1, ROW_TILE, GROUP_W), tok)] * N_GROUPS
    out_shapes[_NA_K_GROUP] = jax.ShapeDtypeStruct((batch, GROUP_W, SEQ), BF16)
    out_specs[_NA_K_GROUP] = pl.BlockSpec((1, GROUP_W, ROW_TILE), lambda b, i: (b, 0, i))
    return pl.pallas_call(
        _in_lat_kernel,
        grid=(batch, SEQ // ROW_TILE),
        in_specs=[pl.BlockSpec((1, ROW_TILE, D_MODEL), tok),
                  pl.BlockSpec((1, N_MOD, D_MODEL), lambda b, i: (b, 0, 0)),
                  pl.BlockSpec((1, D_MODEL), lambda b, i: (0, 0)),
                  _resident((D_MODEL, N_GROUPS * GROUP_W)),
                  tab, tab, tab, tab],
        out_specs=out_specs,
        out_shape=out_shapes,
        compiler_params=_params(2),
        name="in_lat",
    )(x, mods, g_pre, w_in, *_rope_tables())


def _in_proj_ctx(ctx, mods_ctx, g_pre, w_in):
    batch = ctx.shape[0]
    tok = lambda b: (b, 0, 0)
    out_shapes = [jax.ShapeDtypeStruct((batch, GROUP_W, CTX_LEN) if gi == _NA_K_GROUP else (batch, CTX_LEN, GROUP_W),
                                       BF16) for gi in _CTX_GROUPS]
    out_specs = [pl.BlockSpec((1, GROUP_W, CTX_LEN) if gi == _NA_K_GROUP else (1, CTX_LEN, GROUP_W), tok)
                 for gi in _CTX_GROUPS]
    return pl.pallas_call(
        _in_ctx_kernel,
        grid=(batch,),
        in_specs=[pl.BlockSpec((1, CTX_LEN, D_MODEL), tok),
                  pl.BlockSpec((1, N_MOD, D_MODEL), lambda b: (0, 0, 0)),
                  pl.BlockSpec((1, D_MODEL), lambda b: (0, 0)),
                  _resident((D_MODEL, N_GROUPS * GROUP_W))],
        out_specs=out_specs,
        out_shape=out_shapes,
        compiler_params=_params(1),
        name="in_ctx",
    )(ctx, mods_ctx, g_pre, w_in)


def _ret_kernel(lg_ref, q_ref, k_ref, v_ref, g_ref, kc_ref, vc_ref, gn_ref, o_ref,
                dec_ref, wt_ref, upd_ref, st_ref):
    head = pl.program_id(0)
    lgf = lg_ref[0, head]
    lgb = lg_ref[1, head]
    c = RET_CHUNK
    n_chunks = SEQ // c

    @pl.when(pl.program_id(1) == 0)
    def _tables():
        ii = lax.broadcasted_iota(jnp.int32, (c, c), 0)
        jj = lax.broadcasted_iota(jnp.int32, (c, c), 1)
        diff = (ii - jj).astype(F32)
        dec_ref[...] = jnp.exp(jnp.where(diff >= 0, lgf * diff, -lgb * diff))
        ri = lax.broadcasted_iota(jnp.int32, (c, LANES), 0).astype(F32)
        wt_ref[0] = jnp.exp(lgf * (ri + 1.0))
        wt_ref[1] = jnp.exp(lgb * (c - ri))
        wt_ref[2] = jnp.exp(lgf * (c - 1.0 - ri))
        wt_ref[3] = jnp.exp(lgb * ri)

    ones = jnp.ones((1, LANES), F32)
    chunk_f = jnp.exp(ones * (lgf * c))
    chunk_b = jnp.exp(ones * (lgb * c))

    def both(a, wf, wb):
        af = a.astype(F32)
        return jnp.concatenate([(af * wt_ref[wf]).astype(BF16), (af * wt_ref[wb]).astype(BF16)], axis=1)

    def update(k, v):
        return lax.dot_general(both(k, 2, 3), v, _TN, preferred_element_type=F32)

    def chunk_rows(n):
        return pl.ds(pl.multiple_of(n * c, c), c)

    def phase1(n, carry):
        rows = chunk_rows(n)
        upd_ref[n] = update(k_ref[0, rows, :], v_ref[0, rows, :])
        return carry

    lax.fori_loop(0, n_chunks, phase1, 0, unroll=2)

    ctx_upd = update(kc_ref[0], vc_ref[0])
    state = ctx_upd[:RET_DIM]
    for n in range(n_chunks):
        st_ref[n, 0:RET_DIM, :] = state.astype(BF16)
        state = chunk_f * state + upd_ref[n, 0:RET_DIM, :]
    state = ctx_upd[RET_DIM:]
    for n in reversed(range(n_chunks)):
        st_ref[n, RET_DIM:, :] = state.astype(BF16)
        state = chunk_b * state + upd_ref[n, RET_DIM:, :]

    gn = gn_ref[...]

    def phase3(n, carry):
        rows = chunk_rows(n)
        q, k, v = q_ref[0, rows, :], k_ref[0, rows, :], v_ref[0, rows, :]
        s = lax.dot_general(q, k, _NT, preferred_element_type=F32) * dec_ref[...]
        o = (jnp.dot(s.astype(BF16), v, preferred_element_type=F32)
             + jnp.dot(both(q, 0, 1), st_ref[n], preferred_element_type=F32))
        d = o - jnp.mean(o, axis=-1, keepdims=True)
        y = d * lax.rsqrt(jnp.mean(d * d, axis=-1, keepdims=True) + NORM_EPS) * gn
        o_ref[0, rows, :] = (y * _silu(g_ref[0, rows, :].astype(F32))).astype(BF16)
        return carry

    lax.fori_loop(0, n_chunks, phase3, 0, unroll=2)


def _retention(log_gammas, rq, rk, rv, rg, crk, crv, gn_w):
    assert CTX_LEN == RET_CHUNK
    batch = rq.shape[0]
    n_chunks = SEQ // RET_CHUNK
    lat = pl.BlockSpec((1, SEQ, LANES), lambda h, b: (b, 0, h))
    cx = pl.BlockSpec((1, CTX_LEN, LANES), lambda h, b: (b, 0, h))
    return pl.pallas_call(
        _ret_kernel,
        grid=(RET_HEADS, batch),
        in_specs=[pl.BlockSpec(memory_space=pltpu.SMEM), lat, lat, lat, lat, cx, cx,
                  pl.BlockSpec((1, LANES), lambda h, b: (0, h))],
        out_specs=lat,
        out_shape=jax.ShapeDtypeStruct((batch, SEQ, RET_WIDTH), BF16),
        scratch_shapes=[pltpu.VMEM((RET_CHUNK, RET_CHUNK), F32),
                        pltpu.VMEM((4, RET_CHUNK, LANES), F32),
                        pltpu.VMEM((n_chunks, 2 * RET_DIM, RET_DIM), F32),
                        pltpu.VMEM((n_chunks, 2 * RET_DIM, RET_DIM), BF16)],
        compiler_params=_params(2),
        name="ret",
    )(log_gammas, rq, rk, rv, rg, crk, crv, gn_w)


N_DR = 2 * NA_KH - 1
N_DC = 2 * NA_KW - 1
PAIR_ROWS = 2 * GRID_W
NA_GROUP = 4


def _na_build_bias(rpb_ref, bias_ref, pair):
    qi = lax.broadcasted_iota(jnp.int32, (GRID_W, LANES), 0)
    li = lax.broadcasted_iota(jnp.int32, (GRID_W, LANES), 1)
    key_col = li & (GRID_W - 1)
    col_start = jnp.clip(qi - NA_KW // 2, 0, GRID_W - NA_KW)
    valid = (key_col >= col_start) & (key_col < col_start + NA_KW)
    code = jnp.where(valid, key_col - qi + (NA_KW - 1), -1)
    first_row = lax.broadcasted_iota(jnp.int32, (1, LANES), 1) < GRID_W

    def pattern(t, carry):
        for hd in range(2):
            head = 2 * pair + hd
            for blk in range(NA_KH // 2):
                base = (head * N_DR + (2 * blk + NA_KH - 1 - t)) * N_DC
                acc = jnp.full((GRID_W, LANES), NEG_INF, F32)
                for b in range(N_DC):
                    val = jnp.where(first_row, rpb_ref[base + b], rpb_ref[base + N_DC + b])
                    acc = jnp.where(code == b, val, acc)
                bias_ref[t, hd * GRID_W:(hd + 1) * GRID_W, blk * LANES:(blk + 1) * LANES] = acc
        return carry

    lax.fori_loop(0, N_PATTERNS, pattern, 0)


def _na_kernel(rpb_ref, q_ref, kt_ref, v_ref, kct_ref, vc_ref, o_ref,
               bias_ref, kt2_ref, q2_ref, s_ref):
    pair = pl.program_id(0)
    low = lax.broadcasted_iota(jnp.int32, (1, LANES), 1) < NA_DIM
    half = NA_KH // 2

    @pl.when(pl.program_id(1) == 0)
    def _bias():
        _na_build_bias(rpb_ref, bias_ref, pair)

    kt2_ref[0] = kt_ref[0]
    kt2_ref[1, :, 0:SEQ - LANES] = kt_ref[0, :, GRID_W:SEQ - GRID_W]
    kt2_ref[1, :, SEQ - LANES:SEQ] = jnp.zeros((LANES, LANES), BF16)

    def fill_q2(r, carry):
        q = q_ref[0, pl.ds(pl.multiple_of(r * GRID_W, GRID_W), GRID_W), :]
        zero = jnp.zeros_like(q)
        base = pl.multiple_of(r * PAIR_ROWS, PAIR_ROWS)
        q2_ref[pl.ds(base, GRID_W), :] = jnp.where(low, q, zero)
        q2_ref[pl.ds(base + GRID_W, GRID_W), :] = jnp.where(low, zero, q)
        return carry

    lax.fori_loop(0, GRID_ROWS, fill_q2, 0, unroll=4)

    def band_start(r):
        return jnp.clip(r - half, 0, GRID_ROWS - NA_KH)

    def scores(group, slot):
        rows = group * (NA_GROUP * PAIR_ROWS)
        group_q = pl.ds(pl.multiple_of(rows, NA_GROUP * PAIR_ROWS), NA_GROUP * PAIR_ROWS)
        s_ref[slot, :, NA_BAND:] = jnp.dot(q2_ref[group_q, :], kct_ref[0], preferred_element_type=F32)
        for j in range(NA_GROUP):
            r = group * NA_GROUP + j
            start = band_start(r)
            pattern = jnp.where(r < half, r, jnp.where(r > GRID_ROWS - half, r - (GRID_ROWS - NA_KH), half))
            lanes = pl.ds(pl.multiple_of((start >> 1) * LANES, LANES), NA_BAND)
            kb = kt2_ref[start & 1, :, lanes]
            q2 = q2_ref[pl.ds(pl.multiple_of(r * PAIR_ROWS, PAIR_ROWS), PAIR_ROWS), :]
            s_ref[slot, j * PAIR_ROWS:(j + 1) * PAIR_ROWS, 0:NA_BAND] = (
                jnp.dot(q2, kb, preferred_element_type=F32) + bias_ref[pattern])

    n_blocks = (NA_BAND + CTX_LEN) // LANES

    def outputs(group, slot):
        for j in range(NA_GROUP):
            r = group * NA_GROUP + j
            rows = slice(j * PAIR_ROWS, (j + 1) * PAIR_ROWS)

            def block(b):
                return s_ref[slot, rows, b * LANES:(b + 1) * LANES]

            m = block(0)
            for b in range(1, n_blocks):
                m = jnp.maximum(m, block(b))
            m = jnp.max(m, axis=-1, keepdims=True)
            p = [jnp.exp(block(b) - m) for b in range(n_blocks)]
            den = p[0]
            for pb in p[1:]:
                den = den + pb
            den = jnp.sum(den, axis=-1, keepdims=True)
            p = jnp.concatenate([pb.astype(BF16) for pb in p], axis=1)
            vb = v_ref[0, pl.ds(pl.multiple_of(band_start(r) * GRID_W, GRID_W), NA_BAND), :]
            o = (jnp.dot(p[:, 0:NA_BAND], vb, preferred_element_type=F32)
                 + jnp.dot(p[:, NA_BAND:], vc_ref[0], preferred_element_type=F32)) * (1.0 / den)
            o_ref[0, pl.ds(pl.multiple_of(r * GRID_W, GRID_W), GRID_W), :] = (
                jnp.where(low, o[:GRID_W], o[GRID_W:]).astype(BF16))

    n_groups = GRID_ROWS // NA_GROUP
    scores(0, 0)

    def two_groups(h, carry):
        scores(2 * h + 1, 1)
        outputs(2 * h, 0)
        scores(jnp.minimum(2 * h + 2, n_groups - 1), 0)
        outputs(2 * h + 1, 1)
        return carry

    lax.fori_loop(0, n_groups // 2, two_groups, 0)


def _neighbourhood_attention(rpb, nq, nkt, nv, cnkt, cnv):
    batch = nq.shape[0]
    n_q2 = GRID_ROWS * PAIR_ROWS
    lat = pl.BlockSpec((1, SEQ, LANES), lambda p, b: (b, 0, p))
    cx = pl.BlockSpec((1, CTX_LEN, LANES), lambda p, b: (b, 0, p))
    return pl.pallas_call(
        _na_kernel,
        grid=(NA_HEADS // 2, batch),
        in_specs=[pl.BlockSpec(memory_space=pltpu.SMEM), lat,
                  pl.BlockSpec((1, LANES, SEQ), lambda p, b: (b, p, 0)), lat,
                  pl.BlockSpec((1, LANES, CTX_LEN), lambda p, b: (b, p, 0)), cx],
        out_specs=lat,
        out_shape=jax.ShapeDtypeStruct((batch, SEQ, NA_WIDTH), BF16),
        scratch_shapes=[pltpu.VMEM((N_PATTERNS, PAIR_ROWS, NA_BAND), F32),
                        pltpu.VMEM((2, LANES, SEQ), BF16),
                        pltpu.VMEM((n_q2, LANES), BF16),
                        pltpu.VMEM((2, NA_GROUP * PAIR_ROWS, NA_BAND + CTX_LEN), F32)],
        compiler_params=_params(2),
        name="na",
    )(rpb, nq, nkt, nv, cnkt, cnv)


def _out_kernel(x_ref, ret_ref, na_ref, mod_ref, gpm_ref, gpre_ref, gpost_ref, wo_ref, w1_ref, w2_ref, o_ref):
    gt1, sh2, sc2, gt2 = (mod_ref[0, i:i + 1, :] for i in (2, 3, 4, 5))
    mix = (jnp.dot(ret_ref[0], wo_ref[0:RET_WIDTH, :], preferred_element_type=F32)
           + jnp.dot(na_ref[0], wo_ref[RET_WIDTH:, :], preferred_element_type=F32))
    x1 = x_ref[0] + gt1 * _rms(mix, gpm_ref[...])
    h2 = _prenorm(x1, gpre_ref[...], sh2, sc2)
    acc = jnp.zeros((ROW_TILE, D_MODEL), F32)
    for ci in range(D_FF // FF_CHUNK):
        cols = slice(ci * FF_CHUNK, (ci + 1) * FF_CHUNK)
        a = jnp.maximum(jnp.dot(h2, w1_ref[:, cols], preferred_element_type=F32), 0.0)
        acc = acc + jnp.dot((a * a).astype(BF16), w2_ref[cols, :], preferred_element_type=F32)
    o_ref[0] = x1 + gt2 * _rms(acc, gpost_ref[...])


def _out_proj_mlp(x, ret_lat, na_lat, mods, g_post_mix, g_pre_mlp, g_post_mlp, w_out, w_mlp1, w_mlp2):
    batch = x.shape[0]
    tok = lambda b, i: (b, i, 0)
    gain = pl.BlockSpec((1, D_MODEL), lambda b, i: (0, 0))
    return pl.pallas_call(
        _out_kernel,
        grid=(batch, SEQ // ROW_TILE),
        in_specs=[pl.BlockSpec((1, ROW_TILE, D_MODEL), tok),
                  pl.BlockSpec((1, ROW_TILE, RET_WIDTH), tok),
                  pl.BlockSpec((1, ROW_TILE, NA_WIDTH), tok),
                  pl.BlockSpec((1, N_MOD, D_MODEL), lambda b, i: (b, 0, 0)),
                  gain, gain, gain,
                  _resident((RET_WIDTH + NA_WIDTH, D_MODEL)),
                  _resident((D_MODEL, D_FF)),
                  _resident((D_FF, D_MODEL))],
        out_specs=pl.BlockSpec((1, ROW_TILE, D_MODEL), tok),
        out_shape=jax.ShapeDtypeStruct(x.shape, F32),
        compiler_params=_params(2),
        name="out_mlp",
    )(x, ret_lat, na_lat, mods, g_post_mix, g_pre_mlp, g_post_mlp, w_out, w_mlp1, w_mlp2)


def kernel(x, c, ctx, c_ctx, w_ada, b_ada, g_pre_mix, g_post_mix, g_pre_mlp, g_post_mlp,
           w_in, ret_decay, ret_gn, na_rpb, w_out, w_mlp1, w_mlp2):
    assert w_in.shape[0] == 1, "single-layer block: the context stream is never updated"
    batch = x.shape[0]
    pad = (-(batch + 1)) % 8
    cc = jnp.concatenate([c, c_ctx[None, :], jnp.zeros((pad, D_MODEL), F32)], axis=0)
    mods = _modulations(cc, w_ada[0], b_ada)
    mods_lat = mods[:batch].reshape(batch, N_MOD, D_MODEL)
    mods_ctx = mods[batch:batch + 1].reshape(1, N_MOD, D_MODEL)

    w_in_b = w_in[0].astype(BF16)
    rq, rk, rv, rg, nq, nkt, nv = _in_proj_lat(x, mods_lat, g_pre_mix, w_in_b)
    crk, crv, cnkt, cnv = _in_proj_ctx(ctx, mods_ctx, g_pre_mix, w_in_b)

    log_gammas = jax.nn.log_sigmoid(ret_decay[0].astype(F32))
    ret_lat = _retention(log_gammas, rq, rk, rv, rg, crk, crv, ret_gn)
    na_lat = _neighbourhood_attention(na_rpb[0].astype(F32).reshape(-1), nq, nkt, nv, cnkt, cnv)

    return _out_proj_mlp(x, ret_lat, na_lat, mods_lat, g_post_mix, g_pre_mlp, g_post_mlp,
                         w_out[0].astype(BF16), w_mlp1[0].astype(BF16), w_mlp2[0].astype(BF16))
```

```python
import numpy as np
import jax
import jax.numpy as jnp
from jax import lax
from jax.experimental import pallas as pl
from jax.experimental.pallas import tpu as pltpu

D_MODEL = 1024
SEQ = 2048
CTX_LEN = 256
GRID_W = 64
GRID_ROWS = SEQ // GRID_W
RET_HEADS = 4
RET_DIM = 128
RET_WIDTH = RET_HEADS * RET_DIM
NA_HEADS = 8
NA_DIM = 64
NA_WIDTH = NA_HEADS * NA_DIM
NA_KH = 8
NA_KW = 16
N_GROUPS = 7
GROUP_W = 512
_K_GROUPS = (1, 5)
D_FF = 4 * D_MODEL
ROPE_BASE = 10000.0
NORM_EPS = 1e-6
N_MOD = 6
NEG_INF = -1e30

LANES = 128
ROW_TILE = 512
FF_CHUNK = 1024
RET_CHUNK = 256
NA_BAND = NA_KH * GRID_W
N_PATTERNS = 8
VMEM_LIMIT = 56 * 1024 * 1024

F32 = jnp.float32
BF16 = jnp.bfloat16
_NT = (((1,), (1,)), ((), ()))
_TN = (((0,), (0,)), ((), ()))


def _silu(x):
    return x * (1.0 / (1.0 + jnp.exp(-x)))


def _rms(x, g):
    return x * lax.rsqrt(jnp.mean(x * x, axis=-1, keepdims=True) + NORM_EPS) * g


def _params(n_axes):
    return pltpu.CompilerParams(dimension_semantics=("arbitrary",) * n_axes,
                                vmem_limit_bytes=VMEM_LIMIT)


def _resident(shape):
    nd = len(shape)
    return pl.BlockSpec(shape, lambda *_: (0,) * nd, pipeline_mode=pl.Buffered(1))


def _mod_kernel(c_ref, w_ref, b_ref, o_ref):
    a = _silu(c_ref[...]).astype(BF16)
    o_ref[...] = jnp.dot(a, w_ref[...].astype(BF16), preferred_element_type=F32) + b_ref[...]


def _modulations(cc, w_ada, b_ada):
    rows = cc.shape[0]
    return pl.pallas_call(
        _mod_kernel,
        grid=(N_MOD,),
        in_specs=[pl.BlockSpec((rows, D_MODEL), lambda j: (0, 0)),
                  pl.BlockSpec((D_MODEL, D_MODEL), lambda j: (0, j)),
                  pl.BlockSpec((1, D_MODEL), lambda j: (0, j))],
        out_specs=pl.BlockSpec((rows, D_MODEL), lambda j: (0, j)),
        out_shape=jax.ShapeDtypeStruct((rows, N_MOD * D_MODEL), F32),
        compiler_params=_params(1),
        name="mod",
    )(cc, w_ada, b_ada)


def _prenorm(x, g, shift, scale):
    return (_rms(x, g) * (1.0 + scale) + shift).astype(BF16)


def _rope(blk, cos, sin_signed, first_half):
    partner = jnp.where(first_half, pltpu.roll(blk, LANES - 32, 1), pltpu.roll(blk, 32, 1))
    return blk * cos + partner * sin_signed


def _in_lat_kernel(x_ref, mod_ref, g_ref, w_ref, cq_ref, sq_ref, ck_ref, sk_ref, *out_refs):
    h = _prenorm(x_ref[0], g_ref[...], mod_ref[0, 0:1, :], mod_ref[0, 1:2, :])
    lane = lax.broadcasted_iota(jnp.int32, (1, LANES), 1)
    first_half = (lane & 32) == 0
    for gi, o_ref in enumerate(out_refs):
        acc = jnp.dot(h, w_ref[:, gi * GROUP_W:(gi + 1) * GROUP_W], preferred_element_type=F32)
        if gi == 0:
            for hh in range(RET_HEADS):
                blk = _rope(acc[:, hh * LANES:(hh + 1) * LANES], cq_ref[...], sq_ref[...], first_half)
                o_ref[0, :, hh * LANES:(hh + 1) * LANES] = blk.astype(BF16)
        elif gi == 1:
            for hh in range(RET_HEADS):
                blk = _rope(acc[:, hh * LANES:(hh + 1) * LANES], ck_ref[...], sk_ref[...], first_half)
                o_ref[0, hh * LANES:(hh + 1) * LANES, :] = blk.T.astype(BF16)
        elif gi == 4:
            o_ref[0] = (acc * NA_DIM ** -0.5).astype(BF16)
        elif gi in _K_GROUPS:
            o_ref[0] = acc.T.astype(BF16)
        else:
            o_ref[0] = acc.astype(BF16)


_CTX_GROUPS = (1, 2, 5, 6)


def _in_ctx_kernel(x_ref, mod_ref, g_ref, w_ref, *out_refs):
    h = _prenorm(x_ref[0], g_ref[...], mod_ref[0, 0:1, :], mod_ref[0, 1:2, :])
    for gi, o_ref in zip(_CTX_GROUPS, out_refs):
        acc = jnp.dot(h, w_ref[:, gi * GROUP_W:(gi + 1) * GROUP_W], preferred_element_type=F32)
        o_ref[0] = (acc.T if gi in _K_GROUPS else acc).astype(BF16)


def _rope_tables():
    tok = np.arange(SEQ)
    n_freq = RET_DIM // 4
    inv = ROPE_BASE ** (-np.arange(n_freq, dtype=np.float64) / n_freq)
    ang_r = (tok // GRID_W)[:, None] * inv[None, :]
    ang_c = (tok % GRID_W)[:, None] * inv[None, :]
    cos = np.concatenate([np.cos(ang_r), np.cos(ang_r), np.cos(ang_c), np.cos(ang_c)], axis=-1)
    sin = np.concatenate([-np.sin(ang_r), np.sin(ang_r), -np.sin(ang_c), np.sin(ang_c)], axis=-1)
    qs = RET_DIM ** -0.5
    return [jnp.asarray(t, F32) for t in (cos * qs, sin * qs, cos, sin)]


def _in_proj_lat(x, mods, g_pre, w_in):
    batch = x.shape[0]
    tok = lambda b, i: (b, i, 0)
    tab = pl.BlockSpec((ROW_TILE, LANES), lambda b, i: (i, 0))
    out_shapes = [jax.ShapeDtypeStruct((batch, SEQ, GROUP_W), BF16)] * N_GROUPS
    out_specs = [pl.BlockSpec((1, ROW_TILE, GROUP_W), tok)] * N_GROUPS
    for gi in _K_GROUPS:
        out_shapes[gi] = jax.ShapeDtypeStruct((batch, GROUP_W, SEQ), BF16)
        out_specs[gi] = pl.BlockSpec((1, GROUP_W, ROW_TILE), lambda b, i: (b, 0, i))
    return pl.pallas_call(
        _in_lat_kernel,
        grid=(batch, SEQ // ROW_TILE),
        in_specs=[pl.BlockSpec((1, ROW_TILE, D_MODEL), tok),
                  pl.BlockSpec((1, N_MOD, D_MODEL), lambda b, i: (b, 0, 0)),
                  pl.BlockSpec((1, D_MODEL), lambda b, i: (0, 0)),
                  _resident((D_MODEL, N_GROUPS * GROUP_W)),
                  tab, tab, tab, tab],
        out_specs=out_specs,
        out_shape=out_shapes,
        compiler_params=_params(2),
        name="in_lat",
    )(x, mods, g_pre, w_in, *_rope_tables())


def _in_proj_ctx(ctx, mods_ctx, g_pre, w_in):
    batch = ctx.shape[0]
    tok = lambda b: (b, 0, 0)
    out_shapes = [jax.ShapeDtypeStruct((batch, GROUP_W, CTX_LEN) if gi in _K_GROUPS else (batch, CTX_LEN, GROUP_W),
                                       BF16) for gi in _CTX_GROUPS]
    out_specs = [pl.BlockSpec((1, GROUP_W, CTX_LEN) if gi in _K_GROUPS else (1, CTX_LEN, GROUP_W), tok)
                 for gi in _CTX_GROUPS]
    return pl.pallas_call(
        _in_ctx_kernel,
        grid=(batch,),
        in_specs=[pl.BlockSpec((1, CTX_LEN, D_MODEL), tok),
                  pl.BlockSpec((1, N_MOD, D_MODEL), lambda b: (0, 0, 0)),
                  pl.BlockSpec((1, D_MODEL), lambda b: (0, 0)),
                  _resident((D_MODEL, N_GROUPS * GROUP_W))],
        out_specs=out_specs,
        out_shape=out_shapes,
        compiler_params=_params(1),
        name="in_ctx",
    )(ctx, mods_ctx, g_pre, w_in)


def _ret_kernel(lg_ref, q_ref, kt_ref, v_ref, g_ref, kct_ref, vc_ref, gn_ref, o_ref,
                dec_ref, wq_ref, wk_ref, upd_ref, st_ref, sp_ref):
    head = pl.program_id(0)
    lgf = lg_ref[0, head]
    lgb = lg_ref[1, head]
    c = RET_CHUNK
    n_chunks = SEQ // c

    @pl.when(pl.program_id(1) == 0)
    def _tables():
        ii = lax.broadcasted_iota(jnp.int32, (c, c), 0)
        jj = lax.broadcasted_iota(jnp.int32, (c, c), 1)
        diff = (ii - jj).astype(F32)
        dec_ref[...] = jnp.exp(jnp.where(diff >= 0, lgf * diff, -lgb * diff))
        ri = lax.broadcasted_iota(jnp.int32, (c, LANES), 0).astype(F32)
        wq_ref[0] = jnp.exp(lgf * (ri + 1.0))
        wq_ref[1] = jnp.exp(lgb * (c - ri))
        ti = lax.broadcasted_iota(jnp.int32, (1, c), 1).astype(F32)
        wk_ref[0] = jnp.exp(lgf * (c - 1.0 - ti))
        wk_ref[1] = jnp.exp(lgb * ti)

    ones = jnp.ones((1, LANES), F32)
    chunk_f = jnp.exp(ones * (lgf * c))
    chunk_b = jnp.exp(ones * (lgb * c))

    def update(kt, v):
        ktf = kt.astype(F32)
        kw = jnp.concatenate([(ktf * wk_ref[0]).astype(BF16), (ktf * wk_ref[1]).astype(BF16)], axis=0)
        return jnp.dot(kw, v, preferred_element_type=F32)

    def chunk(n):
        return slice(n * c, (n + 1) * c)

    for n in range(n_chunks):
        upd_ref[n] = update(kt_ref[0, :, chunk(n)], v_ref[0, chunk(n), :])

    ctx_upd = update(kct_ref[0], vc_ref[0])
    state = ctx_upd[:RET_DIM]
    for n in range(n_chunks):
        st_ref[n, 0:RET_DIM, :] = state.astype(BF16)
        state = chunk_f * state + upd_ref[n, 0:RET_DIM, :]
    state = ctx_upd[RET_DIM:]
    for n in reversed(range(n_chunks)):
        st_ref[n, RET_DIM:, :] = state.astype(BF16)
        state = chunk_b * state + upd_ref[n, RET_DIM:, :]

    def scores(n, slot):
        s = jnp.dot(q_ref[0, chunk(n), :], kt_ref[0, :, chunk(n)], preferred_element_type=F32)
        sp_ref[slot] = (s * dec_ref[...]).astype(BF16)

    def outputs(n, slot):
        rows = chunk(n)
        qf = q_ref[0, rows, :].astype(F32)
        qw = jnp.concatenate([(qf * wq_ref[0]).astype(BF16), (qf * wq_ref[1]).astype(BF16)], axis=1)
        o = (jnp.dot(sp_ref[slot], v_ref[0, rows, :], preferred_element_type=F32)
             + jnp.dot(qw, st_ref[n], preferred_element_type=F32))
        d = o - jnp.mean(o, axis=-1, keepdims=True)
        y = d * lax.rsqrt(jnp.mean(d * d, axis=-1, keepdims=True) + NORM_EPS) * gn_ref[...]
        o_ref[0, rows, :] = (y * _silu(g_ref[0, rows, :].astype(F32))).astype(BF16)

    scores(0, 0)
    for n in range(n_chunks):
        if n + 1 < n_chunks:
            scores(n + 1, (n + 1) % 2)
        outputs(n, n % 2)


def _retention(log_gammas, rq, rkt, rv, rg, crkt, crv, gn_w):
    assert CTX_LEN == RET_CHUNK
    batch = rq.shape[0]
    n_chunks = SEQ // RET_CHUNK
    lat = pl.BlockSpec((1, SEQ, LANES), lambda h, b: (b, 0, h))
    cx = pl.BlockSpec((1, CTX_LEN, LANES), lambda h, b: (b, 0, h))
    return pl.pallas_call(
        _ret_kernel,
        grid=(RET_HEADS, batch),
        in_specs=[pl.BlockSpec(memory_space=pltpu.SMEM), lat,
                  pl.BlockSpec((1, LANES, SEQ), lambda h, b: (b, h, 0)), lat, lat,
                  pl.BlockSpec((1, LANES, CTX_LEN), lambda h, b: (b, h, 0)), cx,
                  pl.BlockSpec((1, LANES), lambda h, b: (0, h))],
        out_specs=lat,
        out_shape=jax.ShapeDtypeStruct((batch, SEQ, RET_WIDTH), BF16),
        scratch_shapes=[pltpu.VMEM((RET_CHUNK, RET_CHUNK), F32),
                        pltpu.VMEM((2, RET_CHUNK, LANES), F32),
                        pltpu.VMEM((2, 1, RET_CHUNK), F32),
                        pltpu.VMEM((n_chunks, 2 * RET_DIM, RET_DIM), F32),
                        pltpu.VMEM((n_chunks, 2 * RET_DIM, RET_DIM), BF16),
                        pltpu.VMEM((2, RET_CHUNK, RET_CHUNK), BF16)],
        compiler_params=_params(2),
        name="ret",
    )(log_gammas, rq, rkt, rv, rg, crkt, crv, gn_w)


N_DR = 2 * NA_KH - 1
N_DC = 2 * NA_KW - 1
PAIR_ROWS = 2 * GRID_W
NA_GROUP = 4


def _na_build_bias(rpb_ref, bias_ref, pair):
    qi = lax.broadcasted_iota(jnp.int32, (GRID_W, LANES), 0)
    li = lax.broadcasted_iota(jnp.int32, (GRID_W, LANES), 1)
    key_col = li & (GRID_W - 1)
    col_start = jnp.clip(qi - NA_KW // 2, 0, GRID_W - NA_KW)
    valid = (key_col >= col_start) & (key_col < col_start + NA_KW)
    code = jnp.where(valid, key_col - qi + (NA_KW - 1), -1)
    first_row = lax.broadcasted_iota(jnp.int32, (1, LANES), 1) < GRID_W

    def pattern(t, carry):
        for hd in range(2):
            head = 2 * pair + hd
            for blk in range(NA_KH // 2):
                base = (head * N_DR + (2 * blk + NA_KH - 1 - t)) * N_DC
                acc = jnp.full((GRID_W, LANES), NEG_INF, F32)
                for b in range(N_DC):
                    val = jnp.where(first_row, rpb_ref[base + b], rpb_ref[base + N_DC + b])
                    acc = jnp.where(code == b, val, acc)
                bias_ref[t, hd * GRID_W:(hd + 1) * GRID_W, blk * LANES:(blk + 1) * LANES] = acc
        return carry

    lax.fori_loop(0, N_PATTERNS, pattern, 0)


def _na_kernel(rpb_ref, q_ref, kt_ref, v_ref, kct_ref, vc_ref, o_ref,
               bias_ref, kt2_ref, q2_ref, s_ref):
    pair = pl.program_id(0)
    low = lax.broadcasted_iota(jnp.int32, (1, LANES), 1) < NA_DIM
    half = NA_KH // 2

    @pl.when(pl.program_id(1) == 0)
    def _bias():
        _na_build_bias(rpb_ref, bias_ref, pair)

    kt2_ref[0] = kt_ref[0]
    kt2_ref[1, :, 0:SEQ - LANES] = kt_ref[0, :, GRID_W:SEQ - GRID_W]
    kt2_ref[1, :, SEQ - LANES:SEQ] = jnp.zeros((LANES, LANES), BF16)

    def fill_q2(r, carry):
        q = q_ref[0, pl.ds(pl.multiple_of(r * GRID_W, GRID_W), GRID_W), :]
        zero = jnp.zeros_like(q)
        base = pl.multiple_of(r * PAIR_ROWS, PAIR_ROWS)
        q2_ref[pl.ds(base, GRID_W), :] = jnp.where(low, q, zero)
        q2_ref[pl.ds(base + GRID_W, GRID_W), :] = jnp.where(low, zero, q)
        return carry

    lax.fori_loop(0, GRID_ROWS, fill_q2, 0, unroll=4)

    def band_start(r):
        return jnp.clip(r - half, 0, GRID_ROWS - NA_KH)

    def scores(group, slot):
        rows = group * (NA_GROUP * PAIR_ROWS)
        group_q = pl.ds(pl.multiple_of(rows, NA_GROUP * PAIR_ROWS), NA_GROUP * PAIR_ROWS)
        s_ref[slot, :, NA_BAND:] = jnp.dot(q2_ref[group_q, :], kct_ref[0], preferred_element_type=F32)
        for j in range(NA_GROUP):
            r = group * NA_GROUP + j
            start = band_start(r)
            pattern = jnp.where(r < half, r, jnp.where(r > GRID_ROWS - half, r - (GRID_ROWS - NA_KH), half))
            lanes = pl.ds(pl.multiple_of((start >> 1) * LANES, LANES), NA_BAND)
            kb = kt2_ref[start & 1, :, lanes]
            q2 = q2_ref[pl.ds(pl.multiple_of(r * PAIR_ROWS, PAIR_ROWS), PAIR_ROWS), :]
            s_ref[slot, j * PAIR_ROWS:(j + 1) * PAIR_ROWS, 0:NA_BAND] = (
                jnp.dot(q2, kb, preferred_element_type=F32) + bias_ref[pattern])

    n_blocks = (NA_BAND + CTX_LEN) // LANES

    def outputs(group, slot):
        for j in range(NA_GROUP):
            r = group * NA_GROUP + j
            rows = slice(j * PAIR_ROWS, (j + 1) * PAIR_ROWS)

            def block(b):
                return s_ref[slot, rows, b * LANES:(b + 1) * LANES]

            m = block(0)
            for b in range(1, n_blocks):
                m = jnp.maximum(m, block(b))
            m = jnp.max(m, axis=-1, keepdims=True)
            p = [jnp.exp(block(b) - m) for b in range(n_blocks)]
            den = p[0]
            for pb in p[1:]:
                den = den + pb
            den = jnp.sum(den, axis=-1, keepdims=True)
            p = jnp.concatenate([pb.astype(BF16) for pb in p], axis=1)
            vb = v_ref[0, pl.ds(pl.multiple_of(band_start(r) * GRID_W, GRID_W), NA_BAND), :]
            o = (jnp.dot(p[:, 0:NA_BAND], vb, preferred_element_type=F32)
                 + jnp.dot(p[:, NA_BAND:], vc_ref[0], preferred_element_type=F32)) * (1.0 / den)
            o_ref[0, pl.ds(pl.multiple_of(r * GRID_W, GRID_W), GRID_W), :] = (
                jnp.where(low, o[:GRID_W], o[GRID_W:]).astype(BF16))

    n_groups = GRID_ROWS // NA_GROUP
    scores(0, 0)
    for g in range(n_groups):
        if g + 1 < n_groups:
            scores(g + 1, (g + 1) % 2)
        outputs(g, g % 2)


def _neighbourhood_attention(rpb, nq, nkt, nv, cnkt, cnv):
    batch = nq.shape[0]
    n_q2 = GRID_ROWS * PAIR_ROWS
    lat = pl.BlockSpec((1, SEQ, LANES), lambda p, b: (b, 0, p))
    cx = pl.BlockSpec((1, CTX_LEN, LANES), lambda p, b: (b, 0, p))
    return pl.pallas_call(
        _na_kernel,
        grid=(NA_HEADS // 2, batch),
        in_specs=[pl.BlockSpec(memory_space=pltpu.SMEM), lat,
                  pl.BlockSpec((1, LANES, SEQ), lambda p, b: (b, p, 0)), lat,
                  pl.BlockSpec((1, LANES, CTX_LEN), lambda p, b: (b, p, 0)), cx],
        out_specs=lat,
        out_shape=jax.ShapeDtypeStruct((batch, SEQ, NA_WIDTH), BF16),
        scratch_shapes=[pltpu.VMEM((N_PATTERNS, PAIR_ROWS, NA_BAND), F32),
                        pltpu.VMEM((2, LANES, SEQ), BF16),
                        pltpu.VMEM((n_q2, LANES), BF16),
                        pltpu.VMEM((2, NA_GROUP * PAIR_ROWS, NA_BAND + CTX_LEN), F32)],
        compiler_params=_params(2),
        name="na",
    )(rpb, nq, nkt, nv, cnkt, cnv)


def _out_kernel(x_ref, ret_ref, na_ref, mod_ref, gpm_ref, gpre_ref, gpost_ref, wo_ref, w1_ref, w2_ref, o_ref):
    gt1, sh2, sc2, gt2 = (mod_ref[0, i:i + 1, :] for i in (2, 3, 4, 5))
    mix = (jnp.dot(ret_ref[0], wo_ref[0:RET_WIDTH, :], preferred_element_type=F32)
           + jnp.dot(na_ref[0], wo_ref[RET_WIDTH:, :], preferred_element_type=F32))
    x1 = x_ref[0] + gt1 * _rms(mix, gpm_ref[...])
    h2 = _prenorm(x1, gpre_ref[...], sh2, sc2)
    acc = jnp.zeros((ROW_TILE, D_MODEL), F32)
    for ci in range(D_FF // FF_CHUNK):
        cols = slice(ci * FF_CHUNK, (ci + 1) * FF_CHUNK)
        a = jnp.maximum(jnp.dot(h2, w1_ref[:, cols], preferred_element_type=F32), 0.0)
        acc = acc + jnp.dot((a * a).astype(BF16), w2_ref[cols, :], preferred_element_type=F32)
    o_ref[0] = x1 + gt2 * _rms(acc, gpost_ref[...])


def _out_proj_mlp(x, ret_lat, na_lat, mods, g_post_mix, g_pre_mlp, g_post_mlp, w_out, w_mlp1, w_mlp2):
    batch = x.shape[0]
    tok = lambda b, i: (b, i, 0)
    gain = pl.BlockSpec((1, D_MODEL), lambda b, i: (0, 0))
    return pl.pallas_call(
        _out_kernel,
        grid=(batch, SEQ // ROW_TILE),
        in_specs=[pl.BlockSpec((1, ROW_TILE, D_MODEL), tok),
                  pl.BlockSpec((1, ROW_TILE, RET_WIDTH), tok),
                  pl.BlockSpec((1, ROW_TILE, NA_WIDTH), tok),
                  pl.BlockSpec((1, N_MOD, D_MODEL), lambda b, i: (b, 0, 0)),
                  gain, gain, gain,
                  _resident((RET_WIDTH + NA_WIDTH, D_MODEL)),
                  _resident((D_MODEL, D_FF)),
                  _resident((D_FF, D_MODEL))],
        out_specs=pl.BlockSpec((1, ROW_TILE, D_MODEL), tok),
        out_shape=jax.ShapeDtypeStruct(x.shape, F32),
        compiler_params=_params(2),
        name="out_mlp",
    )(x, ret_lat, na_lat, mods, g_post_mix, g_pre_mlp, g_post_mlp, w_out, w_mlp1, w_mlp2)


def kernel(x, c, ctx, c_ctx, w_ada, b_ada, g_pre_mix, g_post_mix, g_pre_mlp, g_post_mlp,
           w_in, ret_decay, ret_gn, na_rpb, w_out, w_mlp1, w_mlp2):
    assert w_in.shape[0] == 1, "single-layer block: the context stream is never updated"
    batch = x.shape[0]
    pad = (-(batch + 1)) % 8
    cc = jnp.concatenate([c, c_ctx[None, :], jnp.zeros((pad, D_MODEL), F32)], axis=0)
    mods = _modulations(cc, w_ada[0], b_ada)
    mods_lat = mods[:batch].reshape(batch, N_MOD, D_MODEL)
    mods_ctx = mods[batch:batch + 1].reshape(1, N_MOD, D_MODEL)

    w_in_b = w_in[0].astype(BF16)
    rq, rk, rv, rg, nq, nkt, nv = _in_proj_lat(x, mods_lat, g_pre_mix, w_in_b)
    crk, crv, cnkt, cnv = _in_proj_ctx(ctx, mods_ctx, g_pre_mix, w_in_b)

    log_gammas = jax.nn.log_sigmoid(ret_decay[0].astype(F32))
    ret_lat = _retention(log_gammas, rq, rk, rv, rg, crk, crv, ret_gn)
    na_lat = _neighbourhood_attention(na_rpb[0].astype(F32).reshape(-1), nq, nkt, nv, cnkt, cnv)

    return _out_proj_mlp(x, ret_lat, na_lat, mods_lat, g_post_mix, g_pre_mlp, g_post_mlp,
                         w_out[0].astype(BF16), w_mlp1[0].astype(BF16), w_mlp2[0].astype(BF16))
```

```python
import numpy as np
import jax
import jax.numpy as jnp
from jax import lax
from jax.experimental import pallas as pl
from jax.experimental.pallas import tpu as pltpu

D_MODEL = 1024
SEQ = 2048
CTX_LEN = 256
GRID_W = 64
GRID_ROWS = SEQ // GRID_W
RET_HEADS = 4
RET_DIM = 128
RET_WIDTH = RET_HEADS * RET_DIM
NA_HEADS = 8
NA_DIM = 64
NA_WIDTH = NA_HEADS * NA_DIM
NA_KH = 8
NA_KW = 16
N_GROUPS = 7
GROUP_W = 512
_K_GROUPS = (1, 5)
D_FF = 4 * D_MODEL
ROPE_BASE = 10000.0
NORM_EPS = 1e-6
N_MOD = 6
NEG_INF = -1e30
LOG2_E = 1.4426950408889634

LANES = 128
ROW_TILE = 512
SUB_TILE = 256
FF_CHUNK = 1024
RET_CHUNK = 256
NA_BAND = NA_KH * GRID_W
N_PATTERNS = 8
VMEM_LIMIT = 56 * 1024 * 1024

F32 = jnp.float32
BF16 = jnp.bfloat16
_NT = (((1,), (1,)), ((), ()))
_TN = (((0,), (0,)), ((), ()))


def _silu(x):
    return x * (1.0 / (1.0 + jnp.exp(-x)))


def _rms(x, g):
    return x * lax.rsqrt(jnp.mean(x * x, axis=-1, keepdims=True) + NORM_EPS) * g


def _params(n_axes):
    return pltpu.CompilerParams(dimension_semantics=("arbitrary",) * n_axes,
                                vmem_limit_bytes=VMEM_LIMIT)


def _resident(shape):
    nd = len(shape)
    return pl.BlockSpec(shape, lambda *_: (0,) * nd, pipeline_mode=pl.Buffered(1))


def _mod_kernel(c_ref, w_ref, b_ref, o_ref):
    a = _silu(c_ref[...]).astype(BF16)
    o_ref[...] = jnp.dot(a, w_ref[...].astype(BF16), preferred_element_type=F32) + b_ref[...]


def _modulations(cc, w_ada, b_ada):
    rows = cc.shape[0]
    return pl.pallas_call(
        _mod_kernel,
        grid=(N_MOD,),
        in_specs=[pl.BlockSpec((rows, D_MODEL), lambda j: (0, 0)),
                  pl.BlockSpec((D_MODEL, D_MODEL), lambda j: (0, j)),
                  pl.BlockSpec((1, D_MODEL), lambda j: (0, j))],
        out_specs=pl.BlockSpec((rows, D_MODEL), lambda j: (0, j)),
        out_shape=jax.ShapeDtypeStruct((rows, N_MOD * D_MODEL), F32),
        compiler_params=_params(1),
        name="mod",
    )(cc, w_ada, b_ada)


def _prenorm(x, g, shift, scale):
    return (_rms(x, g) * (1.0 + scale) + shift).astype(BF16)


def _rope(blk, cos, sin_signed, first_half):
    partner = jnp.where(first_half, pltpu.roll(blk, LANES - 32, 1), pltpu.roll(blk, 32, 1))
    return blk * cos + partner * sin_signed


def _in_lat_kernel(x_ref, mod_ref, g_ref, w_ref, cq_ref, sq_ref, ck_ref, sk_ref, *out_refs):
    h = _prenorm(x_ref[0], g_ref[...], mod_ref[0, 0:1, :], mod_ref[0, 1:2, :])
    lane = lax.broadcasted_iota(jnp.int32, (1, LANES), 1)
    first_half = (lane & 32) == 0
    for gi, o_ref in enumerate(out_refs):
        acc = jnp.dot(h, w_ref[:, gi * GROUP_W:(gi + 1) * GROUP_W], preferred_element_type=F32)
        if gi == 0:
            for hh in range(RET_HEADS):
                blk = _rope(acc[:, hh * LANES:(hh + 1) * LANES], cq_ref[...], sq_ref[...], first_half)
                o_ref[0, :, hh * LANES:(hh + 1) * LANES] = blk.astype(BF16)
        elif gi == 1:
            for hh in range(RET_HEADS):
                blk = _rope(acc[:, hh * LANES:(hh + 1) * LANES], ck_ref[...], sk_ref[...], first_half)
                o_ref[0, hh * LANES:(hh + 1) * LANES, :] = blk.T.astype(BF16)
        elif gi == 4:
            o_ref[0] = (acc * (NA_DIM ** -0.5 * LOG2_E)).astype(BF16)
        elif gi in _K_GROUPS:
            o_ref[0] = acc.T.astype(BF16)
        else:
            o_ref[0] = acc.astype(BF16)


_CTX_GROUPS = (1, 2, 5, 6)


def _in_ctx_kernel(x_ref, mod_ref, g_ref, w_ref, *out_refs):
    h = _prenorm(x_ref[0], g_ref[...], mod_ref[0, 0:1, :], mod_ref[0, 1:2, :])
    for gi, o_ref in zip(_CTX_GROUPS, out_refs):
        acc = jnp.dot(h, w_ref[:, gi * GROUP_W:(gi + 1) * GROUP_W], preferred_element_type=F32)
        o_ref[0] = (acc.T if gi in _K_GROUPS else acc).astype(BF16)


def _rope_tables():
    tok = np.arange(SEQ)
    n_freq = RET_DIM // 4
    inv = ROPE_BASE ** (-np.arange(n_freq, dtype=np.float64) / n_freq)
    ang_r = (tok // GRID_W)[:, None] * inv[None, :]
    ang_c = (tok % GRID_W)[:, None] * inv[None, :]
    cos = np.concatenate([np.cos(ang_r), np.cos(ang_r), np.cos(ang_c), np.cos(ang_c)], axis=-1)
    sin = np.concatenate([-np.sin(ang_r), np.sin(ang_r), -np.sin(ang_c), np.sin(ang_c)], axis=-1)
    qs = RET_DIM ** -0.5
    return [jnp.asarray(t, F32) for t in (cos * qs, sin * qs, cos, sin)]


def _in_proj_lat(x, mods, g_pre, w_in):
    batch = x.shape[0]
    tok = lambda b, i: (b, i, 0)
    tab = pl.BlockSpec((ROW_TILE, LANES), lambda b, i: (i, 0))
    out_shapes = [jax.ShapeDtypeStruct((batch, SEQ, GROUP_W), BF16)] * N_GROUPS
    out_specs = [pl.BlockSpec((1, ROW_TILE, GROUP_W), tok)] * N_GROUPS
    for gi in _K_GROUPS:
        out_shapes[gi] = jax.ShapeDtypeStruct((batch, GROUP_W, SEQ), BF16)
        out_specs[gi] = pl.BlockSpec((1, GROUP_W, ROW_TILE), lambda b, i: (b, 0, i))
    return pl.pallas_call(
        _in_lat_kernel,
        grid=(batch, SEQ // ROW_TILE),
        in_specs=[pl.BlockSpec((1, ROW_TILE, D_MODEL), tok),
                  pl.BlockSpec((1, N_MOD, D_MODEL), lambda b, i: (b, 0, 0)),
                  pl.BlockSpec((1, D_MODEL), lambda b, i: (0, 0)),
                  _resident((D_MODEL, N_GROUPS * GROUP_W)),
                  tab, tab, tab, tab],
        out_specs=out_specs,
        out_shape=out_shapes,
        compiler_params=_params(2),
        name="in_lat",
    )(x, mods, g_pre, w_in, *_rope_tables())


def _in_proj_ctx(ctx, mods_ctx, g_pre, w_in):
    batch = ctx.shape[0]
    tok = lambda b: (b, 0, 0)
    out_shapes = [jax.ShapeDtypeStruct((batch, GROUP_W, CTX_LEN) if gi in _K_GROUPS else (batch, CTX_LEN, GROUP_W),
                                       BF16) for gi in _CTX_GROUPS]
    out_specs = [pl.BlockSpec((1, GROUP_W, CTX_LEN) if gi in _K_GROUPS else (1, CTX_LEN, GROUP_W), tok)
                 for gi in _CTX_GROUPS]
    return pl.pallas_call(
        _in_ctx_kernel,
        grid=(batch,),
        in_specs=[pl.BlockSpec((1, CTX_LEN, D_MODEL), tok),
                  pl.BlockSpec((1, N_MOD, D_MODEL), lambda b: (0, 0, 0)),
                  pl.BlockSpec((1, D_MODEL), lambda b: (0, 0)),
                  _resident((D_MODEL, N_GROUPS * GROUP_W))],
        out_specs=out_specs,
        out_shape=out_shapes,
        compiler_params=_params(1),
        name="in_ctx",
    )(ctx, mods_ctx, g_pre, w_in)


def _ret_kernel(lg_ref, q_ref, kt_ref, v_ref, g_ref, kct_ref, vc_ref, gn_ref, o_ref,
                dec_ref, wq_ref, wk_ref, upd_ref, st_ref, sp_ref):
    head = pl.program_id(0)
    lgf = lg_ref[0, head]
    lgb = lg_ref[1, head]
    c = RET_CHUNK
    n_chunks = SEQ // c

    @pl.when(pl.program_id(1) == 0)
    def _tables():
        ii = lax.broadcasted_iota(jnp.int32, (c, c), 0)
        jj = lax.broadcasted_iota(jnp.int32, (c, c), 1)
        diff = (ii - jj).astype(F32)
        dec_ref[...] = jnp.exp(jnp.where(diff >= 0, lgf * diff, -lgb * diff))
        ri = lax.broadcasted_iota(jnp.int32, (c, LANES), 0).astype(F32)
        wq_ref[0] = jnp.exp(lgf * (ri + 1.0))
        wq_ref[1] = jnp.exp(lgb * (c - ri))
        ti = lax.broadcasted_iota(jnp.int32, (1, c), 1).astype(F32)
        wk_ref[0] = jnp.exp(lgf * (c - 1.0 - ti))
        wk_ref[1] = jnp.exp(lgb * ti)

    ones = jnp.ones((1, LANES), F32)
    chunk_f = jnp.exp(ones * (lgf * c))
    chunk_b = jnp.exp(ones * (lgb * c))

    def update(kt, v):
        ktf = kt.astype(F32)
        kw = jnp.concatenate([(ktf * wk_ref[0]).astype(BF16), (ktf * wk_ref[1]).astype(BF16)], axis=0)
        return jnp.dot(kw, v, preferred_element_type=F32)

    def chunk(n):
        return slice(n * c, (n + 1) * c)

    for n in range(n_chunks):
        upd_ref[n] = update(kt_ref[0, :, chunk(n)], v_ref[0, chunk(n), :])

    ctx_upd = update(kct_ref[0], vc_ref[0])
    state = ctx_upd[:RET_DIM]
    for n in range(n_chunks):
        st_ref[n, 0:RET_DIM, :] = state.astype(BF16)
        state = chunk_f * state + upd_ref[n, 0:RET_DIM, :]
    state = ctx_upd[RET_DIM:]
    for n in reversed(range(n_chunks)):
        st_ref[n, RET_DIM:, :] = state.astype(BF16)
        state = chunk_b * state + upd_ref[n, RET_DIM:, :]

    def scores(n, slot):
        s = jnp.dot(q_ref[0, chunk(n), :], kt_ref[0, :, chunk(n)], preferred_element_type=F32)
        sp_ref[slot] = (s * dec_ref[...]).astype(BF16)

    def outputs(n, slot):
        rows = chunk(n)
        qf = q_ref[0, rows, :].astype(F32)
        qw = jnp.concatenate([(qf * wq_ref[0]).astype(BF16), (qf * wq_ref[1]).astype(BF16)], axis=1)
        o = (jnp.dot(sp_ref[slot], v_ref[0, rows, :], preferred_element_type=F32)
             + jnp.dot(qw, st_ref[n], preferred_element_type=F32))
        d = o - jnp.mean(o, axis=-1, keepdims=True)
        y = d * lax.rsqrt(jnp.mean(d * d, axis=-1, keepdims=True) + NORM_EPS) * gn_ref[...]
        o_ref[0, rows, :] = (y * _silu(g_ref[0, rows, :].astype(F32))).astype(BF16)

    scores(0, 0)
    for n in range(n_chunks):
        if n + 1 < n_chunks:
            scores(n + 1, (n + 1) % 2)
        outputs(n, n % 2)


def _retention(log_gammas, rq, rkt, rv, rg, crkt, crv, gn_w):
    assert CTX_LEN == RET_CHUNK
    batch = rq.shape[0]
    n_chunks = SEQ // RET_CHUNK
    lat = pl.BlockSpec((1, SEQ, LANES), lambda h, b: (b, 0, h))
    cx = pl.BlockSpec((1, CTX_LEN, LANES), lambda h, b: (b, 0, h))
    return pl.pallas_call(
        _ret_kernel,
        grid=(RET_HEADS, batch),
        in_specs=[pl.BlockSpec(memory_space=pltpu.SMEM), lat,
                  pl.BlockSpec((1, LANES, SEQ), lambda h, b: (b, h, 0)), lat, lat,
                  pl.BlockSpec((1, LANES, CTX_LEN), lambda h, b: (b, h, 0)), cx,
                  pl.BlockSpec((1, LANES), lambda h, b: (0, h))],
        out_specs=lat,
        out_shape=jax.ShapeDtypeStruct((batch, SEQ, RET_WIDTH), BF16),
        scratch_shapes=[pltpu.VMEM((RET_CHUNK, RET_CHUNK), F32),
                        pltpu.VMEM((2, RET_CHUNK, LANES), F32),
                        pltpu.VMEM((2, 1, RET_CHUNK), F32),
                        pltpu.VMEM((n_chunks, 2 * RET_DIM, RET_DIM), F32),
                        pltpu.VMEM((n_chunks, 2 * RET_DIM, RET_DIM), BF16),
                        pltpu.VMEM((2, RET_CHUNK, RET_CHUNK), BF16)],
        compiler_params=_params(2),
        name="ret",
    )(log_gammas, rq, rkt, rv, rg, crkt, crv, gn_w)


N_DR = 2 * NA_KH - 1
N_DC = 2 * NA_KW - 1
PAIR_ROWS = 2 * GRID_W
NA_GROUP = 1
S_SLOTS = 3


def _na_build_bias(rpb_ref, bias_ref, pair):
    qi = lax.broadcasted_iota(jnp.int32, (GRID_W, LANES), 0)
    li = lax.broadcasted_iota(jnp.int32, (GRID_W, LANES), 1)
    key_col = li & (GRID_W - 1)
    col_start = jnp.clip(qi - NA_KW // 2, 0, GRID_W - NA_KW)
    valid = (key_col >= col_start) & (key_col < col_start + NA_KW)
    code = jnp.where(valid, key_col - qi + (NA_KW - 1), -1)
    first_row = lax.broadcasted_iota(jnp.int32, (1, LANES), 1) < GRID_W

    def pattern(t, carry):
        for hd in range(2):
            head = 2 * pair + hd
            for blk in range(NA_KH // 2):
                base = (head * N_DR + (2 * blk + NA_KH - 1 - t)) * N_DC
                acc = jnp.full((GRID_W, LANES), NEG_INF, F32)
                for b in range(N_DC):
                    val = jnp.where(first_row, rpb_ref[base + b], rpb_ref[base + N_DC + b]) * LOG2_E
                    acc = jnp.where(code == b, val, acc)
                bias_ref[t, hd * GRID_W:(hd + 1) * GRID_W, blk * LANES:(blk + 1) * LANES] = acc
        return carry

    lax.fori_loop(0, N_PATTERNS, pattern, 0)


def _na_kernel(rpb_ref, q_ref, kt_ref, v_ref, kct_ref, vc_ref, o_ref,
               bias_ref, kt2_ref, q2_ref, s_ref, p_ref, rden_ref):
    pair = pl.program_id(0)
    low = lax.broadcasted_iota(jnp.int32, (1, LANES), 1) < NA_DIM
    half = NA_KH // 2

    @pl.when(pl.program_id(1) == 0)
    def _bias():
        _na_build_bias(rpb_ref, bias_ref, pair)

    kt2_ref[0] = kt_ref[0]
    kt2_ref[1, :, 0:SEQ - LANES] = kt_ref[0, :, GRID_W:SEQ - GRID_W]
    kt2_ref[1, :, SEQ - LANES:SEQ] = jnp.zeros((LANES, LANES), BF16)

    def fill_q2(r, carry):
        q = q_ref[0, pl.ds(pl.multiple_of(r * GRID_W, GRID_W), GRID_W), :]
        zero = jnp.zeros_like(q)
        base = pl.multiple_of(r * PAIR_ROWS, PAIR_ROWS)
        q2_ref[pl.ds(base, GRID_W), :] = jnp.where(low, q, zero)
        q2_ref[pl.ds(base + GRID_W, GRID_W), :] = jnp.where(low, zero, q)
        return carry

    lax.fori_loop(0, GRID_ROWS, fill_q2, 0, unroll=4)

    def band_start(r):
        return min(max(r - half, 0), GRID_ROWS - NA_KH)

    def bias_pattern(r):
        if r < half:
            return r
        return r - (GRID_ROWS - NA_KH) if r > GRID_ROWS - half else half

    group_rows = NA_GROUP * PAIR_ROWS
    n_blocks = (NA_BAND + CTX_LEN) // LANES

    def pair_rows(j):
        return slice(j * PAIR_ROWS, (j + 1) * PAIR_ROWS)

    def scores(g):
        slot = g % S_SLOTS
        s_ref[slot, :, NA_BAND:] = jnp.dot(q2_ref[g * group_rows:(g + 1) * group_rows, :], kct_ref[0],
                                           preferred_element_type=F32)
        for j in range(NA_GROUP):
            r = g * NA_GROUP + j
            start = band_start(r)
            first_lane = (start // 2) * LANES
            kb = kt2_ref[start % 2, :, first_lane:first_lane + NA_BAND]
            s_ref[slot, pair_rows(j), 0:NA_BAND] = (
                jnp.dot(q2_ref[r * PAIR_ROWS:(r + 1) * PAIR_ROWS, :], kb, preferred_element_type=F32)
                + bias_ref[bias_pattern(r)])

    def softmax(g):
        slot = g % S_SLOTS
        for j in range(NA_GROUP):
            rows = pair_rows(j)

            def block(b):
                return s_ref[slot, rows, b * LANES:(b + 1) * LANES]

            m = block(0)
            for b in range(1, n_blocks):
                m = jnp.maximum(m, block(b))
            m = jnp.max(m, axis=-1, keepdims=True)
            den = None
            for b in range(n_blocks):
                pb = jnp.exp2(block(b) - m)
                den = pb if den is None else den + pb
                p_ref[g % 2, rows, b * LANES:(b + 1) * LANES] = pb.astype(BF16)
            den = jnp.sum(den, axis=-1, keepdims=True)
            rden_ref[g % 2, rows, :] = jnp.broadcast_to(1.0 / den, (PAIR_ROWS, LANES))

    def values(g):
        slot = g % 2
        ctx = jnp.dot(p_ref[slot, :, NA_BAND:], vc_ref[0], preferred_element_type=F32)
        for j in range(NA_GROUP):
            r = g * NA_GROUP + j
            rows = pair_rows(j)
            start = band_start(r)
            vb = v_ref[0, start * GRID_W:start * GRID_W + NA_BAND, :]
            o = (jnp.dot(p_ref[slot, rows, 0:NA_BAND], vb, preferred_element_type=F32) + ctx[rows]) * rden_ref[slot, rows, :]
            o_ref[0, r * GRID_W:(r + 1) * GRID_W, :] = jnp.where(low, o[:GRID_W], o[GRID_W:]).astype(BF16)

    n_groups = GRID_ROWS // NA_GROUP
    for t in range(n_groups + 2):
        if t < n_groups:
            scores(t)
        if t >= 2:
            values(t - 2)
        if 1 <= t <= n_groups:
            softmax(t - 1)


def _neighbourhood_attention(rpb, nq, nkt, nv, cnkt, cnv):
    batch = nq.shape[0]
    n_q2 = GRID_ROWS * PAIR_ROWS
    lat = pl.BlockSpec((1, SEQ, LANES), lambda p, b: (b, 0, p))
    cx = pl.BlockSpec((1, CTX_LEN, LANES), lambda p, b: (b, 0, p))
    return pl.pallas_call(
        _na_kernel,
        grid=(NA_HEADS // 2, batch),
        in_specs=[pl.BlockSpec(memory_space=pltpu.SMEM), lat,
                  pl.BlockSpec((1, LANES, SEQ), lambda p, b: (b, p, 0)), lat,
                  pl.BlockSpec((1, LANES, CTX_LEN), lambda p, b: (b, p, 0)), cx],
        out_specs=lat,
        out_shape=jax.ShapeDtypeStruct((batch, SEQ, NA_WIDTH), BF16),
        scratch_shapes=[pltpu.VMEM((N_PATTERNS, PAIR_ROWS, NA_BAND), F32),
                        pltpu.VMEM((2, LANES, SEQ), BF16),
                        pltpu.VMEM((n_q2, LANES), BF16),
                        pltpu.VMEM((S_SLOTS, NA_GROUP * PAIR_ROWS, NA_BAND + CTX_LEN), F32),
                        pltpu.VMEM((2, NA_GROUP * PAIR_ROWS, NA_BAND + CTX_LEN), BF16),
                        pltpu.VMEM((2, NA_GROUP * PAIR_ROWS, LANES), F32)],
        compiler_params=_params(2),
        name="na",
    )(rpb, nq, nkt, nv, cnkt, cnv)


def _out_kernel(x_ref, ret_ref, na_ref, mod_ref, gpm_ref, gpre_ref, gpost_ref, wo_ref, w1_ref, w2_ref, o_ref):
    gt1, sh2, sc2, gt2 = (mod_ref[0, i:i + 1, :] for i in (2, 3, 4, 5))
    subs = [slice(i * SUB_TILE, (i + 1) * SUB_TILE) for i in range(ROW_TILE // SUB_TILE)]
    mix = [jnp.dot(ret_ref[0, rows, :], wo_ref[0:RET_WIDTH, :], preferred_element_type=F32)
           + jnp.dot(na_ref[0, rows, :], wo_ref[RET_WIDTH:, :], preferred_element_type=F32) for rows in subs]
    x1 = [x_ref[0, rows, :] + gt1 * _rms(m, gpm_ref[...]) for rows, m in zip(subs, mix)]
    h2 = [_prenorm(x, gpre_ref[...], sh2, sc2) for x in x1]
    mlp = []
    for h in h2:
        acc = jnp.zeros((SUB_TILE, D_MODEL), F32)
        for ci in range(D_FF // FF_CHUNK):
            cols = slice(ci * FF_CHUNK, (ci + 1) * FF_CHUNK)
            a = jnp.maximum(jnp.dot(h, w1_ref[:, cols], preferred_element_type=F32), 0.0)
            acc = acc + jnp.dot((a * a).astype(BF16), w2_ref[cols, :], preferred_element_type=F32)
        mlp.append(acc)
    for rows, x, y in zip(subs, x1, mlp):
        o_ref[0, rows, :] = x + gt2 * _rms(y, gpost_ref[...])


def _out_proj_mlp(x, ret_lat, na_lat, mods, g_post_mix, g_pre_mlp, g_post_mlp, w_out, w_mlp1, w_mlp2):
    batch = x.shape[0]
    tok = lambda b, i: (b, i, 0)
    gain = pl.BlockSpec((1, D_MODEL), lambda b, i: (0, 0))
    return pl.pallas_call(
        _out_kernel,
        grid=(batch, SEQ // ROW_TILE),
        in_specs=[pl.BlockSpec((1, ROW_TILE, D_MODEL), tok),
                  pl.BlockSpec((1, ROW_TILE, RET_WIDTH), tok),
                  pl.BlockSpec((1, ROW_TILE, NA_WIDTH), tok),
                  pl.BlockSpec((1, N_MOD, D_MODEL), lambda b, i: (b, 0, 0)),
                  gain, gain, gain,
                  _resident((RET_WIDTH + NA_WIDTH, D_MODEL)),
                  _resident((D_MODEL, D_FF)),
                  _resident((D_FF, D_MODEL))],
        out_specs=pl.BlockSpec((1, ROW_TILE, D_MODEL), tok),
        out_shape=jax.ShapeDtypeStruct(x.shape, F32),
        compiler_params=_params(2),
        name="out_mlp",
    )(x, ret_lat, na_lat, mods, g_post_mix, g_pre_mlp, g_post_mlp, w_out, w_mlp1, w_mlp2)


def kernel(x, c, ctx, c_ctx, w_ada, b_ada, g_pre_mix, g_post_mix, g_pre_mlp, g_post_mlp,
           w_in, ret_decay, ret_gn, na_rpb, w_out, w_mlp1, w_mlp2):
    assert w_in.shape[0] == 1, "single-layer block: the context stream is never updated"
    batch = x.shape[0]
    pad = (-(batch + 1)) % 8
    cc = jnp.concatenate([c, c_ctx[None, :], jnp.zeros((pad, D_MODEL), F32)], axis=0)
    mods = _modulations(cc, w_ada[0], b_ada)
    mods_lat = mods[:batch].reshape(batch, N_MOD, D_MODEL)
    mods_ctx = mods[batch:batch + 1].reshape(1, N_MOD, D_MODEL)

    w_in_b = w_in[0].astype(BF16)
    rq, rk, rv, rg, nq, nkt, nv = _in_proj_lat(x, mods_lat, g_pre_mix, w_in_b)
    crk, crv, cnkt, cnv = _in_proj_ctx(ctx, mods_ctx, g_pre_mix, w_in_b)

    log_gammas = jax.nn.log_sigmoid(ret_decay[0].astype(F32))
    ret_lat = _retention(log_gammas, rq, rk, rv, rg, crk, crv, ret_gn)
    na_lat = _neighbourhood_attention(na_rpb[0].astype(F32).reshape(-1), nq, nkt, nv, cnkt, cnv)

    return _out_proj_mlp(x, ret_lat, na_lat, mods_lat, g_post_mix, g_pre_mlp, g_post_mlp,
                         w_out[0].astype(BF16), w_mlp1[0].astype(BF16), w_mlp2[0].astype(BF16))
```

```python
import numpy as np
import jax
import jax.numpy as jnp
from jax import lax
from jax.experimental import pallas as pl
from jax.experimental.pallas import tpu as pltpu

D_MODEL = 1024
SEQ = 2048
CTX_LEN = 256
GRID_W = 64
GRID_ROWS = SEQ // GRID_W
RET_HEADS = 4
RET_DIM = 128
RET_WIDTH = RET_HEADS * RET_DIM
NA_HEADS = 8
NA_DIM = 64
NA_WIDTH = NA_HEADS * NA_DIM
NA_KH = 8
NA_KW = 16
N_GROUPS = 7
GROUP_W = 512
_K_GROUPS = (1, 5)
D_FF = 4 * D_MODEL
ROPE_BASE = 10000.0
NORM_EPS = 1e-6
N_MOD = 6
NEG_INF = -1e30
LOG2_E = 1.4426950408889634

LANES = 128
ROW_TILE = 512
SUB_TILE = 256
FF_CHUNK = 1024
RET_CHUNK = 256
NA_BAND = NA_KH * GRID_W
N_PATTERNS = 8
VMEM_LIMIT = 56 * 1024 * 1024

F32 = jnp.float32
BF16 = jnp.bfloat16
_NT = (((1,), (1,)), ((), ()))
_TN = (((0,), (0,)), ((), ()))


def _silu(x):
    return x * (1.0 / (1.0 + jnp.exp(-x)))


def _rms(x, g):
    return x * lax.rsqrt(jnp.mean(x * x, axis=-1, keepdims=True) + NORM_EPS) * g


def _params(n_axes):
    return pltpu.CompilerParams(dimension_semantics=("arbitrary",) * n_axes,
                                vmem_limit_bytes=VMEM_LIMIT)


def _resident(shape):
    nd = len(shape)
    return pl.BlockSpec(shape, lambda *_: (0,) * nd, pipeline_mode=pl.Buffered(1))


def _mod_kernel(c_ref, w_ref, b_ref, o_ref):
    a = _silu(c_ref[...]).astype(BF16)
    o_ref[...] = jnp.dot(a, w_ref[...].astype(BF16), preferred_element_type=F32) + b_ref[...]


def _modulations(cc, w_ada, b_ada):
    rows = cc.shape[0]
    return pl.pallas_call(
        _mod_kernel,
        grid=(N_MOD,),
        in_specs=[pl.BlockSpec((rows, D_MODEL), lambda j: (0, 0)),
                  pl.BlockSpec((D_MODEL, D_MODEL), lambda j: (0, j)),
                  pl.BlockSpec((1, D_MODEL), lambda j: (0, j))],
        out_specs=pl.BlockSpec((rows, D_MODEL), lambda j: (0, j)),
        out_shape=jax.ShapeDtypeStruct((rows, N_MOD * D_MODEL), F32),
        compiler_params=_params(1),
        name="mod",
    )(cc, w_ada, b_ada)


def _prenorm(x, g, shift, scale):
    return (_rms(x, g) * (1.0 + scale) + shift).astype(BF16)


def _rope(blk, cos, sin_signed, first_half):
    partner = jnp.where(first_half, pltpu.roll(blk, LANES - 32, 1), pltpu.roll(blk, 32, 1))
    return blk * cos + partner * sin_signed


def _in_lat_kernel(x_ref, mod_ref, g_ref, w_ref, cq_ref, sq_ref, ck_ref, sk_ref, *out_refs):
    h = _prenorm(x_ref[0], g_ref[...], mod_ref[0, 0:1, :], mod_ref[0, 1:2, :])
    lane = lax.broadcasted_iota(jnp.int32, (1, LANES), 1)
    first_half = (lane & 32) == 0
    for gi, o_ref in enumerate(out_refs):
        acc = jnp.dot(h, w_ref[:, gi * GROUP_W:(gi + 1) * GROUP_W], preferred_element_type=F32)
        if gi == 0:
            for hh in range(RET_HEADS):
                blk = _rope(acc[:, hh * LANES:(hh + 1) * LANES], cq_ref[...], sq_ref[...], first_half)
                o_ref[0, :, hh * LANES:(hh + 1) * LANES] = blk.astype(BF16)
        elif gi == 1:
            for hh in range(RET_HEADS):
                blk = _rope(acc[:, hh * LANES:(hh + 1) * LANES], ck_ref[...], sk_ref[...], first_half)
                o_ref[0, hh * LANES:(hh + 1) * LANES, :] = blk.T.astype(BF16)
        elif gi == 4:
            o_ref[0] = (acc * (NA_DIM ** -0.5 * LOG2_E)).astype(BF16)
        elif gi in _K_GROUPS:
            o_ref[0] = acc.T.astype(BF16)
        else:
            o_ref[0] = acc.astype(BF16)


_CTX_GROUPS = (1, 2, 5, 6)


def _in_ctx_kernel(x_ref, mod_ref, g_ref, w_ref, *out_refs):
    h = _prenorm(x_ref[0], g_ref[...], mod_ref[0, 0:1, :], mod_ref[0, 1:2, :])
    for gi, o_ref in zip(_CTX_GROUPS, out_refs):
        acc = jnp.dot(h, w_ref[:, gi * GROUP_W:(gi + 1) * GROUP_W], preferred_element_type=F32)
        o_ref[0] = (acc.T if gi in _K_GROUPS else acc).astype(BF16)


def _rope_tables():
    tok = np.arange(SEQ)
    n_freq = RET_DIM // 4
    inv = ROPE_BASE ** (-np.arange(n_freq, dtype=np.float64) / n_freq)
    ang_r = (tok // GRID_W)[:, None] * inv[None, :]
    ang_c = (tok % GRID_W)[:, None] * inv[None, :]
    cos = np.concatenate([np.cos(ang_r), np.cos(ang_r), np.cos(ang_c), np.cos(ang_c)], axis=-1)
    sin = np.concatenate([-np.sin(ang_r), np.sin(ang_r), -np.sin(ang_c), np.sin(ang_c)], axis=-1)
    qs = RET_DIM ** -0.5
    return [jnp.asarray(t, F32) for t in (cos * qs, sin * qs, cos, sin)]


def _in_proj_lat(x, mods, g_pre, w_in):
    batch = x.shape[0]
    tok = lambda b, i: (b, i, 0)
    tab = pl.BlockSpec((ROW_TILE, LANES), lambda b, i: (i, 0))
    out_shapes = [jax.ShapeDtypeStruct((batch, SEQ, GROUP_W), BF16)] * N_GROUPS
    out_specs = [pl.BlockSpec((1, ROW_TILE, GROUP_W), tok)] * N_GROUPS
    for gi in _K_GROUPS:
        out_shapes[gi] = jax.ShapeDtypeStruct((batch, GROUP_W, SEQ), BF16)
        out_specs[gi] = pl.BlockSpec((1, GROUP_W, ROW_TILE), lambda b, i: (b, 0, i))
    return pl.pallas_call(
        _in_lat_kernel,
        grid=(batch, SEQ // ROW_TILE),
        in_specs=[pl.BlockSpec((1, ROW_TILE, D_MODEL), tok),
                  pl.BlockSpec((1, N_MOD, D_MODEL), lambda b, i: (b, 0, 0)),
                  pl.BlockSpec((1, D_MODEL), lambda b, i: (0, 0)),
                  _resident((D_MODEL, N_GROUPS * GROUP_W)),
                  tab, tab, tab, tab],
        out_specs=out_specs,
        out_shape=out_shapes,
        compiler_params=_params(2),
        name="in_lat",
    )(x, mods, g_pre, w_in, *_rope_tables())


def _in_proj_ctx(ctx, mods_ctx, g_pre, w_in):
    batch = ctx.shape[0]
    tok = lambda b: (b, 0, 0)
    out_shapes = [jax.ShapeDtypeStruct((batch, GROUP_W, CTX_LEN) if gi in _K_GROUPS else (batch, CTX_LEN, GROUP_W),
                                       BF16) for gi in _CTX_GROUPS]
    out_specs = [pl.BlockSpec((1, GROUP_W, CTX_LEN) if gi in _K_GROUPS else (1, CTX_LEN, GROUP_W), tok)
                 for gi in _CTX_GROUPS]
    return pl.pallas_call(
        _in_ctx_kernel,
        grid=(batch,),
        in_specs=[pl.BlockSpec((1, CTX_LEN, D_MODEL), tok),
                  pl.BlockSpec((1, N_MOD, D_MODEL), lambda b: (0, 0, 0)),
                  pl.BlockSpec((1, D_MODEL), lambda b: (0, 0)),
                  _resident((D_MODEL, N_GROUPS * GROUP_W))],
        out_specs=out_specs,
        out_shape=out_shapes,
        compiler_params=_params(1),
        name="in_ctx",
    )(ctx, mods_ctx, g_pre, w_in)


def _ret_kernel(lg_ref, q_ref, kt_ref, v_ref, g_ref, kct_ref, vc_ref, gn_ref, o_ref,
                dec_ref, wq_ref, wk_ref, upd_ref, st_ref, sp_ref):
    head = pl.program_id(0)
    lgf = lg_ref[0, head]
    lgb = lg_ref[1, head]
    c = RET_CHUNK
    n_chunks = SEQ // c

    @pl.when(pl.program_id(1) == 0)
    def _tables():
        ii = lax.broadcasted_iota(jnp.int32, (c, c), 0)
        jj = lax.broadcasted_iota(jnp.int32, (c, c), 1)
        diff = (ii - jj).astype(F32)
        dec_ref[...] = jnp.exp(jnp.where(diff >= 0, lgf * diff, -lgb * diff))
        ri = lax.broadcasted_iota(jnp.int32, (c, LANES), 0).astype(F32)
        wq_ref[0] = jnp.exp(lgf * (ri + 1.0))
        wq_ref[1] = jnp.exp(lgb * (c - ri))
        ti = lax.broadcasted_iota(jnp.int32, (1, c), 1).astype(F32)
        wk_ref[0] = jnp.exp(lgf * (c - 1.0 - ti))
        wk_ref[1] = jnp.exp(lgb * ti)

    ones = jnp.ones((1, LANES), F32)
    chunk_f = jnp.exp(ones * (lgf * c))
    chunk_b = jnp.exp(ones * (lgb * c))

    def update(kt, v):
        ktf = kt.astype(F32)
        kw = jnp.concatenate([(ktf * wk_ref[0]).astype(BF16), (ktf * wk_ref[1]).astype(BF16)], axis=0)
        return jnp.dot(kw, v, preferred_element_type=F32)

    def chunk(n):
        return slice(n * c, (n + 1) * c)

    for n in range(n_chunks):
        upd_ref[n] = update(kt_ref[0, :, chunk(n)], v_ref[0, chunk(n), :])

    ctx_upd = update(kct_ref[0], vc_ref[0])
    state = ctx_upd[:RET_DIM]
    for n in range(n_chunks):
        st_ref[n, 0:RET_DIM, :] = state.astype(BF16)
        state = chunk_f * state + upd_ref[n, 0:RET_DIM, :]
    state = ctx_upd[RET_DIM:]
    for n in reversed(range(n_chunks)):
        st_ref[n, RET_DIM:, :] = state.astype(BF16)
        state = chunk_b * state + upd_ref[n, RET_DIM:, :]

    def scores(n, slot):
        s = jnp.dot(q_ref[0, chunk(n), :], kt_ref[0, :, chunk(n)], preferred_element_type=F32)
        sp_ref[slot] = (s * dec_ref[...]).astype(BF16)

    def outputs(n, slot):
        rows = chunk(n)
        qf = q_ref[0, rows, :].astype(F32)
        qw = jnp.concatenate([(qf * wq_ref[0]).astype(BF16), (qf * wq_ref[1]).astype(BF16)], axis=1)
        o = (jnp.dot(sp_ref[slot], v_ref[0, rows, :], preferred_element_type=F32)
             + jnp.dot(qw, st_ref[n], preferred_element_type=F32))
        d = o - jnp.mean(o, axis=-1, keepdims=True)
        y = d * lax.rsqrt(jnp.mean(d * d, axis=-1, keepdims=True) + NORM_EPS) * gn_ref[...]
        o_ref[0, rows, :] = (y * _silu(g_ref[0, rows, :].astype(F32))).astype(BF16)

    scores(0, 0)
    for n in range(n_chunks):
        if n + 1 < n_chunks:
            scores(n + 1, (n + 1) % 2)
        outputs(n, n % 2)


def _retention(log_gammas, rq, rkt, rv, rg, crkt, crv, gn_w):
    assert CTX_LEN == RET_CHUNK
    batch = rq.shape[0]
    n_chunks = SEQ // RET_CHUNK
    lat = pl.BlockSpec((1, SEQ, LANES), lambda h, b: (b, 0, h))
    cx = pl.BlockSpec((1, CTX_LEN, LANES), lambda h, b: (b, 0, h))
    return pl.pallas_call(
        _ret_kernel,
        grid=(RET_HEADS, batch),
        in_specs=[pl.BlockSpec(memory_space=pltpu.SMEM), lat,
                  pl.BlockSpec((1, LANES, SEQ), lambda h, b: (b, h, 0)), lat, lat,
                  pl.BlockSpec((1, LANES, CTX_LEN), lambda h, b: (b, h, 0)), cx,
                  pl.BlockSpec((1, LANES), lambda h, b: (0, h))],
        out_specs=lat,
        out_shape=jax.ShapeDtypeStruct((batch, SEQ, RET_WIDTH), BF16),
        scratch_shapes=[pltpu.VMEM((RET_CHUNK, RET_CHUNK), F32),
                        pltpu.VMEM((2, RET_CHUNK, LANES), F32),
                        pltpu.VMEM((2, 1, RET_CHUNK), F32),
                        pltpu.VMEM((n_chunks, 2 * RET_DIM, RET_DIM), F32),
                        pltpu.VMEM((n_chunks, 2 * RET_DIM, RET_DIM), BF16),
                        pltpu.VMEM((2, RET_CHUNK, RET_CHUNK), BF16)],
        compiler_params=_params(2),
        name="ret",
    )(log_gammas, rq, rkt, rv, rg, crkt, crv, gn_w)


N_DR = 2 * NA_KH - 1
N_DC = 2 * NA_KW - 1
PAIR_ROWS = 2 * GRID_W
NA_GROUP = 4


def _na_build_bias(rpb_ref, bias_ref, pair):
    qi = lax.broadcasted_iota(jnp.int32, (GRID_W, LANES), 0)
    li = lax.broadcasted_iota(jnp.int32, (GRID_W, LANES), 1)
    key_col = li & (GRID_W - 1)
    col_start = jnp.clip(qi - NA_KW // 2, 0, GRID_W - NA_KW)
    valid = (key_col >= col_start) & (key_col < col_start + NA_KW)
    code = jnp.where(valid, key_col - qi + (NA_KW - 1), -1)
    first_row = lax.broadcasted_iota(jnp.int32, (1, LANES), 1) < GRID_W

    def pattern(t, carry):
        for hd in range(2):
            head = 2 * pair + hd
            for blk in range(NA_KH // 2):
                base = (head * N_DR + (2 * blk + NA_KH - 1 - t)) * N_DC
                acc = jnp.full((GRID_W, LANES), NEG_INF, F32)
                for b in range(N_DC):
                    val = jnp.where(first_row, rpb_ref[base + b], rpb_ref[base + N_DC + b]) * LOG2_E
                    acc = jnp.where(code == b, val, acc)
                bias_ref[t, hd * GRID_W:(hd + 1) * GRID_W, blk * LANES:(blk + 1) * LANES] = acc
        return carry

    lax.fori_loop(0, N_PATTERNS, pattern, 0)


def _na_kernel(rpb_ref, q_ref, kt_ref, v_ref, kct_ref, vc_ref, o_ref,
               bias_ref, kt2_ref, q2_ref, s_ref):
    pair = pl.program_id(0)
    low = lax.broadcasted_iota(jnp.int32, (1, LANES), 1) < NA_DIM
    half = NA_KH // 2

    @pl.when(pl.program_id(1) == 0)
    def _bias():
        _na_build_bias(rpb_ref, bias_ref, pair)

    kt2_ref[0] = kt_ref[0]
    kt2_ref[1, :, 0:SEQ - LANES] = kt_ref[0, :, GRID_W:SEQ - GRID_W]
    kt2_ref[1, :, SEQ - LANES:SEQ] = jnp.zeros((LANES, LANES), BF16)

    def fill_q2(r, carry):
        q = q_ref[0, pl.ds(pl.multiple_of(r * GRID_W, GRID_W), GRID_W), :]
        zero = jnp.zeros_like(q)
        base = pl.multiple_of(r * PAIR_ROWS, PAIR_ROWS)
        q2_ref[pl.ds(base, GRID_W), :] = jnp.where(low, q, zero)
        q2_ref[pl.ds(base + GRID_W, GRID_W), :] = jnp.where(low, zero, q)
        return carry

    lax.fori_loop(0, GRID_ROWS, fill_q2, 0, unroll=4)

    def band_start(r):
        return jnp.clip(r - half, 0, GRID_ROWS - NA_KH)

    def scores(group, slot):
        rows = group * (NA_GROUP * PAIR_ROWS)
        group_q = pl.ds(pl.multiple_of(rows, NA_GROUP * PAIR_ROWS), NA_GROUP * PAIR_ROWS)
        s_ref[slot, :, NA_BAND:] = jnp.dot(q2_ref[group_q, :], kct_ref[0], preferred_element_type=F32)
        for j in range(NA_GROUP):
            r = group * NA_GROUP + j
            start = band_start(r)
            pattern = jnp.where(r < half, r, jnp.where(r > GRID_ROWS - half, r - (GRID_ROWS - NA_KH), half))
            lanes = pl.ds(pl.multiple_of((start >> 1) * LANES, LANES), NA_BAND)
            kb = kt2_ref[start & 1, :, lanes]
            q2 = q2_ref[pl.ds(pl.multiple_of(r * PAIR_ROWS, PAIR_ROWS), PAIR_ROWS), :]
            s_ref[slot, j * PAIR_ROWS:(j + 1) * PAIR_ROWS, 0:NA_BAND] = (
                jnp.dot(q2, kb, preferred_element_type=F32) + bias_ref[pattern])

    n_blocks = (NA_BAND + CTX_LEN) // LANES

    def outputs(group, slot):
        for j in range(NA_GROUP):
            r = group * NA_GROUP + j
            rows = slice(j * PAIR_ROWS, (j + 1) * PAIR_ROWS)

            def block(b):
                return s_ref[slot, rows, b * LANES:(b + 1) * LANES]

            m = block(0)
            for b in range(1, n_blocks):
                m = jnp.maximum(m, block(b))
            m = jnp.max(m, axis=-1, keepdims=True)
            p = [jnp.exp2(block(b) - m) for b in range(n_blocks)]
            den = p[0]
            for pb in p[1:]:
                den = den + pb
            den = jnp.sum(den, axis=-1, keepdims=True)
            p = jnp.concatenate([pb.astype(BF16) for pb in p], axis=1)
            vb = v_ref[0, pl.ds(pl.multiple_of(band_start(r) * GRID_W, GRID_W), NA_BAND), :]
            o = (jnp.dot(p[:, 0:NA_BAND], vb, preferred_element_type=F32)
                 + jnp.dot(p[:, NA_BAND:], vc_ref[0], preferred_element_type=F32)) * (1.0 / den)
            o_ref[0, pl.ds(pl.multiple_of(r * GRID_W, GRID_W), GRID_W), :] = (
                jnp.where(low, o[:GRID_W], o[GRID_W:]).astype(BF16))

    n_groups = GRID_ROWS // NA_GROUP
    scores(0, 0)
    for g in range(n_groups):
        if g + 1 < n_groups:
            scores(g + 1, (g + 1) % 2)
        outputs(g, g % 2)


def _neighbourhood_attention(rpb, nq, nkt, nv, cnkt, cnv):
    batch = nq.shape[0]
    n_q2 = GRID_ROWS * PAIR_ROWS
    lat = pl.BlockSpec((1, SEQ, LANES), lambda p, b: (b, 0, p))
    cx = pl.BlockSpec((1, CTX_LEN, LANES), lambda p, b: (b, 0, p))
    return pl.pallas_call(
        _na_kernel,
        grid=(NA_HEADS // 2, batch),
        in_specs=[pl.BlockSpec(memory_space=pltpu.SMEM), lat,
                  pl.BlockSpec((1, LANES, SEQ), lambda p, b: (b, p, 0)), lat,
                  pl.BlockSpec((1, LANES, CTX_LEN), lambda p, b: (b, p, 0)), cx],
        out_specs=lat,
        out_shape=jax.ShapeDtypeStruct((batch, SEQ, NA_WIDTH), BF16),
        scratch_shapes=[pltpu.VMEM((N_PATTERNS, PAIR_ROWS, NA_BAND), F32),
                        pltpu.VMEM((2, LANES, SEQ), BF16),
                        pltpu.VMEM((n_q2, LANES), BF16),
                        pltpu.VMEM((2, NA_GROUP * PAIR_ROWS, NA_BAND + CTX_LEN), F32)],
        compiler_params=_params(2),
        name="na",
    )(rpb, nq, nkt, nv, cnkt, cnv)


def _out_kernel(x_ref, ret_ref, na_ref, mod_ref, gpm_ref, gpre_ref, gpost_ref, wo_ref, w1_ref, w2_ref, o_ref):
    gt1, sh2, sc2, gt2 = (mod_ref[0, i:i + 1, :] for i in (2, 3, 4, 5))
    subs = [slice(i * SUB_TILE, (i + 1) * SUB_TILE) for i in range(ROW_TILE // SUB_TILE)]
    mix = [jnp.dot(ret_ref[0, rows, :], wo_ref[0:RET_WIDTH, :], preferred_element_type=F32)
           + jnp.dot(na_ref[0, rows, :], wo_ref[RET_WIDTH:, :], preferred_element_type=F32) for rows in subs]
    x1 = [x_ref[0, rows, :] + gt1 * _rms(m, gpm_ref[...]) for rows, m in zip(subs, mix)]
    h2 = [_prenorm(x, gpre_ref[...], sh2, sc2) for x in x1]
    mlp = []
    for h in h2:
        acc = jnp.zeros((SUB_TILE, D_MODEL), F32)
        for ci in range(D_FF // FF_CHUNK):
            cols = slice(ci * FF_CHUNK, (ci + 1) * FF_CHUNK)
            a = jnp.maximum(jnp.dot(h, w1_ref[:, cols], preferred_element_type=F32), 0.0)
            acc = acc + jnp.dot((a * a).astype(BF16), w2_ref[cols, :], preferred_element_type=F32)
        mlp.append(acc)
    for rows, x, y in zip(subs, x1, mlp):
        o_ref[0, rows, :] = x + gt2 * _rms(y, gpost_ref[...])


def _out_proj_mlp(x, ret_lat, na_lat, mods, g_post_mix, g_pre_mlp, g_post_mlp, w_out, w_mlp1, w_mlp2):
    batch = x.shape[0]
    tok = lambda b, i: (b, i, 0)
    gain = pl.BlockSpec((1, D_MODEL), lambda b, i: (0, 0))
    return pl.pallas_call(
        _out_kernel,
        grid=(batch, SEQ // ROW_TILE),
        in_specs=[pl.BlockSpec((1, ROW_TILE, D_MODEL), tok),
                  pl.BlockSpec((1, ROW_TILE, RET_WIDTH), tok),
                  pl.BlockSpec((1, ROW_TILE, NA_WIDTH), tok),
                  pl.BlockSpec((1, N_MOD, D_MODEL), lambda b, i: (b, 0, 0)),
                  gain, gain, gain,
                  _resident((RET_WIDTH + NA_WIDTH, D_MODEL)),
                  _resident((D_MODEL, D_FF)),
                  _resident((D_FF, D_MODEL))],
        out_specs=pl.BlockSpec((1, ROW_TILE, D_MODEL), tok),
        out_shape=jax.ShapeDtypeStruct(x.shape, F32),
        compiler_params=_params(2),
        name="out_mlp",
    )(x, ret_lat, na_lat, mods, g_post_mix, g_pre_mlp, g_post_mlp, w_out, w_mlp1, w_mlp2)


def kernel(x, c, ctx, c_ctx, w_ada, b_ada, g_pre_mix, g_post_mix, g_pre_mlp, g_post_mlp,
           w_in, ret_decay, ret_gn, na_rpb, w_out, w_mlp1, w_mlp2):
    assert w_in.shape[0] == 1, "single-layer block: the context stream is never updated"
    batch = x.shape[0]
    pad = (-(batch + 1)) % 8
    cc = jnp.concatenate([c, c_ctx[None, :], jnp.zeros((pad, D_MODEL), F32)], axis=0)
    mods = _modulations(cc, w_ada[0], b_ada)
    mods_lat = mods[:batch].reshape(batch, N_MOD, D_MODEL)
    mods_ctx = mods[batch:batch + 1].reshape(1, N_MOD, D_MODEL)

    w_in_b = w_in[0].astype(BF16)
    rq, rk, rv, rg, nq, nkt, nv = _in_proj_lat(x, mods_lat, g_pre_mix, w_in_b)
    crk, crv, cnkt, cnv = _in_proj_ctx(ctx, mods_ctx, g_pre_mix, w_in_b)

    log_gammas = jax.nn.log_sigmoid(ret_decay[0].astype(F32))
    ret_lat = _retention(log_gammas, rq, rk, rv, rg, crk, crv, ret_gn)
    na_lat = _neighbourhood_attention(na_rpb[0].astype(F32).reshape(-1), nq, nkt, nv, cnkt, cnv)

    return _out_proj_mlp(x, ret_lat, na_lat, mods_lat, g_post_mix, g_pre_mlp, g_post_mlp,
                         w_out[0].astype(BF16), w_mlp1[0].astype(BF16), w_mlp2[0].astype(BF16))
```

```python
import numpy as np
import jax
import jax.numpy as jnp
from jax import lax
from jax.experimental import pallas as pl
from jax.experimental.pallas import tpu as pltpu

D_MODEL = 1024
SEQ = 2048
CTX_LEN = 256
GRID_W = 64
GRID_ROWS = SEQ // GRID_W
RET_HEADS = 4
RET_DIM = 128
RET_WIDTH = RET_HEADS * RET_DIM
NA_HEADS = 8
NA_DIM = 64
NA_WIDTH = NA_HEADS * NA_DIM
NA_KH = 8
NA_KW = 16
N_GROUPS = 7
GROUP_W = 512
_K_GROUPS = (1, 5)
D_FF = 4 * D_MODEL
ROPE_BASE = 10000.0
NORM_EPS = 1e-6
N_MOD = 6
NEG_INF = -1e30
LOG2_E = 1.4426950408889634

LANES = 128
ROW_TILE = 1024
CTX_BATCHES = 2
OUT_TILE = 512
SUB_TILE = 256
FF_CHUNK = 1024
RET_CHUNK = 256
NA_BAND = NA_KH * GRID_W
N_PATTERNS = 8
VMEM_LIMIT = 56 * 1024 * 1024

F32 = jnp.float32
BF16 = jnp.bfloat16
_NT = (((1,), (1,)), ((), ()))
_TN = (((0,), (0,)), ((), ()))


def _silu(x):
    return x * (1.0 / (1.0 + jnp.exp(-x)))


def _rms(x, g):
    return x * lax.rsqrt(jnp.mean(x * x, axis=-1, keepdims=True) + NORM_EPS) * g


def _params(n_axes):
    return pltpu.CompilerParams(dimension_semantics=("arbitrary",) * n_axes,
                                vmem_limit_bytes=VMEM_LIMIT)


def _resident(shape):
    nd = len(shape)
    return pl.BlockSpec(shape, lambda *_: (0,) * nd, pipeline_mode=pl.Buffered(1))


def _mod_kernel(c_ref, w_ref, b_ref, o_ref):
    a = _silu(c_ref[...]).astype(BF16)
    o_ref[...] = jnp.dot(a, w_ref[...].astype(BF16), preferred_element_type=F32) + b_ref[...]


def _modulations(cc, w_ada, b_ada):
    rows = cc.shape[0]
    return pl.pallas_call(
        _mod_kernel,
        grid=(N_MOD,),
        in_specs=[pl.BlockSpec((rows, D_MODEL), lambda j: (0, 0)),
                  pl.BlockSpec((D_MODEL, D_MODEL), lambda j: (0, j)),
                  pl.BlockSpec((1, D_MODEL), lambda j: (0, j))],
        out_specs=pl.BlockSpec((rows, D_MODEL), lambda j: (0, j)),
        out_shape=jax.ShapeDtypeStruct((rows, N_MOD * D_MODEL), F32),
        compiler_params=_params(1),
        name="mod",
    )(cc, w_ada, b_ada)


def _prenorm(x, g, shift, scale):
    return (_rms(x, g) * (1.0 + scale) + shift).astype(BF16)


def _rope(blk, cos, sin_signed, first_half):
    partner = jnp.where(first_half, pltpu.roll(blk, LANES - 32, 1), pltpu.roll(blk, 32, 1))
    return blk * cos + partner * sin_signed


def _in_lat_kernel(x_ref, mod_ref, g_ref, w_ref, cq_ref, sq_ref, ck_ref, sk_ref, *out_refs):
    h = _prenorm(x_ref[0], g_ref[...], mod_ref[0, 0:1, :], mod_ref[0, 1:2, :])
    lane = lax.broadcasted_iota(jnp.int32, (1, LANES), 1)
    first_half = (lane & 32) == 0
    for gi, o_ref in enumerate(out_refs):
        acc = jnp.dot(h, w_ref[:, gi * GROUP_W:(gi + 1) * GROUP_W], preferred_element_type=F32)
        if gi == 0:
            for hh in range(RET_HEADS):
                blk = _rope(acc[:, hh * LANES:(hh + 1) * LANES], cq_ref[...], sq_ref[...], first_half)
                o_ref[0, :, hh * LANES:(hh + 1) * LANES] = blk.astype(BF16)
        elif gi == 1:
            for hh in range(RET_HEADS):
                blk = _rope(acc[:, hh * LANES:(hh + 1) * LANES], ck_ref[...], sk_ref[...], first_half)
                o_ref[0, hh * LANES:(hh + 1) * LANES, :] = blk.T.astype(BF16)
        elif gi == 4:
            o_ref[0] = (acc * (NA_DIM ** -0.5 * LOG2_E)).astype(BF16)
        elif gi in _K_GROUPS:
            o_ref[0] = acc.T.astype(BF16)
        else:
            o_ref[0] = acc.astype(BF16)


_CTX_GROUPS = (1, 2, 5, 6)


def _in_ctx_kernel(x_ref, mod_ref, g_ref, w_ref, *out_refs):
    x = jnp.concatenate([x_ref[i] for i in range(CTX_BATCHES)], axis=0)
    h = _prenorm(x, g_ref[...], mod_ref[0, 0:1, :], mod_ref[0, 1:2, :])
    for gi, o_ref in zip(_CTX_GROUPS, out_refs):
        acc = jnp.dot(h, w_ref[:, gi * GROUP_W:(gi + 1) * GROUP_W], preferred_element_type=F32)
        acc = (acc.T if gi in _K_GROUPS else acc).astype(BF16)
        for i in range(CTX_BATCHES):
            tokens = slice(i * CTX_LEN, (i + 1) * CTX_LEN)
            o_ref[i] = acc[:, tokens] if gi in _K_GROUPS else acc[tokens, :]


def _rope_tables():
    tok = np.arange(SEQ)
    n_freq = RET_DIM // 4
    inv = ROPE_BASE ** (-np.arange(n_freq, dtype=np.float64) / n_freq)
    ang_r = (tok // GRID_W)[:, None] * inv[None, :]
    ang_c = (tok % GRID_W)[:, None] * inv[None, :]
    cos = np.concatenate([np.cos(ang_r), np.cos(ang_r), np.cos(ang_c), np.cos(ang_c)], axis=-1)
    sin = np.concatenate([-np.sin(ang_r), np.sin(ang_r), -np.sin(ang_c), np.sin(ang_c)], axis=-1)
    qs = RET_DIM ** -0.5
    return [jnp.asarray(t, F32) for t in (cos * qs, sin * qs, cos, sin)]


def _in_proj_lat(x, mods, g_pre, w_in):
    batch = x.shape[0]
    tok = lambda b, i: (b, i, 0)
    tab = pl.BlockSpec((ROW_TILE, LANES), lambda b, i: (i, 0))
    out_shapes = [jax.ShapeDtypeStruct((batch, SEQ, GROUP_W), BF16)] * N_GROUPS
    out_specs = [pl.BlockSpec((1, ROW_TILE, GROUP_W), tok)] * N_GROUPS
    for gi in _K_GROUPS:
        out_shapes[gi] = jax.ShapeDtypeStruct((batch, GROUP_W, SEQ), BF16)
        out_specs[gi] = pl.BlockSpec((1, GROUP_W, ROW_TILE), lambda b, i: (b, 0, i))
    return pl.pallas_call(
        _in_lat_kernel,
        grid=(batch, SEQ // ROW_TILE),
        in_specs=[pl.BlockSpec((1, ROW_TILE, D_MODEL), tok),
                  pl.BlockSpec((1, N_MOD, D_MODEL), lambda b, i: (b, 0, 0)),
                  pl.BlockSpec((1, D_MODEL), lambda b, i: (0, 0)),
                  _resident((D_MODEL, N_GROUPS * GROUP_W)),
                  tab, tab, tab, tab],
        out_specs=out_specs,
        out_shape=out_shapes,
        compiler_params=_params(2),
        name="in_lat",
    )(x, mods, g_pre, w_in, *_rope_tables())


def _in_proj_ctx(ctx, mods_ctx, g_pre, w_in):
    batch = ctx.shape[0]
    tok = lambda b: (b, 0, 0)
    out_shapes = [jax.ShapeDtypeStruct((batch, GROUP_W, CTX_LEN) if gi in _K_GROUPS else (batch, CTX_LEN, GROUP_W),
                                       BF16) for gi in _CTX_GROUPS]
    out_specs = [pl.BlockSpec((CTX_BATCHES, GROUP_W, CTX_LEN) if gi in _K_GROUPS else (CTX_BATCHES, CTX_LEN, GROUP_W),
                              tok) for gi in _CTX_GROUPS]
    return pl.pallas_call(
        _in_ctx_kernel,
        grid=(batch // CTX_BATCHES,),
        in_specs=[pl.BlockSpec((CTX_BATCHES, CTX_LEN, D_MODEL), tok),
                  pl.BlockSpec((1, N_MOD, D_MODEL), lambda b: (0, 0, 0)),
                  pl.BlockSpec((1, D_MODEL), lambda b: (0, 0)),
                  _resident((D_MODEL, N_GROUPS * GROUP_W))],
        out_specs=out_specs,
        out_shape=out_shapes,
        compiler_params=_params(1),
        name="in_ctx",
    )(ctx, mods_ctx, g_pre, w_in)


def _ret_kernel(lg_ref, q_ref, kt_ref, v_ref, g_ref, kct_ref, vc_ref, gn_ref, o_ref,
                dec_ref, wq_ref, wk_ref, upd_ref, st_ref, sp_ref):
    head = pl.program_id(0)
    lgf = lg_ref[0, head]
    lgb = lg_ref[1, head]
    c = RET_CHUNK
    n_chunks = SEQ // c

    @pl.when(pl.program_id(1) == 0)
    def _tables():
        ii = lax.broadcasted_iota(jnp.int32, (c, c), 0)
        jj = lax.broadcasted_iota(jnp.int32, (c, c), 1)
        diff = (ii - jj).astype(F32)
        dec_ref[...] = jnp.exp(jnp.where(diff >= 0, lgf * diff, -lgb * diff))
        ri = lax.broadcasted_iota(jnp.int32, (c, LANES), 0).astype(F32)
        wq_ref[0] = jnp.exp(lgf * (ri + 1.0))
        wq_ref[1] = jnp.exp(lgb * (c - ri))
        ti = lax.broadcasted_iota(jnp.int32, (1, c), 1).astype(F32)
        wk_ref[0] = jnp.exp(lgf * (c - 1.0 - ti))
        wk_ref[1] = jnp.exp(lgb * ti)

    ones = jnp.ones((1, LANES), F32)
    chunk_f = jnp.exp(ones * (lgf * c))
    chunk_b = jnp.exp(ones * (lgb * c))

    def update(kt, v):
        ktf = kt.astype(F32)
        kw = jnp.concatenate([(ktf * wk_ref[0]).astype(BF16), (ktf * wk_ref[1]).astype(BF16)], axis=0)
        return jnp.dot(kw, v, preferred_element_type=F32)

    def chunk(n):
        return slice(n * c, (n + 1) * c)

    for n in range(n_chunks):
        upd_ref[n] = update(kt_ref[0, :, chunk(n)], v_ref[0, chunk(n), :])

    ctx_upd = update(kct_ref[0], vc_ref[0])
    state = ctx_upd[:RET_DIM]
    for n in range(n_chunks):
        st_ref[n, 0:RET_DIM, :] = state.astype(BF16)
        state = chunk_f * state + upd_ref[n, 0:RET_DIM, :]
    state = ctx_upd[RET_DIM:]
    for n in reversed(range(n_chunks)):
        st_ref[n, RET_DIM:, :] = state.astype(BF16)
        state = chunk_b * state + upd_ref[n, RET_DIM:, :]

    def scores(n, slot):
        s = jnp.dot(q_ref[0, chunk(n), :], kt_ref[0, :, chunk(n)], preferred_element_type=F32)
        sp_ref[slot] = (s * dec_ref[...]).astype(BF16)

    def outputs(n, slot):
        rows = chunk(n)
        qf = q_ref[0, rows, :].astype(F32)
        qw = jnp.concatenate([(qf * wq_ref[0]).astype(BF16), (qf * wq_ref[1]).astype(BF16)], axis=1)
        o = (jnp.dot(sp_ref[slot], v_ref[0, rows, :], preferred_element_type=F32)
             + jnp.dot(qw, st_ref[n], preferred_element_type=F32))
        d = o - jnp.mean(o, axis=-1, keepdims=True)
        y = d * lax.rsqrt(jnp.mean(d * d, axis=-1, keepdims=True) + NORM_EPS) * gn_ref[...]
        o_ref[0, rows, :] = (y * _silu(g_ref[0, rows, :].astype(F32))).astype(BF16)

    scores(0, 0)
    for n in range(n_chunks):
        if n + 1 < n_chunks:
            scores(n + 1, (n + 1) % 2)
        outputs(n, n % 2)


def _retention(log_gammas, rq, rkt, rv, rg, crkt, crv, gn_w):
    assert CTX_LEN == RET_CHUNK
    batch = rq.shape[0]
    n_chunks = SEQ // RET_CHUNK
    lat = pl.BlockSpec((1, SEQ, LANES), lambda h, b: (b, 0, h))
    cx = pl.BlockSpec((1, CTX_LEN, LANES), lambda h, b: (b, 0, h))
    return pl.pallas_call(
        _ret_kernel,
        grid=(RET_HEADS, batch),
        in_specs=[pl.BlockSpec(memory_space=pltpu.SMEM), lat,
                  pl.BlockSpec((1, LANES, SEQ), lambda h, b: (b, h, 0)), lat, lat,
                  pl.BlockSpec((1, LANES, CTX_LEN), lambda h, b: (b, h, 0)), cx,
                  pl.BlockSpec((1, LANES), lambda h, b: (0, h))],
        out_specs=lat,
        out_shape=jax.ShapeDtypeStruct((batch, SEQ, RET_WIDTH), BF16),
        scratch_shapes=[pltpu.VMEM((RET_CHUNK, RET_CHUNK), F32),
                        pltpu.VMEM((2, RET_CHUNK, LANES), F32),
                        pltpu.VMEM((2, 1, RET_CHUNK), F32),
                        pltpu.VMEM((n_chunks, 2 * RET_DIM, RET_DIM), F32),
                        pltpu.VMEM((n_chunks, 2 * RET_DIM, RET_DIM), BF16),
                        pltpu.VMEM((2, RET_CHUNK, RET_CHUNK), BF16)],
        compiler_params=_params(2),
        name="ret",
    )(log_gammas, rq, rkt, rv, rg, crkt, crv, gn_w)


N_DR = 2 * NA_KH - 1
N_DC = 2 * NA_KW - 1
PAIR_ROWS = 2 * GRID_W
NA_GROUP = 4


def _na_build_bias(rpb_ref, bias_ref, pair):
    qi = lax.broadcasted_iota(jnp.int32, (GRID_W, LANES), 0)
    li = lax.broadcasted_iota(jnp.int32, (GRID_W, LANES), 1)
    key_col = li & (GRID_W - 1)
    col_start = jnp.clip(qi - NA_KW // 2, 0, GRID_W - NA_KW)
    valid = (key_col >= col_start) & (key_col < col_start + NA_KW)
    code = jnp.where(valid, key_col - qi + (NA_KW - 1), -1)
    first_row = lax.broadcasted_iota(jnp.int32, (1, LANES), 1) < GRID_W

    def pattern(t, carry):
        for hd in range(2):
            head = 2 * pair + hd
            for blk in range(NA_KH // 2):
                base = (head * N_DR + (2 * blk + NA_KH - 1 - t)) * N_DC
                acc = jnp.full((GRID_W, LANES), NEG_INF, F32)
                for b in range(N_DC):
                    val = jnp.where(first_row, rpb_ref[base + b], rpb_ref[base + N_DC + b]) * LOG2_E
                    acc = jnp.where(code == b, val, acc)
                bias_ref[t, hd * GRID_W:(hd + 1) * GRID_W, blk * LANES:(blk + 1) * LANES] = acc
        return carry

    lax.fori_loop(0, N_PATTERNS, pattern, 0)


def _na_kernel(rpb_ref, q_ref, kt_ref, v_ref, kct_ref, vc_ref, o_ref,
               bias_ref, kt2_ref, q2_ref, s_ref):
    pair = pl.program_id(0)
    low = lax.broadcasted_iota(jnp.int32, (1, LANES), 1) < NA_DIM
    half = NA_KH // 2

    @pl.when(pl.program_id(1) == 0)
    def _bias():
        _na_build_bias(rpb_ref, bias_ref, pair)

    kt2_ref[0] = kt_ref[0]
    kt2_ref[1, :, 0:SEQ - LANES] = kt_ref[0, :, GRID_W:SEQ - GRID_W]
    kt2_ref[1, :, SEQ - LANES:SEQ] = jnp.zeros((LANES, LANES), BF16)

    def fill_q2(r, carry):
        q = q_ref[0, pl.ds(pl.multiple_of(r * GRID_W, GRID_W), GRID_W), :]
        zero = jnp.zeros_like(q)
        base = pl.multiple_of(r * PAIR_ROWS, PAIR_ROWS)
        q2_ref[pl.ds(base, GRID_W), :] = jnp.where(low, q, zero)
        q2_ref[pl.ds(base + GRID_W, GRID_W), :] = jnp.where(low, zero, q)
        return carry

    lax.fori_loop(0, GRID_ROWS, fill_q2, 0, unroll=4)

    def band_start(r):
        return jnp.clip(r - half, 0, GRID_ROWS - NA_KH)

    def scores(group, slot):
        rows = group * (NA_GROUP * PAIR_ROWS)
        group_q = pl.ds(pl.multiple_of(rows, NA_GROUP * PAIR_ROWS), NA_GROUP * PAIR_ROWS)
        s_ref[slot, :, NA_BAND:] = jnp.dot(q2_ref[group_q, :], kct_ref[0], preferred_element_type=F32)
        for j in range(NA_GROUP):
            r = group * NA_GROUP + j
            start = band_start(r)
            pattern = jnp.where(r < half, r, jnp.where(r > GRID_ROWS - half, r - (GRID_ROWS - NA_KH), half))
            lanes = pl.ds(pl.multiple_of((start >> 1) * LANES, LANES), NA_BAND)
            kb = kt2_ref[start & 1, :, lanes]
            q2 = q2_ref[pl.ds(pl.multiple_of(r * PAIR_ROWS, PAIR_ROWS), PAIR_ROWS), :]
            s_ref[slot, j * PAIR_ROWS:(j + 1) * PAIR_ROWS, 0:NA_BAND] = (
                jnp.dot(q2, kb, preferred_element_type=F32) + bias_ref[pattern])

    n_blocks = (NA_BAND + CTX_LEN) // LANES

    def outputs(group, slot):
        for j in range(NA_GROUP):
            r = group * NA_GROUP + j
            rows = slice(j * PAIR_ROWS, (j + 1) * PAIR_ROWS)

            def block(b):
                return s_ref[slot, rows, b * LANES:(b + 1) * LANES]

            m = block(0)
            for b in range(1, n_blocks):
                m = jnp.maximum(m, block(b))
            m = jnp.max(m, axis=-1, keepdims=True)
            p = [jnp.exp2(block(b) - m) for b in range(n_blocks)]
            den = p[0]
            for pb in p[1:]:
                den = den + pb
            den = jnp.sum(den, axis=-1, keepdims=True)
            p = jnp.concatenate([pb.astype(BF16) for pb in p], axis=1)
            vb = v_ref[0, pl.ds(pl.multiple_of(band_start(r) * GRID_W, GRID_W), NA_BAND), :]
            o = (jnp.dot(p[:, 0:NA_BAND], vb, preferred_element_type=F32)
                 + jnp.dot(p[:, NA_BAND:], vc_ref[0], preferred_element_type=F32)) * (1.0 / den)
            o_ref[0, pl.ds(pl.multiple_of(r * GRID_W, GRID_W), GRID_W), :] = (
                jnp.where(low, o[:GRID_W], o[GRID_W:]).astype(BF16))

    n_groups = GRID_ROWS // NA_GROUP
    scores(0, 0)
    for g in range(n_groups):
        if g + 1 < n_groups:
            scores(g + 1, (g + 1) % 2)
        outputs(g, g % 2)


def _neighbourhood_attention(rpb, nq, nkt, nv, cnkt, cnv):
    batch = nq.shape[0]
    n_q2 = GRID_ROWS * PAIR_ROWS
    lat = pl.BlockSpec((1, SEQ, LANES), lambda p, b: (b, 0, p))
    cx = pl.BlockSpec((1, CTX_LEN, LANES), lambda p, b: (b, 0, p))
    return pl.pallas_call(
        _na_kernel,
        grid=(NA_HEADS // 2, batch),
        in_specs=[pl.BlockSpec(memory_space=pltpu.SMEM), lat,
                  pl.BlockSpec((1, LANES, SEQ), lambda p, b: (b, p, 0)), lat,
                  pl.BlockSpec((1, LANES, CTX_LEN), lambda p, b: (b, p, 0)), cx],
        out_specs=lat,
        out_shape=jax.ShapeDtypeStruct((batch, SEQ, NA_WIDTH), BF16),
        scratch_shapes=[pltpu.VMEM((N_PATTERNS, PAIR_ROWS, NA_BAND), F32),
                        pltpu.VMEM((2, LANES, SEQ), BF16),
                        pltpu.VMEM((n_q2, LANES), BF16),
                        pltpu.VMEM((2, NA_GROUP * PAIR_ROWS, NA_BAND + CTX_LEN), F32)],
        compiler_params=_params(2),
        name="na",
    )(rpb, nq, nkt, nv, cnkt, cnv)


def _out_kernel(x_ref, ret_ref, na_ref, mod_ref, gpm_ref, gpre_ref, gpost_ref, wo_ref, w1_ref, w2_ref, o_ref):
    gt1, sh2, sc2, gt2 = (mod_ref[0, i:i + 1, :] for i in (2, 3, 4, 5))
    subs = [slice(i * SUB_TILE, (i + 1) * SUB_TILE) for i in range(OUT_TILE // SUB_TILE)]
    mix = [jnp.dot(ret_ref[0, rows, :], wo_ref[0:RET_WIDTH, :], preferred_element_type=F32)
           + jnp.dot(na_ref[0, rows, :], wo_ref[RET_WIDTH:, :], preferred_element_type=F32) for rows in subs]
    x1 = [x_ref[0, rows, :] + gt1 * _rms(m, gpm_ref[...]) for rows, m in zip(subs, mix)]
    h2 = [_prenorm(x, gpre_ref[...], sh2, sc2) for x in x1]
    mlp = []
    for h in h2:
        acc = jnp.zeros((SUB_TILE, D_MODEL), F32)
        for ci in range(D_FF // FF_CHUNK):
            cols = slice(ci * FF_CHUNK, (ci + 1) * FF_CHUNK)
            a = jnp.maximum(jnp.dot(h, w1_ref[:, cols], preferred_element_type=F32), 0.0)
            acc = acc + jnp.dot((a * a).astype(BF16), w2_ref[cols, :], preferred_element_type=F32)
        mlp.append(acc)
    for rows, x, y in zip(subs, x1, mlp):
        o_ref[0, rows, :] = x + gt2 * _rms(y, gpost_ref[...])


def _out_proj_mlp(x, ret_lat, na_lat, mods, g_post_mix, g_pre_mlp, g_post_mlp, w_out, w_mlp1, w_mlp2):
    batch = x.shape[0]
    tok = lambda b, i: (b, i, 0)
    gain = pl.BlockSpec((1, D_MODEL), lambda b, i: (0, 0))
    return pl.pallas_call(
        _out_kernel,
        grid=(batch, SEQ // OUT_TILE),
        in_specs=[pl.BlockSpec((1, OUT_TILE, D_MODEL), tok),
                  pl.BlockSpec((1, OUT_TILE, RET_WIDTH), tok),
                  pl.BlockSpec((1, OUT_TILE, NA_WIDTH), tok),
                  pl.BlockSpec((1, N_MOD, D_MODEL), lambda b, i: (b, 0, 0)),
                  gain, gain, gain,
                  _resident((RET_WIDTH + NA_WIDTH, D_MODEL)),
                  _resident((D_MODEL, D_FF)),
                  _resident((D_FF, D_MODEL))],
        out_specs=pl.BlockSpec((1, OUT_TILE, D_MODEL), tok),
        out_shape=jax.ShapeDtypeStruct(x.shape, F32),
        compiler_params=_params(2),
        name="out_mlp",
    )(x, ret_lat, na_lat, mods, g_post_mix, g_pre_mlp, g_post_mlp, w_out, w_mlp1, w_mlp2)


def kernel(x, c, ctx, c_ctx, w_ada, b_ada, g_pre_mix, g_post_mix, g_pre_mlp, g_post_mlp,
           w_in, ret_decay, ret_gn, na_rpb, w_out, w_mlp1, w_mlp2):
    assert w_in.shape[0] == 1, "single-layer block: the context stream is never updated"
    batch = x.shape[0]
    pad = (-(batch + 1)) % 8
    cc = jnp.concatenate([c, c_ctx[None, :], jnp.zeros((pad, D_MODEL), F32)], axis=0)
    mods = _modulations(cc, w_ada[0], b_ada)
    mods_lat = mods[:batch].reshape(batch, N_MOD, D_MODEL)
    mods_ctx = mods[batch:batch + 1].reshape(1, N_MOD, D_MODEL)

    w_in_b = w_in[0].astype(BF16)
    rq, rk, rv, rg, nq, nkt, nv = _in_proj_lat(x, mods_lat, g_pre_mix, w_in_b)
    crk, crv, cnkt, cnv = _in_proj_ctx(ctx, mods_ctx, g_pre_mix, w_in_b)

    log_gammas = jax.nn.log_sigmoid(ret_decay[0].astype(F32))
    ret_lat = _retention(log_gammas, rq, rk, rv, rg, crk, crv, ret_gn)
    na_lat = _neighbourhood_attention(na_rpb[0].astype(F32).reshape(-1), nq, nkt, nv, cnkt, cnv)

    return _out_proj_mlp(x, ret_lat, na_lat, mods_lat, g_post_mix, g_pre_mlp, g_post_mlp,
                         w_out[0].astype(BF16), w_mlp1[0].astype(BF16), w_mlp2[0].astype(BF16))
```

```python
import numpy as np
import jax
import jax.numpy as jnp
from jax import lax
from jax.experimental import pallas as pl
from jax.experimental.pallas import tpu as pltpu

D_MODEL = 1024
SEQ = 2048
CTX_LEN = 256
GRID_W = 64
GRID_ROWS = SEQ // GRID_W
RET_HEADS = 4
RET_DIM = 128
RET_WIDTH = RET_HEADS * RET_DIM
NA_HEADS = 8
NA_DIM = 64
NA_WIDTH = NA_HEADS * NA_DIM
NA_KH = 8
NA_KW = 16
N_GROUPS = 7
GROUP_W = 512
_K_GROUPS = (1, 5)
D_FF = 4 * D_MODEL
ROPE_BASE = 10000.0
NORM_EPS = 1e-6
N_MOD = 6
NEG_INF = -1e30
LOG2_E = 1.4426950408889634

LANES = 128
ROW_TILE = 1024
CTX_BATCHES = 2
OUT_TILE = 512
SUB_TILE = 256
FF_CHUNK = 1024
RET_CHUNK = 256
NA_BAND = NA_KH * GRID_W
N_PATTERNS = 8
VMEM_LIMIT = 56 * 1024 * 1024

F32 = jnp.float32
BF16 = jnp.bfloat16
_NT = (((1,), (1,)), ((), ()))
_TN = (((0,), (0,)), ((), ()))


def _silu(x):
    return x * (1.0 / (1.0 + jnp.exp(-x)))


def _rms(x, g):
    return x * lax.rsqrt(jnp.mean(x * x, axis=-1, keepdims=True) + NORM_EPS) * g


def _params(n_axes):
    return pltpu.CompilerParams(dimension_semantics=("arbitrary",) * n_axes,
                                vmem_limit_bytes=VMEM_LIMIT)


def _resident(shape):
    nd = len(shape)
    return pl.BlockSpec(shape, lambda *_: (0,) * nd, pipeline_mode=pl.Buffered(1))


def _mod_kernel(c_ref, w_ref, b_ref, o_ref):
    a = _silu(c_ref[...]).astype(BF16)
    o_ref[...] = jnp.dot(a, w_ref[...].astype(BF16), preferred_element_type=F32) + b_ref[...]


def _modulations(cc, w_ada, b_ada):
    rows = cc.shape[0]
    return pl.pallas_call(
        _mod_kernel,
        grid=(N_MOD,),
        in_specs=[pl.BlockSpec((rows, D_MODEL), lambda j: (0, 0)),
                  pl.BlockSpec((D_MODEL, D_MODEL), lambda j: (0, j)),
                  pl.BlockSpec((1, D_MODEL), lambda j: (0, j))],
        out_specs=pl.BlockSpec((rows, D_MODEL), lambda j: (0, j)),
        out_shape=jax.ShapeDtypeStruct((rows, N_MOD * D_MODEL), F32),
        compiler_params=_params(1),
        name="mod",
    )(cc, w_ada, b_ada)


def _prenorm(x, g, shift, scale):
    return (_rms(x, g) * (1.0 + scale) + shift).astype(BF16)


def _rope(blk, cos, sin_signed, first_half):
    partner = jnp.where(first_half, pltpu.roll(blk, LANES - 32, 1), pltpu.roll(blk, 32, 1))
    return blk * cos + partner * sin_signed


def _in_lat_kernel(x_ref, mod_ref, g_ref, w_ref, cq_ref, sq_ref, ck_ref, sk_ref, *out_refs):
    h = _prenorm(x_ref[0], g_ref[...], mod_ref[0, 0:1, :], mod_ref[0, 1:2, :])
    lane = lax.broadcasted_iota(jnp.int32, (1, LANES), 1)
    first_half = (lane & 32) == 0
    for gi, o_ref in enumerate(out_refs):
        acc = jnp.dot(h, w_ref[:, gi * GROUP_W:(gi + 1) * GROUP_W], preferred_element_type=F32)
        if gi == 0:
            for hh in range(RET_HEADS):
                blk = _rope(acc[:, hh * LANES:(hh + 1) * LANES], cq_ref[...], sq_ref[...], first_half)
                o_ref[0, :, hh * LANES:(hh + 1) * LANES] = blk.astype(BF16)
        elif gi == 1:
            for hh in range(RET_HEADS):
                blk = _rope(acc[:, hh * LANES:(hh + 1) * LANES], ck_ref[...], sk_ref[...], first_half)
                o_ref[0, hh * LANES:(hh + 1) * LANES, :] = blk.T.astype(BF16)
        elif gi == 4:
            o_ref[0] = (acc * (NA_DIM ** -0.5 * LOG2_E)).astype(BF16)
        elif gi in _K_GROUPS:
            o_ref[0] = acc.T.astype(BF16)
        else:
            o_ref[0] = acc.astype(BF16)


_CTX_GROUPS = (1, 2, 5, 6)


def _in_ctx_kernel(x_ref, mod_ref, g_ref, w_ref, *out_refs):
    x = jnp.concatenate([x_ref[i] for i in range(CTX_BATCHES)], axis=0)
    h = _prenorm(x, g_ref[...], mod_ref[0, 0:1, :], mod_ref[0, 1:2, :])
    for gi, o_ref in zip(_CTX_GROUPS, out_refs):
        acc = jnp.dot(h, w_ref[:, gi * GROUP_W:(gi + 1) * GROUP_W], preferred_element_type=F32)
        acc = (acc.T if gi in _K_GROUPS else acc).astype(BF16)
        for i in range(CTX_BATCHES):
            tokens = slice(i * CTX_LEN, (i + 1) * CTX_LEN)
            o_ref[i] = acc[:, tokens] if gi in _K_GROUPS else acc[tokens, :]


def _rope_tables():
    tok = np.arange(SEQ)
    n_freq = RET_DIM // 4
    inv = ROPE_BASE ** (-np.arange(n_freq, dtype=np.float64) / n_freq)
    ang_r = (tok // GRID_W)[:, None] * inv[None, :]
    ang_c = (tok % GRID_W)[:, None] * inv[None, :]
    cos = np.concatenate([np.cos(ang_r), np.cos(ang_r), np.cos(ang_c), np.cos(ang_c)], axis=-1)
    sin = np.concatenate([-np.sin(ang_r), np.sin(ang_r), -np.sin(ang_c), np.sin(ang_c)], axis=-1)
    qs = RET_DIM ** -0.5
    return [jnp.asarray(t, F32) for t in (cos * qs, sin * qs, cos, sin)]


def _in_proj_lat(x, mods, g_pre, w_in):
    batch = x.shape[0]
    tok = lambda b, i: (b, i, 0)
    tab = pl.BlockSpec((ROW_TILE, LANES), lambda b, i: (i, 0))
    out_shapes = [jax.ShapeDtypeStruct((batch, SEQ, GROUP_W), BF16)] * N_GROUPS
    out_specs = [pl.BlockSpec((1, ROW_TILE, GROUP_W), tok)] * N_GROUPS
    for gi in _K_GROUPS:
        out_shapes[gi] = jax.ShapeDtypeStruct((batch, GROUP_W, SEQ), BF16)
        out_specs[gi] = pl.BlockSpec((1, GROUP_W, ROW_TILE), lambda b, i: (b, 0, i))
    return pl.pallas_call(
        _in_lat_kernel,
        grid=(batch, SEQ // ROW_TILE),
        in_specs=[pl.BlockSpec((1, ROW_TILE, D_MODEL), tok),
                  pl.BlockSpec((1, N_MOD, D_MODEL), lambda b, i: (b, 0, 0)),
                  pl.BlockSpec((1, D_MODEL), lambda b, i: (0, 0)),
                  _resident((D_MODEL, N_GROUPS * GROUP_W)),
                  tab, tab, tab, tab],
        out_specs=out_specs,
        out_shape=out_shapes,
        compiler_params=_params(2),
        name="in_lat",
    )(x, mods, g_pre, w_in, *_rope_tables())


def _in_proj_ctx(ctx, mods_ctx, g_pre, w_in):
    batch = ctx.shape[0]
    tok = lambda b: (b, 0, 0)
    out_shapes = [jax.ShapeDtypeStruct((batch, GROUP_W, CTX_LEN) if gi in _K_GROUPS else (batch, CTX_LEN, GROUP_W),
                                       BF16) for gi in _CTX_GROUPS]
    out_specs = [pl.BlockSpec((CTX_BATCHES, GROUP_W, CTX_LEN) if gi in _K_GROUPS else (CTX_BATCHES, CTX_LEN, GROUP_W),
                              tok) for gi in _CTX_GROUPS]
    return pl.pallas_call(
        _in_ctx_kernel,
        grid=(batch // CTX_BATCHES,),
        in_specs=[pl.BlockSpec((CTX_BATCHES, CTX_LEN, D_MODEL), tok),
                  pl.BlockSpec((1, N_MOD, D_MODEL), lambda b: (0, 0, 0)),
                  pl.BlockSpec((1, D_MODEL), lambda b: (0, 0)),
                  _resident((D_MODEL, N_GROUPS * GROUP_W))],
        out_specs=out_specs,
        out_shape=out_shapes,
        compiler_params=_params(1),
        name="in_ctx",
    )(ctx, mods_ctx, g_pre, w_in)


def _ret_kernel(lg_ref, q_ref, kt_ref, v_ref, g_ref, kct_ref, vc_ref, gn_ref, o_ref,
                dec_ref, wq_ref, wk_ref, upd_ref, st_ref, sp_ref):
    head = pl.program_id(0)
    lgf = lg_ref[0, head]
    lgb = lg_ref[1, head]
    c = RET_CHUNK
    n_chunks = SEQ // c

    @pl.when(pl.program_id(1) == 0)
    def _tables():
        ii = lax.broadcasted_iota(jnp.int32, (c, c), 0)
        jj = lax.broadcasted_iota(jnp.int32, (c, c), 1)
        diff = (ii - jj).astype(F32)
        dec_ref[...] = jnp.exp(jnp.where(diff >= 0, lgf * diff, -lgb * diff))
        ri = lax.broadcasted_iota(jnp.int32, (c, LANES), 0).astype(F32)
        wq_ref[0] = jnp.exp(lgf * (ri + 1.0))
        wq_ref[1] = jnp.exp(lgb * (c - ri))
        ti = lax.broadcasted_iota(jnp.int32, (1, c), 1).astype(F32)
        wk_ref[0] = jnp.exp(lgf * (c - 1.0 - ti))
        wk_ref[1] = jnp.exp(lgb * ti)

    ones = jnp.ones((1, LANES), F32)
    chunk_f = jnp.exp(ones * (lgf * c))
    chunk_b = jnp.exp(ones * (lgb * c))

    def update(kt, v):
        ktf = kt.astype(F32)
        kw = jnp.concatenate([(ktf * wk_ref[0]).astype(BF16), (ktf * wk_ref[1]).astype(BF16)], axis=0)
        return jnp.dot(kw, v, preferred_element_type=F32)

    def chunk(n):
        return slice(n * c, (n + 1) * c)

    for n in range(n_chunks):
        upd_ref[n] = update(kt_ref[0, :, chunk(n)], v_ref[0, chunk(n), :])

    ctx_upd = update(kct_ref[0], vc_ref[0])
    state = ctx_upd[:RET_DIM]
    for n in range(n_chunks):
        st_ref[n, 0:RET_DIM, :] = state.astype(BF16)
        state = chunk_f * state + upd_ref[n, 0:RET_DIM, :]
    state = ctx_upd[RET_DIM:]
    for n in reversed(range(n_chunks)):
        st_ref[n, RET_DIM:, :] = state.astype(BF16)
        state = chunk_b * state + upd_ref[n, RET_DIM:, :]

    def scores(n, slot):
        s = jnp.dot(q_ref[0, chunk(n), :], kt_ref[0, :, chunk(n)], preferred_element_type=F32)
        sp_ref[slot] = (s * dec_ref[...]).astype(BF16)

    def outputs(n, slot):
        rows = chunk(n)
        qf = q_ref[0, rows, :].astype(F32)
        qw = jnp.concatenate([(qf * wq_ref[0]).astype(BF16), (qf * wq_ref[1]).astype(BF16)], axis=1)
        o = (jnp.dot(sp_ref[slot], v_ref[0, rows, :], preferred_element_type=F32)
             + jnp.dot(qw, st_ref[n], preferred_element_type=F32))
        d = o - jnp.mean(o, axis=-1, keepdims=True)
        y = d * lax.rsqrt(jnp.mean(d * d, axis=-1, keepdims=True) + NORM_EPS) * gn_ref[...]
        o_ref[0, rows, :] = (y * _silu(g_ref[0, rows, :].astype(F32))).astype(BF16)

    scores(0, 0)
    for n in range(n_chunks):
        if n + 1 < n_chunks:
            scores(n + 1, (n + 1) % 2)
        outputs(n, n % 2)


def _retention(log_gammas, rq, rkt, rv, rg, crkt, crv, gn_w):
    assert CTX_LEN == RET_CHUNK
    batch = rq.shape[0]
    n_chunks = SEQ // RET_CHUNK
    lat = pl.BlockSpec((1, SEQ, LANES), lambda h, b: (b, 0, h))
    cx = pl.BlockSpec((1, CTX_LEN, LANES), lambda h, b: (b, 0, h))
    return pl.pallas_call(
        _ret_kernel,
        grid=(RET_HEADS, batch),
        in_specs=[pl.BlockSpec(memory_space=pltpu.SMEM), lat,
                  pl.BlockSpec((1, LANES, SEQ), lambda h, b: (b, h, 0)), lat, lat,
                  pl.BlockSpec((1, LANES, CTX_LEN), lambda h, b: (b, h, 0)), cx,
                  pl.BlockSpec((1, LANES), lambda h, b: (0, h))],
        out_specs=lat,
        out_shape=jax.ShapeDtypeStruct((batch, SEQ, RET_WIDTH), BF16),
        scratch_shapes=[pltpu.VMEM((RET_CHUNK, RET_CHUNK), F32),
                        pltpu.VMEM((2, RET_CHUNK, LANES), F32),
                        pltpu.VMEM((2, 1, RET_CHUNK), F32),
                        pltpu.VMEM((n_chunks, 2 * RET_DIM, RET_DIM), F32),
                        pltpu.VMEM((n_chunks, 2 * RET_DIM, RET_DIM), BF16),
                        pltpu.VMEM((2, RET_CHUNK, RET_CHUNK), BF16)],
        compiler_params=_params(2),
        name="ret",
    )(log_gammas, rq, rkt, rv, rg, crkt, crv, gn_w)


N_DR = 2 * NA_KH - 1
N_DC = 2 * NA_KW - 1
PAIR_ROWS = 2 * GRID_W
NA_GROUP = 4


def _na_build_bias(rpb_ref, bias_ref, pair):
    qi = lax.broadcasted_iota(jnp.int32, (GRID_W, LANES), 0)
    li = lax.broadcasted_iota(jnp.int32, (GRID_W, LANES), 1)
    key_col = li & (GRID_W - 1)
    col_start = jnp.clip(qi - NA_KW // 2, 0, GRID_W - NA_KW)
    valid = (key_col >= col_start) & (key_col < col_start + NA_KW)
    code = jnp.where(valid, key_col - qi + (NA_KW - 1), -1)
    first_row = lax.broadcasted_iota(jnp.int32, (1, LANES), 1) < GRID_W

    def pattern(t, carry):
        for hd in range(2):
            head = 2 * pair + hd
            for blk in range(NA_KH // 2):
                base = (head * N_DR + (2 * blk + NA_KH - 1 - t)) * N_DC
                acc = jnp.full((GRID_W, LANES), NEG_INF, F32)
                for b in range(N_DC):
                    val = jnp.where(first_row, rpb_ref[base + b], rpb_ref[base + N_DC + b]) * LOG2_E
                    acc = jnp.where(code == b, val, acc)
                bias_ref[t, hd * GRID_W:(hd + 1) * GRID_W, blk * LANES:(blk + 1) * LANES] = acc
        return carry

    lax.fori_loop(0, N_PATTERNS, pattern, 0)


def _na_kernel(rpb_ref, q_ref, kt_ref, v_ref, kct_ref, vc_ref, o_ref,
               bias_ref, kt2_ref, q2_ref, s_ref):
    pair = pl.program_id(0)
    low = lax.broadcasted_iota(jnp.int32, (1, LANES), 1) < NA_DIM
    half = NA_KH // 2

    @pl.when(pl.program_id(1) == 0)
    def _bias():
        _na_build_bias(rpb_ref, bias_ref, pair)

    kt2_ref[0] = kt_ref[0]
    kt2_ref[1, :, 0:SEQ - LANES] = kt_ref[0, :, GRID_W:SEQ - GRID_W]
    kt2_ref[1, :, SEQ - LANES:SEQ] = jnp.zeros((LANES, LANES), BF16)

    def fill_q2(r, carry):
        q = q_ref[0, pl.ds(pl.multiple_of(r * GRID_W, GRID_W), GRID_W), :]
        zero = jnp.zeros_like(q)
        base = pl.multiple_of(r * PAIR_ROWS, PAIR_ROWS)
        q2_ref[pl.ds(base, GRID_W), :] = jnp.where(low, q, zero)
        q2_ref[pl.ds(base + GRID_W, GRID_W), :] = jnp.where(low, zero, q)
        return carry

    lax.fori_loop(0, GRID_ROWS, fill_q2, 0, unroll=4)

    def band_start(r):
        return jnp.clip(r - half, 0, GRID_ROWS - NA_KH)

    def scores(group, slot):
        rows = group * (NA_GROUP * PAIR_ROWS)
        group_q = pl.ds(pl.multiple_of(rows, NA_GROUP * PAIR_ROWS), NA_GROUP * PAIR_ROWS)
        s_ref[slot, :, NA_BAND:] = jnp.dot(q2_ref[group_q, :], kct_ref[0], preferred_element_type=F32)
        for j in range(NA_GROUP):
            r = group * NA_GROUP + j
            start = band_start(r)
            pattern = jnp.where(r < half, r, jnp.where(r > GRID_ROWS - half, r - (GRID_ROWS - NA_KH), half))
            lanes = pl.ds(pl.multiple_of((start >> 1) * LANES, LANES), NA_BAND)
            kb = kt2_ref[start & 1, :, lanes]
            q2 = q2_ref[pl.ds(pl.multiple_of(r * PAIR_ROWS, PAIR_ROWS), PAIR_ROWS), :]
            s_ref[slot, j * PAIR_ROWS:(j + 1) * PAIR_ROWS, 0:NA_BAND] = (
                jnp.dot(q2, kb, preferred_element_type=F32) + bias_ref[pattern])

    n_blocks = (NA_BAND + CTX_LEN) // LANES

    def outputs(group, slot):
        probs = []
        for j in range(NA_GROUP):
            rows = slice(j * PAIR_ROWS, (j + 1) * PAIR_ROWS)

            def block(b):
                return s_ref[slot, rows, b * LANES:(b + 1) * LANES]

            m = block(0)
            for b in range(1, n_blocks):
                m = jnp.maximum(m, block(b))
            m = jnp.max(m, axis=-1, keepdims=True)
            p = [jnp.exp2(block(b) - m) for b in range(n_blocks)]
            den = p[0]
            for pb in p[1:]:
                den = den + pb
            den = jnp.sum(den, axis=-1, keepdims=True)
            probs.append((jnp.concatenate([pb.astype(BF16) for pb in p], axis=1), den))
        ctx = jnp.dot(jnp.concatenate([p[:, NA_BAND:] for p, _ in probs], axis=0), vc_ref[0],
                      preferred_element_type=F32)
        for j, (p, den) in enumerate(probs):
            r = group * NA_GROUP + j
            vb = v_ref[0, pl.ds(pl.multiple_of(band_start(r) * GRID_W, GRID_W), NA_BAND), :]
            o = (jnp.dot(p[:, 0:NA_BAND], vb, preferred_element_type=F32)
                 + ctx[j * PAIR_ROWS:(j + 1) * PAIR_ROWS]) * (1.0 / den)
            o_ref[0, pl.ds(pl.multiple_of(r * GRID_W, GRID_W), GRID_W), :] = (
                jnp.where(low, o[:GRID_W], o[GRID_W:]).astype(BF16))

    n_groups = GRID_ROWS // NA_GROUP
    scores(0, 0)
    for g in range(n_groups):
        if g + 1 < n_groups:
            scores(g + 1, (g + 1) % 2)
        outputs(g, g % 2)


def _neighbourhood_attention(rpb, nq, nkt, nv, cnkt, cnv):
    batch = nq.shape[0]
    n_q2 = GRID_ROWS * PAIR_ROWS
    lat = pl.BlockSpec((1, SEQ, LANES), lambda p, b: (b, 0, p))
    cx = pl.BlockSpec((1, CTX_LEN, LANES), lambda p, b: (b, 0, p))
    return pl.pallas_call(
        _na_kernel,
        grid=(NA_HEADS // 2, batch),
        in_specs=[pl.BlockSpec(memory_space=pltpu.SMEM), lat,
                  pl.BlockSpec((1, LANES, SEQ), lambda p, b: (b, p, 0)), lat,
                  pl.BlockSpec((1, LANES, CTX_LEN), lambda p, b: (b, p, 0)), cx],
        out_specs=lat,
        out_shape=jax.ShapeDtypeStruct((batch, SEQ, NA_WIDTH), BF16),
        scratch_shapes=[pltpu.VMEM((N_PATTERNS, PAIR_ROWS, NA_BAND), F32),
                        pltpu.VMEM((2, LANES, SEQ), BF16),
                        pltpu.VMEM((n_q2, LANES), BF16),
                        pltpu.VMEM((2, NA_GROUP * PAIR_ROWS, NA_BAND + CTX_LEN), F32)],
        compiler_params=_params(2),
        name="na",
    )(rpb, nq, nkt, nv, cnkt, cnv)


def _out_kernel(x_ref, ret_ref, na_ref, mod_ref, gpm_ref, gpre_ref, gpost_ref, wo_ref, w1_ref, w2_ref, o_ref):
    gt1, sh2, sc2, gt2 = (mod_ref[0, i:i + 1, :] for i in (2, 3, 4, 5))
    subs = [slice(i * SUB_TILE, (i + 1) * SUB_TILE) for i in range(OUT_TILE // SUB_TILE)]
    mix = [jnp.dot(ret_ref[0, rows, :], wo_ref[0:RET_WIDTH, :], preferred_element_type=F32)
           + jnp.dot(na_ref[0, rows, :], wo_ref[RET_WIDTH:, :], preferred_element_type=F32) for rows in subs]
    x1 = [x_ref[0, rows, :] + gt1 * _rms(m, gpm_ref[...]) for rows, m in zip(subs, mix)]
    h2 = [_prenorm(x, gpre_ref[...], sh2, sc2) for x in x1]
    mlp = []
    for h in h2:
        acc = jnp.zeros((SUB_TILE, D_MODEL), F32)
        for ci in range(D_FF // FF_CHUNK):
            cols = slice(ci * FF_CHUNK, (ci + 1) * FF_CHUNK)
            a = jnp.maximum(jnp.dot(h, w1_ref[:, cols], preferred_element_type=F32), 0.0)
            acc = acc + jnp.dot((a * a).astype(BF16), w2_ref[cols, :], preferred_element_type=F32)
        mlp.append(acc)
    for rows, x, y in zip(subs, x1, mlp):
        o_ref[0, rows, :] = x + gt2 * _rms(y, gpost_ref[...])


def _out_proj_mlp(x, ret_lat, na_lat, mods, g_post_mix, g_pre_mlp, g_post_mlp, w_out, w_mlp1, w_mlp2):
    batch = x.shape[0]
    tok = lambda b, i: (b, i, 0)
    gain = pl.BlockSpec((1, D_MODEL), lambda b, i: (0, 0))
    return pl.pallas_call(
        _out_kernel,
        grid=(batch, SEQ // OUT_TILE),
        in_specs=[pl.BlockSpec((1, OUT_TILE, D_MODEL), tok),
                  pl.BlockSpec((1, OUT_TILE, RET_WIDTH), tok),
                  pl.BlockSpec((1, OUT_TILE, NA_WIDTH), tok),
                  pl.BlockSpec((1, N_MOD, D_MODEL), lambda b, i: (b, 0, 0)),
                  gain, gain, gain,
                  _resident((RET_WIDTH + NA_WIDTH, D_MODEL)),
                  _resident((D_MODEL, D_FF)),
                  _resident((D_FF, D_MODEL))],
        out_specs=pl.BlockSpec((1, OUT_TILE, D_MODEL), tok),
        out_shape=jax.ShapeDtypeStruct(x.shape, F32),
        compiler_params=_params(2),
        name="out_mlp",
    )(x, ret_lat, na_lat, mods, g_post_mix, g_pre_mlp, g_post_mlp, w_out, w_mlp1, w_mlp2)


def kernel(x, c, ctx, c_ctx, w_ada, b_ada, g_pre_mix, g_post_mix, g_pre_mlp, g_post_mlp,
           w_in, ret_decay, ret_gn, na_rpb, w_out, w_mlp1, w_mlp2):
    assert w_in.shape[0] == 1, "single-layer block: the context stream is never updated"
    batch = x.shape[0]
    pad = (-(batch + 1)) % 8
    cc = jnp.concatenate([c, c_ctx[None, :], jnp.zeros((pad, D_MODEL), F32)], axis=0)
    mods = _modulations(cc, w_ada[0], b_ada)
    mods_lat = mods[:batch].reshape(batch, N_MOD, D_MODEL)
    mods_ctx = mods[batch:batch + 1].reshape(1, N_MOD, D_MODEL)

    w_in_b = w_in[0].astype(BF16)
    rq, rk, rv, rg, nq, nkt, nv = _in_proj_lat(x, mods_lat, g_pre_mix, w_in_b)
    crk, crv, cnkt, cnv = _in_proj_ctx(ctx, mods_ctx, g_pre_mix, w_in_b)

    log_gammas = jax.nn.log_sigmoid(ret_decay[0].astype(F32))
    ret_lat = _retention(log_gammas, rq, rk, rv, rg, crk, crv, ret_gn)
    na_lat = _neighbourhood_attention(na_rpb[0].astype(F32).reshape(-1), nq, nkt, nv, cnkt, cnv)

    return _out_proj_mlp(x, ret_lat, na_lat, mods_lat, g_post_mix, g_pre_mlp, g_post_mlp,
                         w_out[0].astype(BF16), w_mlp1[0].astype(BF16), w_mlp2[0].astype(BF16))
```

```python
import numpy as np
import jax
import jax.numpy as jnp
from jax import lax
from jax.experimental import pallas as pl
from jax.experimental.pallas import tpu as pltpu

D_MODEL = 1024
SEQ = 2048
CTX_LEN = 256
GRID_W = 64
GRID_ROWS = SEQ // GRID_W
RET_HEADS = 4
RET_DIM = 128
RET_WIDTH = RET_HEADS * RET_DIM
NA_HEADS = 8
NA_DIM = 64
NA_WIDTH = NA_HEADS * NA_DIM
NA_KH = 8
NA_KW = 16
N_GROUPS = 7
GROUP_W = 512
_K_GROUPS = (1, 5)
D_FF = 4 * D_MODEL
ROPE_BASE = 10000.0
NORM_EPS = 1e-6
N_MOD = 6
NEG_INF = -1e30
LOG2_E = 1.4426950408889634

LANES = 128
ROW_TILE = 1024
CTX_BATCHES = 2
OUT_TILE = 512
SUB_TILE = 256
FF_CHUNK = 1024
RET_CHUNK = 256
NA_BAND = NA_KH * GRID_W
N_PATTERNS = 8
VMEM_LIMIT = 56 * 1024 * 1024

F32 = jnp.float32
BF16 = jnp.bfloat16
_NT = (((1,), (1,)), ((), ()))
_TN = (((0,), (0,)), ((), ()))


def _silu(x):
    return x * (1.0 / (1.0 + jnp.exp(-x)))


def _rms(x, g):
    return x * lax.rsqrt(jnp.mean(x * x, axis=-1, keepdims=True) + NORM_EPS) * g


def _params(n_axes):
    return pltpu.CompilerParams(dimension_semantics=("arbitrary",) * n_axes,
                                vmem_limit_bytes=VMEM_LIMIT)


def _resident(shape):
    nd = len(shape)
    return pl.BlockSpec(shape, lambda *_: (0,) * nd, pipeline_mode=pl.Buffered(1))


def _mod_kernel(c_ref, w_ref, b_ref, o_ref):
    a = _silu(c_ref[...]).astype(BF16)
    o_ref[...] = jnp.dot(a, w_ref[...].astype(BF16), preferred_element_type=F32) + b_ref[...]


def _modulations(cc, w_ada, b_ada):
    rows = cc.shape[0]
    return pl.pallas_call(
        _mod_kernel,
        grid=(N_MOD,),
        in_specs=[pl.BlockSpec((rows, D_MODEL), lambda j: (0, 0)),
                  pl.BlockSpec((D_MODEL, D_MODEL), lambda j: (0, j)),
                  pl.BlockSpec((1, D_MODEL), lambda j: (0, j))],
        out_specs=pl.BlockSpec((rows, D_MODEL), lambda j: (0, j)),
        out_shape=jax.ShapeDtypeStruct((rows, N_MOD * D_MODEL), F32),
        compiler_params=_params(1),
        name="mod",
    )(cc, w_ada, b_ada)


def _prenorm(x, g, shift, scale):
    return (_rms(x, g) * (1.0 + scale) + shift).astype(BF16)


def _rope(blk, cos, sin_signed, first_half):
    partner = jnp.where(first_half, pltpu.roll(blk, LANES - 32, 1), pltpu.roll(blk, 32, 1))
    return blk * cos + partner * sin_signed


def _in_lat_kernel(x_ref, mod_ref, g_ref, w_ref, cq_ref, sq_ref, ck_ref, sk_ref, *out_refs):
    h = _prenorm(x_ref[0], g_ref[...], mod_ref[0, 0:1, :], mod_ref[0, 1:2, :])
    lane = lax.broadcasted_iota(jnp.int32, (1, LANES), 1)
    first_half = (lane & 32) == 0
    for gi, o_ref in enumerate(out_refs):
        acc = jnp.dot(h, w_ref[:, gi * GROUP_W:(gi + 1) * GROUP_W], preferred_element_type=F32)
        if gi == 0:
            for hh in range(RET_HEADS):
                blk = _rope(acc[:, hh * LANES:(hh + 1) * LANES], cq_ref[...], sq_ref[...], first_half)
                o_ref[0, :, hh * LANES:(hh + 1) * LANES] = blk.astype(BF16)
        elif gi == 1:
            for hh in range(RET_HEADS):
                blk = _rope(acc[:, hh * LANES:(hh + 1) * LANES], ck_ref[...], sk_ref[...], first_half)
                o_ref[0, hh * LANES:(hh + 1) * LANES, :] = blk.T.astype(BF16)
        elif gi == 4:
            o_ref[0] = (acc * (NA_DIM ** -0.5 * LOG2_E)).astype(BF16)
        elif gi in _K_GROUPS:
            o_ref[0] = acc.T.astype(BF16)
        else:
            o_ref[0] = acc.astype(BF16)


_CTX_GROUPS = (1, 2, 5, 6)


def _in_ctx_kernel(x_ref, mod_ref, g_ref, w_ref, *out_refs):
    x = jnp.concatenate([x_ref[i] for i in range(CTX_BATCHES)], axis=0)
    h = _prenorm(x, g_ref[...], mod_ref[0, 0:1, :], mod_ref[0, 1:2, :])
    for gi, o_ref in zip(_CTX_GROUPS, out_refs):
        acc = jnp.dot(h, w_ref[:, gi * GROUP_W:(gi + 1) * GROUP_W], preferred_element_type=F32)
        acc = (acc.T if gi in _K_GROUPS else acc).astype(BF16)
        for i in range(CTX_BATCHES):
            tokens = slice(i * CTX_LEN, (i + 1) * CTX_LEN)
            o_ref[i] = acc[:, tokens] if gi in _K_GROUPS else acc[tokens, :]


def _rope_tables():
    tok = np.arange(SEQ)
    n_freq = RET_DIM // 4
    inv = ROPE_BASE ** (-np.arange(n_freq, dtype=np.float64) / n_freq)
    ang_r = (tok // GRID_W)[:, None] * inv[None, :]
    ang_c = (tok % GRID_W)[:, None] * inv[None, :]
    cos = np.concatenate([np.cos(ang_r), np.cos(ang_r), np.cos(ang_c), np.cos(ang_c)], axis=-1)
    sin = np.concatenate([-np.sin(ang_r), np.sin(ang_r), -np.sin(ang_c), np.sin(ang_c)], axis=-1)
    qs = RET_DIM ** -0.5
    return [jnp.asarray(t, F32) for t in (cos * qs, sin * qs, cos, sin)]


def _in_proj_lat(x, mods, g_pre, w_in):
    batch = x.shape[0]
    tok = lambda b, i: (b, i, 0)
    tab = pl.BlockSpec((ROW_TILE, LANES), lambda b, i: (i, 0))
    out_shapes = [jax.ShapeDtypeStruct((batch, SEQ, GROUP_W), BF16)] * N_GROUPS
    out_specs = [pl.BlockSpec((1, ROW_TILE, GROUP_W), tok)] * N_GROUPS
    for gi in _K_GROUPS:
        out_shapes[gi] = jax.ShapeDtypeStruct((batch, GROUP_W, SEQ), BF16)
        out_specs[gi] = pl.BlockSpec((1, GROUP_W, ROW_TILE), lambda b, i: (b, 0, i))
    return pl.pallas_call(
        _in_lat_kernel,
        grid=(batch, SEQ // ROW_TILE),
        in_specs=[pl.BlockSpec((1, ROW_TILE, D_MODEL), tok),
                  pl.BlockSpec((1, N_MOD, D_MODEL), lambda b, i: (b, 0, 0)),
                  pl.BlockSpec((1, D_MODEL), lambda b, i: (0, 0)),
                  _resident((D_MODEL, N_GROUPS * GROUP_W)),
                  tab, tab, tab, tab],
        out_specs=out_specs,
        out_shape=out_shapes,
        compiler_params=_params(2),
        name="in_lat",
    )(x, mods, g_pre, w_in, *_rope_tables())


def _in_proj_ctx(ctx, mods_ctx, g_pre, w_in):
    batch = ctx.shape[0]
    tok = lambda b: (b, 0, 0)
    out_shapes = [jax.ShapeDtypeStruct((batch, GROUP_W, CTX_LEN) if gi in _K_GROUPS else (batch, CTX_LEN, GROUP_W),
                                       BF16) for gi in _CTX_GROUPS]
    out_specs = [pl.BlockSpec((CTX_BATCHES, GROUP_W, CTX_LEN) if gi in _K_GROUPS else (CTX_BATCHES, CTX_LEN, GROUP_W),
                              tok) for gi in _CTX_GROUPS]
    return pl.pallas_call(
        _in_ctx_kernel,
        grid=(batch // CTX_BATCHES,),
        in_specs=[pl.BlockSpec((CTX_BATCHES, CTX_LEN, D_MODEL), tok),
                  pl.BlockSpec((1, N_MOD, D_MODEL), lambda b: (0, 0, 0)),
                  pl.BlockSpec((1, D_MODEL), lambda b: (0, 0)),
                  _resident((D_MODEL, N_GROUPS * GROUP_W))],
        out_specs=out_specs,
        out_shape=out_shapes,
        compiler_params=_params(1),
        name="in_ctx",
    )(ctx, mods_ctx, g_pre, w_in)


def _ret_kernel(lg_ref, q_ref, kt_ref, v_ref, g_ref, kct_ref, vc_ref, gn_ref, o_ref,
                dec_ref, wq_ref, wk_ref, upd_ref, st_ref, sp_ref):
    head = pl.program_id(0)
    lgf = lg_ref[0, head]
    lgb = lg_ref[1, head]
    c = RET_CHUNK
    n_chunks = SEQ // c

    @pl.when(pl.program_id(1) == 0)
    def _tables():
        ii = lax.broadcasted_iota(jnp.int32, (c, c), 0)
        jj = lax.broadcasted_iota(jnp.int32, (c, c), 1)
        diff = (ii - jj).astype(F32)
        dec_ref[...] = jnp.exp(jnp.where(diff >= 0, lgf * diff, -lgb * diff))
        ri = lax.broadcasted_iota(jnp.int32, (c, LANES), 0).astype(F32)
        wq_ref[0] = jnp.exp(lgf * (ri + 1.0))
        wq_ref[1] = jnp.exp(lgb * (c - ri))
        ti = lax.broadcasted_iota(jnp.int32, (1, c), 1).astype(F32)
        wk_ref[0] = jnp.exp(lgf * (c - 1.0 - ti))
        wk_ref[1] = jnp.exp(lgb * ti)

    ones = jnp.ones((1, LANES), F32)
    chunk_f = jnp.exp(ones * (lgf * c))
    chunk_b = jnp.exp(ones * (lgb * c))

    def update(kt, v):
        ktf = kt.astype(F32)
        kw = jnp.concatenate([(ktf * wk_ref[0]).astype(BF16), (ktf * wk_ref[1]).astype(BF16)], axis=0)
        return jnp.dot(kw, v, preferred_element_type=F32)

    def chunk(n):
        return slice(n * c, (n + 1) * c)

    for n in range(n_chunks):
        upd_ref[n] = update(kt_ref[0, :, chunk(n)], v_ref[0, chunk(n), :])

    ctx_upd = update(kct_ref[0], vc_ref[0])
    state = ctx_upd[:RET_DIM]
    for n in range(n_chunks):
        st_ref[n, 0:RET_DIM, :] = state.astype(BF16)
        state = chunk_f * state + upd_ref[n, 0:RET_DIM, :]
    state = ctx_upd[RET_DIM:]
    for n in reversed(range(n_chunks)):
        st_ref[n, RET_DIM:, :] = state.astype(BF16)
        state = chunk_b * state + upd_ref[n, RET_DIM:, :]

    def scores(n, slot):
        s = jnp.dot(q_ref[0, chunk(n), :], kt_ref[0, :, chunk(n)], preferred_element_type=F32)
        sp_ref[slot] = (s * dec_ref[...]).astype(BF16)

    def outputs(n, slot):
        rows = chunk(n)
        qf = q_ref[0, rows, :].astype(F32)
        qw = jnp.concatenate([(qf * wq_ref[0]).astype(BF16), (qf * wq_ref[1]).astype(BF16)], axis=1)
        o = (jnp.dot(sp_ref[slot], v_ref[0, rows, :], preferred_element_type=F32)
             + jnp.dot(qw, st_ref[n], preferred_element_type=F32))
        d = o - jnp.mean(o, axis=-1, keepdims=True)
        y = d * lax.rsqrt(jnp.mean(d * d, axis=-1, keepdims=True) + NORM_EPS) * gn_ref[...]
        o_ref[0, rows, :] = (y * _silu(g_ref[0, rows, :].astype(F32))).astype(BF16)

    scores(0, 0)
    for n in range(n_chunks):
        if n + 1 < n_chunks:
            scores(n + 1, (n + 1) % 2)
        outputs(n, n % 2)


def _retention(log_gammas, rq, rkt, rv, rg, crkt, crv, gn_w):
    assert CTX_LEN == RET_CHUNK
    batch = rq.shape[0]
    n_chunks = SEQ // RET_CHUNK
    lat = pl.BlockSpec((1, SEQ, LANES), lambda h, b: (b, 0, h))
    cx = pl.BlockSpec((1, CTX_LEN, LANES), lambda h, b: (b, 0, h))
    return pl.pallas_call(
        _ret_kernel,
        grid=(RET_HEADS, batch),
        in_specs=[pl.BlockSpec(memory_space=pltpu.SMEM), lat,
                  pl.BlockSpec((1, LANES, SEQ), lambda h, b: (b, h, 0)), lat, lat,
                  pl.BlockSpec((1, LANES, CTX_LEN), lambda h, b: (b, h, 0)), cx,
                  pl.BlockSpec((1, LANES), lambda h, b: (0, h))],
        out_specs=lat,
        out_shape=jax.ShapeDtypeStruct((batch, SEQ, RET_WIDTH), BF16),
        scratch_shapes=[pltpu.VMEM((RET_CHUNK, RET_CHUNK), F32),
                        pltpu.VMEM((2, RET_CHUNK, LANES), F32),
                        pltpu.VMEM((2, 1, RET_CHUNK), F32),
                        pltpu.VMEM((n_chunks, 2 * RET_DIM, RET_DIM), F32),
                        pltpu.VMEM((n_chunks, 2 * RET_DIM, RET_DIM), BF16),
                        pltpu.VMEM((2, RET_CHUNK, RET_CHUNK), BF16)],
        compiler_params=_params(2),
        name="ret",
    )(log_gammas, rq, rkt, rv, rg, crkt, crv, gn_w)


N_DR = 2 * NA_KH - 1
N_DC = 2 * NA_KW - 1
PAIR_ROWS = 2 * GRID_W
NA_GROUP = 2


def _na_build_bias(rpb_ref, bias_ref, pair):
    qi = lax.broadcasted_iota(jnp.int32, (GRID_W, LANES), 0)
    li = lax.broadcasted_iota(jnp.int32, (GRID_W, LANES), 1)
    key_col = li & (GRID_W - 1)
    col_start = jnp.clip(qi - NA_KW // 2, 0, GRID_W - NA_KW)
    valid = (key_col >= col_start) & (key_col < col_start + NA_KW)
    code = jnp.where(valid, key_col - qi + (NA_KW - 1), -1)
    first_row = lax.broadcasted_iota(jnp.int32, (1, LANES), 1) < GRID_W

    def pattern(t, carry):
        for hd in range(2):
            head = 2 * pair + hd
            for blk in range(NA_KH // 2):
                base = (head * N_DR + (2 * blk + NA_KH - 1 - t)) * N_DC
                acc = jnp.full((GRID_W, LANES), NEG_INF, F32)
                for b in range(N_DC):
                    val = jnp.where(first_row, rpb_ref[base + b], rpb_ref[base + N_DC + b]) * LOG2_E
                    acc = jnp.where(code == b, val, acc)
                bias_ref[t, hd * GRID_W:(hd + 1) * GRID_W, blk * LANES:(blk + 1) * LANES] = acc
        return carry

    lax.fori_loop(0, N_PATTERNS, pattern, 0)


def _na_kernel(rpb_ref, q_ref, kt_ref, v_ref, kct_ref, vc_ref, o_ref,
               bias_ref, kt2_ref, q2_ref, s_ref):
    pair = pl.program_id(0)
    low = lax.broadcasted_iota(jnp.int32, (1, LANES), 1) < NA_DIM
    half = NA_KH // 2

    @pl.when(pl.program_id(1) == 0)
    def _bias():
        _na_build_bias(rpb_ref, bias_ref, pair)

    kt2_ref[0] = kt_ref[0]
    kt2_ref[1, :, 0:SEQ - LANES] = kt_ref[0, :, GRID_W:SEQ - GRID_W]
    kt2_ref[1, :, SEQ - LANES:SEQ] = jnp.zeros((LANES, LANES), BF16)

    def fill_q2(r, carry):
        q = q_ref[0, pl.ds(pl.multiple_of(r * GRID_W, GRID_W), GRID_W), :]
        zero = jnp.zeros_like(q)
        base = pl.multiple_of(r * PAIR_ROWS, PAIR_ROWS)
        q2_ref[pl.ds(base, GRID_W), :] = jnp.where(low, q, zero)
        q2_ref[pl.ds(base + GRID_W, GRID_W), :] = jnp.where(low, zero, q)
        return carry

    lax.fori_loop(0, GRID_ROWS, fill_q2, 0, unroll=4)

    def band_start(r):
        return jnp.clip(r - half, 0, GRID_ROWS - NA_KH)

    def scores(group, slot):
        rows = group * (NA_GROUP * PAIR_ROWS)
        group_q = pl.ds(pl.multiple_of(rows, NA_GROUP * PAIR_ROWS), NA_GROUP * PAIR_ROWS)
        s_ref[slot, :, NA_BAND:] = jnp.dot(q2_ref[group_q, :], kct_ref[0], preferred_element_type=F32)
        for j in range(NA_GROUP):
            r = group * NA_GROUP + j
            start = band_start(r)
            pattern = jnp.where(r < half, r, jnp.where(r > GRID_ROWS - half, r - (GRID_ROWS - NA_KH), half))
            lanes = pl.ds(pl.multiple_of((start >> 1) * LANES, LANES), NA_BAND)
            kb = kt2_ref[start & 1, :, lanes]
            q2 = q2_ref[pl.ds(pl.multiple_of(r * PAIR_ROWS, PAIR_ROWS), PAIR_ROWS), :]
            s_ref[slot, j * PAIR_ROWS:(j + 1) * PAIR_ROWS, 0:NA_BAND] = (
                jnp.dot(q2, kb, preferred_element_type=F32) + bias_ref[pattern])

    n_blocks = (NA_BAND + CTX_LEN) // LANES

    def outputs(group, slot):
        probs = []
        for j in range(NA_GROUP):
            rows = slice(j * PAIR_ROWS, (j + 1) * PAIR_ROWS)

            def block(b):
                return s_ref[slot, rows, b * LANES:(b + 1) * LANES]

            m = block(0)
            for b in range(1, n_blocks):
                m = jnp.maximum(m, block(b))
            m = jnp.max(m, axis=-1, keepdims=True)
            p = [jnp.exp2(block(b) - m) for b in range(n_blocks)]
            den = p[0]
            for pb in p[1:]:
                den = den + pb
            den = jnp.sum(den, axis=-1, keepdims=True)
            probs.append((jnp.concatenate([pb.astype(BF16) for pb in p], axis=1), den))
        ctx = jnp.dot(jnp.concatenate([p[:, NA_BAND:] for p, _ in probs], axis=0), vc_ref[0],
                      preferred_element_type=F32)
        for j, (p, den) in enumerate(probs):
            r = group * NA_GROUP + j
            vb = v_ref[0, pl.ds(pl.multiple_of(band_start(r) * GRID_W, GRID_W), NA_BAND), :]
            o = (jnp.dot(p[:, 0:NA_BAND], vb, preferred_element_type=F32)
                 + ctx[j * PAIR_ROWS:(j + 1) * PAIR_ROWS]) * (1.0 / den)
            o_ref[0, pl.ds(pl.multiple_of(r * GRID_W, GRID_W), GRID_W), :] = (
                jnp.where(low, o[:GRID_W], o[GRID_W:]).astype(BF16))

    n_groups = GRID_ROWS // NA_GROUP
    scores(0, 0)
    for g in range(n_groups):
        if g + 1 < n_groups:
            scores(g + 1, (g + 1) % 2)
        outputs(g, g % 2)


def _neighbourhood_attention(rpb, nq, nkt, nv, cnkt, cnv):
    batch = nq.shape[0]
    n_q2 = GRID_ROWS * PAIR_ROWS
    lat = pl.BlockSpec((1, SEQ, LANES), lambda p, b: (b, 0, p))
    cx = pl.BlockSpec((1, CTX_LEN, LANES), lambda p, b: (b, 0, p))
    return pl.pallas_call(
        _na_kernel,
        grid=(NA_HEADS // 2, batch),
        in_specs=[pl.BlockSpec(memory_space=pltpu.SMEM), lat,
                  pl.BlockSpec((1, LANES, SEQ), lambda p, b: (b, p, 0)), lat,
                  pl.BlockSpec((1, LANES, CTX_LEN), lambda p, b: (b, p, 0)), cx],
        out_specs=lat,
        out_shape=jax.ShapeDtypeStruct((batch, SEQ, NA_WIDTH), BF16),
        scratch_shapes=[pltpu.VMEM((N_PATTERNS, PAIR_ROWS, NA_BAND), F32),
                        pltpu.VMEM((2, LANES, SEQ), BF16),
                        pltpu.VMEM((n_q2, LANES), BF16),
                        pltpu.VMEM((2, NA_GROUP * PAIR_ROWS, NA_BAND + CTX_LEN), F32)],
        compiler_params=_params(2),
        name="na",
    )(rpb, nq, nkt, nv, cnkt, cnv)


def _out_kernel(x_ref, ret_ref, na_ref, mod_ref, gpm_ref, gpre_ref, gpost_ref, wo_ref, w1_ref, w2_ref, o_ref):
    gt1, sh2, sc2, gt2 = (mod_ref[0, i:i + 1, :] for i in (2, 3, 4, 5))
    subs = [slice(i * SUB_TILE, (i + 1) * SUB_TILE) for i in range(OUT_TILE // SUB_TILE)]
    mix = [jnp.dot(ret_ref[0, rows, :], wo_ref[0:RET_WIDTH, :], preferred_element_type=F32)
           + jnp.dot(na_ref[0, rows, :], wo_ref[RET_WIDTH:, :], preferred_element_type=F32) for rows in subs]
    x1 = [x_ref[0, rows, :] + gt1 * _rms(m, gpm_ref[...]) for rows, m in zip(subs, mix)]
    h2 = [_prenorm(x, gpre_ref[...], sh2, sc2) for x in x1]
    mlp = []
    for h in h2:
        acc = jnp.zeros((SUB_TILE, D_MODEL), F32)
        for ci in range(D_FF // FF_CHUNK):
            cols = slice(ci * FF_CHUNK, (ci + 1) * FF_CHUNK)
            a = jnp.maximum(jnp.dot(h, w1_ref[:, cols], preferred_element_type=F32), 0.0)
            acc = acc + jnp.dot((a * a).astype(BF16), w2_ref[cols, :], preferred_element_type=F32)
        mlp.append(acc)
    for rows, x, y in zip(subs, x1, mlp):
        o_ref[0, rows, :] = x + gt2 * _rms(y, gpost_ref[...])


def _out_proj_mlp(x, ret_lat, na_lat, mods, g_post_mix, g_pre_mlp, g_post_mlp, w_out, w_mlp1, w_mlp2):
    batch = x.shape[0]
    tok = lambda b, i: (b, i, 0)
    gain = pl.BlockSpec((1, D_MODEL), lambda b, i: (0, 0))
    return pl.pallas_call(
        _out_kernel,
        grid=(batch, SEQ // OUT_TILE),
        in_specs=[pl.BlockSpec((1, OUT_TILE, D_MODEL), tok),
                  pl.BlockSpec((1, OUT_TILE, RET_WIDTH), tok),
                  pl.BlockSpec((1, OUT_TILE, NA_WIDTH), tok),
                  pl.BlockSpec((1, N_MOD, D_MODEL), lambda b, i: (b, 0, 0)),
                  gain, gain, gain,
                  _resident((RET_WIDTH + NA_WIDTH, D_MODEL)),
                  _resident((D_MODEL, D_FF)),
                  _resident((D_FF, D_MODEL))],
        out_specs=pl.BlockSpec((1, OUT_TILE, D_MODEL), tok),
        out_shape=jax.ShapeDtypeStruct(x.shape, F32),
        compiler_params=_params(2),
        name="out_mlp",
    )(x, ret_lat, na_lat, mods, g_post_mix, g_pre_mlp, g_post_mlp, w_out, w_mlp1, w_mlp2)


def kernel(x, c, ctx, c_ctx, w_ada, b_ada, g_pre_mix, g_post_mix, g_pre_mlp, g_post_mlp,
           w_in, ret_decay, ret_gn, na_rpb, w_out, w_mlp1, w_mlp2):
    assert w_in.shape[0] == 1, "single-layer block: the context stream is never updated"
    batch = x.shape[0]
    pad = (-(batch + 1)) % 8
    cc = jnp.concatenate([c, c_ctx[None, :], jnp.zeros((pad, D_MODEL), F32)], axis=0)
    mods = _modulations(cc, w_ada[0], b_ada)
    mods_lat = mods[:batch].reshape(batch, N_MOD, D_MODEL)
    mods_ctx = mods[batch:batch + 1].reshape(1, N_MOD, D_MODEL)

    w_in_b = w_in[0].astype(BF16)
    rq, rk, rv, rg, nq, nkt, nv = _in_proj_lat(x, mods_lat, g_pre_mix, w_in_b)
    crk, crv, cnkt, cnv = _in_proj_ctx(ctx, mods_ctx, g_pre_mix, w_in_b)

    log_gammas = jax.nn.log_sigmoid(ret_decay[0].astype(F32))
    ret_lat = _retention(log_gammas, rq, rk, rv, rg, crk, crv, ret_gn)
    na_lat = _neighbourhood_attention(na_rpb[0].astype(F32).reshape(-1), nq, nkt, nv, cnkt, cnv)

    return _out_proj_mlp(x, ret_lat, na_lat, mods_lat, g_post_mix, g_pre_mlp, g_post_mlp,
                         w_out[0].astype(BF16), w_mlp1[0].astype(BF16), w_mlp2[0].astype(BF16))
```

```python
import numpy as np
import jax
import jax.numpy as jnp
from jax import lax
from jax.experimental import pallas as pl
from jax.experimental.pallas import tpu as pltpu

D_MODEL = 1024
SEQ = 2048
CTX_LEN = 256
GRID_W = 64
GRID_ROWS = SEQ // GRID_W
RET_HEADS = 4
RET_DIM = 128
RET_WIDTH = RET_HEADS * RET_DIM
NA_HEADS = 8
NA_DIM = 64
NA_WIDTH = NA_HEADS * NA_DIM
NA_KH = 8
NA_KW = 16
N_GROUPS = 7
GROUP_W = 512
_K_GROUPS = (1, 5)
D_FF = 4 * D_MODEL
ROPE_BASE = 10000.0
NORM_EPS = 1e-6
N_MOD = 6
NEG_INF = -1e30
LOG2_E = 1.4426950408889634

LANES = 128
ROW_TILE = 1024
CTX_BATCHES = 2
OUT_TILE = 512
SUB_TILE = 256
FF_CHUNK = 1024
RET_CHUNK = 256
RET_BATCHES = 2
NA_BAND = NA_KH * GRID_W
N_PATTERNS = 8
VMEM_LIMIT = 56 * 1024 * 1024

F32 = jnp.float32
BF16 = jnp.bfloat16
_NT = (((1,), (1,)), ((), ()))
_TN = (((0,), (0,)), ((), ()))


def _silu(x):
    return x * (1.0 / (1.0 + jnp.exp(-x)))


def _rms(x, g):
    return x * lax.rsqrt(jnp.mean(x * x, axis=-1, keepdims=True) + NORM_EPS) * g


def _params(n_axes):
    return pltpu.CompilerParams(dimension_semantics=("arbitrary",) * n_axes,
                                vmem_limit_bytes=VMEM_LIMIT)


def _resident(shape):
    nd = len(shape)
    return pl.BlockSpec(shape, lambda *_: (0,) * nd, pipeline_mode=pl.Buffered(1))


def _mod_kernel(c_ref, w_ref, b_ref, o_ref):
    a = _silu(c_ref[...]).astype(BF16)
    o_ref[...] = jnp.dot(a, w_ref[...].astype(BF16), preferred_element_type=F32) + b_ref[...]


def _modulations(cc, w_ada, b_ada):
    rows = cc.shape[0]
    return pl.pallas_call(
        _mod_kernel,
        grid=(N_MOD,),
        in_specs=[pl.BlockSpec((rows, D_MODEL), lambda j: (0, 0)),
                  pl.BlockSpec((D_MODEL, D_MODEL), lambda j: (0, j)),
                  pl.BlockSpec((1, D_MODEL), lambda j: (0, j))],
        out_specs=pl.BlockSpec((rows, D_MODEL), lambda j: (0, j)),
        out_shape=jax.ShapeDtypeStruct((rows, N_MOD * D_MODEL), F32),
        compiler_params=_params(1),
        name="mod",
    )(cc, w_ada, b_ada)


def _prenorm(x, g, shift, scale):
    return (_rms(x, g) * (1.0 + scale) + shift).astype(BF16)


def _rope(blk, cos, sin_signed, first_half):
    partner = jnp.where(first_half, pltpu.roll(blk, LANES - 32, 1), pltpu.roll(blk, 32, 1))
    return blk * cos + partner * sin_signed


def _in_lat_kernel(x_ref, mod_ref, g_ref, w_ref, cq_ref, sq_ref, ck_ref, sk_ref, *out_refs):
    h = _prenorm(x_ref[0], g_ref[...], mod_ref[0, 0:1, :], mod_ref[0, 1:2, :])
    lane = lax.broadcasted_iota(jnp.int32, (1, LANES), 1)
    first_half = (lane & 32) == 0
    for gi, o_ref in enumerate(out_refs):
        w = w_ref[:, gi * GROUP_W:(gi + 1) * GROUP_W].astype(BF16)
        acc = jnp.dot(h, w, preferred_element_type=F32)
        if gi == 0:
            for hh in range(RET_HEADS):
                blk = _rope(acc[:, hh * LANES:(hh + 1) * LANES], cq_ref[...], sq_ref[...], first_half)
                o_ref[0, :, hh * LANES:(hh + 1) * LANES] = blk.astype(BF16)
        elif gi == 1:
            for hh in range(RET_HEADS):
                blk = _rope(acc[:, hh * LANES:(hh + 1) * LANES], ck_ref[...], sk_ref[...], first_half)
                o_ref[0, hh * LANES:(hh + 1) * LANES, :] = blk.T.astype(BF16)
        elif gi == 4:
            o_ref[0] = (acc * (NA_DIM ** -0.5 * LOG2_E)).astype(BF16)
        elif gi in _K_GROUPS:
            o_ref[0] = acc.T.astype(BF16)
        else:
            o_ref[0] = acc.astype(BF16)


_CTX_GROUPS = (1, 2, 5, 6)


def _in_ctx_kernel(x_ref, mod_ref, g_ref, w_ref, *out_refs):
    x = jnp.concatenate([x_ref[i] for i in range(CTX_BATCHES)], axis=0)
    h = _prenorm(x, g_ref[...], mod_ref[0, 0:1, :], mod_ref[0, 1:2, :])
    for gi, o_ref in zip(_CTX_GROUPS, out_refs):
        w = w_ref[:, gi * GROUP_W:(gi + 1) * GROUP_W].astype(BF16)
        acc = jnp.dot(h, w, preferred_element_type=F32)
        acc = (acc.T if gi in _K_GROUPS else acc).astype(BF16)
        for i in range(CTX_BATCHES):
            tokens = slice(i * CTX_LEN, (i + 1) * CTX_LEN)
            o_ref[i] = acc[:, tokens] if gi in _K_GROUPS else acc[tokens, :]


def _rope_tables():
    tok = np.arange(SEQ)
    n_freq = RET_DIM // 4
    inv = ROPE_BASE ** (-np.arange(n_freq, dtype=np.float64) / n_freq)
    ang_r = (tok // GRID_W)[:, None] * inv[None, :]
    ang_c = (tok % GRID_W)[:, None] * inv[None, :]
    cos = np.concatenate([np.cos(ang_r), np.cos(ang_r), np.cos(ang_c), np.cos(ang_c)], axis=-1)
    sin = np.concatenate([-np.sin(ang_r), np.sin(ang_r), -np.sin(ang_c), np.sin(ang_c)], axis=-1)
    qs = RET_DIM ** -0.5
    return [jnp.asarray(t, F32) for t in (cos * qs, sin * qs, cos, sin)]


def _in_proj_lat(x, mods, g_pre, w_in):
    batch = x.shape[0]
    tok = lambda b, i: (b, i, 0)
    tab = pl.BlockSpec((ROW_TILE, LANES), lambda b, i: (i, 0))
    out_shapes = [jax.ShapeDtypeStruct((batch, SEQ, GROUP_W), BF16)] * N_GROUPS
    out_specs = [pl.BlockSpec((1, ROW_TILE, GROUP_W), tok)] * N_GROUPS
    for gi in _K_GROUPS:
        out_shapes[gi] = jax.ShapeDtypeStruct((batch, GROUP_W, SEQ), BF16)
        out_specs[gi] = pl.BlockSpec((1, GROUP_W, ROW_TILE), lambda b, i: (b, 0, i))
    return pl.pallas_call(
        _in_lat_kernel,
        grid=(batch, SEQ // ROW_TILE),
        in_specs=[pl.BlockSpec((1, ROW_TILE, D_MODEL), tok),
                  pl.BlockSpec((1, N_MOD, D_MODEL), lambda b, i: (b, 0, 0)),
                  pl.BlockSpec((1, D_MODEL), lambda b, i: (0, 0)),
                  _resident((D_MODEL, N_GROUPS * GROUP_W)),
                  tab, tab, tab, tab],
        out_specs=out_specs,
        out_shape=out_shapes,
        compiler_params=_params(2),
        name="in_lat",
    )(x, mods, g_pre, w_in, *_rope_tables())


def _in_proj_ctx(ctx, mods_ctx, g_pre, w_in):
    batch = ctx.shape[0]
    tok = lambda b: (b, 0, 0)
    out_shapes = [jax.ShapeDtypeStruct((batch, GROUP_W, CTX_LEN) if gi in _K_GROUPS else (batch, CTX_LEN, GROUP_W),
                                       BF16) for gi in _CTX_GROUPS]
    out_specs = [pl.BlockSpec((CTX_BATCHES, GROUP_W, CTX_LEN) if gi in _K_GROUPS else (CTX_BATCHES, CTX_LEN, GROUP_W),
                              tok) for gi in _CTX_GROUPS]
    return pl.pallas_call(
        _in_ctx_kernel,
        grid=(batch // CTX_BATCHES,),
        in_specs=[pl.BlockSpec((CTX_BATCHES, CTX_LEN, D_MODEL), tok),
                  pl.BlockSpec((1, N_MOD, D_MODEL), lambda b: (0, 0, 0)),
                  pl.BlockSpec((1, D_MODEL), lambda b: (0, 0)),
                  _resident((D_MODEL, N_GROUPS * GROUP_W))],
        out_specs=out_specs,
        out_shape=out_shapes,
        compiler_params=_params(1),
        name="in_ctx",
    )(ctx, mods_ctx, g_pre, w_in)


def _ret_kernel(lg_ref, q_ref, kt_ref, v_ref, g_ref, kct_ref, vc_ref, gn_ref, o_ref,
                dec_ref, wq_ref, wk_ref, upd_ref, st_ref, sp_ref):
    head = pl.program_id(0)
    lgf = lg_ref[0, head]
    lgb = lg_ref[1, head]
    c = RET_CHUNK
    n_chunks = SEQ // c

    @pl.when(pl.program_id(1) == 0)
    def _tables():
        ii = lax.broadcasted_iota(jnp.int32, (c, c), 0)
        jj = lax.broadcasted_iota(jnp.int32, (c, c), 1)
        diff = (ii - jj).astype(F32)
        dec_ref[...] = jnp.exp(jnp.where(diff >= 0, lgf * diff, -lgb * diff))
        ri = lax.broadcasted_iota(jnp.int32, (c, LANES), 0).astype(F32)
        wq_ref[0] = jnp.exp(lgf * (ri + 1.0))
        wq_ref[1] = jnp.exp(lgb * (c - ri))
        ti = lax.broadcasted_iota(jnp.int32, (1, c), 1).astype(F32)
        wk_ref[0] = jnp.exp(lgf * (c - 1.0 - ti))
        wk_ref[1] = jnp.exp(lgb * ti)

    ones = jnp.ones((1, LANES), F32)
    chunk_f = jnp.exp(ones * (lgf * c))
    chunk_b = jnp.exp(ones * (lgb * c))

    def update(kt, v):
        ktf = kt.astype(F32)
        kw = jnp.concatenate([(ktf * wk_ref[0]).astype(BF16), (ktf * wk_ref[1]).astype(BF16)], axis=0)
        return jnp.dot(kw, v, preferred_element_type=F32)

    def chunk(n):
        return slice(n * c, (n + 1) * c)

    batches = range(RET_BATCHES)
    for b in batches:
        for n in range(n_chunks):
            upd_ref[b, n] = update(kt_ref[b, :, chunk(n)], v_ref[b, chunk(n), :])

    for b in batches:
        ctx_upd = update(kct_ref[b], vc_ref[b])
        state = ctx_upd[:RET_DIM]
        for n in range(n_chunks):
            st_ref[b, n, 0:RET_DIM, :] = state.astype(BF16)
            state = chunk_f * state + upd_ref[b, n, 0:RET_DIM, :]
        state = ctx_upd[RET_DIM:]
        for n in reversed(range(n_chunks)):
            st_ref[b, n, RET_DIM:, :] = state.astype(BF16)
            state = chunk_b * state + upd_ref[b, n, RET_DIM:, :]

    def scores(b, n):
        s = jnp.dot(q_ref[b, chunk(n), :], kt_ref[b, :, chunk(n)], preferred_element_type=F32)
        sp_ref[b, n % 2] = (s * dec_ref[...]).astype(BF16)

    def outputs(b, n):
        rows = chunk(n)
        qf = q_ref[b, rows, :].astype(F32)
        qw = jnp.concatenate([(qf * wq_ref[0]).astype(BF16), (qf * wq_ref[1]).astype(BF16)], axis=1)
        o = (jnp.dot(sp_ref[b, n % 2], v_ref[b, rows, :], preferred_element_type=F32)
             + jnp.dot(qw, st_ref[b, n], preferred_element_type=F32))
        d = o - jnp.mean(o, axis=-1, keepdims=True)
        y = d * lax.rsqrt(jnp.mean(d * d, axis=-1, keepdims=True) + NORM_EPS) * gn_ref[...]
        o_ref[b, rows, :] = (y * _silu(g_ref[b, rows, :].astype(F32))).astype(BF16)

    for b in batches:
        scores(b, 0)
    for n in range(n_chunks):
        for b in batches:
            if n + 1 < n_chunks:
                scores(b, n + 1)
        for b in batches:
            outputs(b, n)


def _retention(log_gammas, rq, rkt, rv, rg, crkt, crv, gn_w):
    assert CTX_LEN == RET_CHUNK
    batch = rq.shape[0]
    n_chunks = SEQ // RET_CHUNK
    nb = RET_BATCHES
    lat = pl.BlockSpec((nb, SEQ, LANES), lambda h, b: (b, 0, h))
    cx = pl.BlockSpec((nb, CTX_LEN, LANES), lambda h, b: (b, 0, h))
    return pl.pallas_call(
        _ret_kernel,
        grid=(RET_HEADS, batch // nb),
        in_specs=[pl.BlockSpec(memory_space=pltpu.SMEM), lat,
                  pl.BlockSpec((nb, LANES, SEQ), lambda h, b: (b, h, 0)), lat, lat,
                  pl.BlockSpec((nb, LANES, CTX_LEN), lambda h, b: (b, h, 0)), cx,
                  pl.BlockSpec((1, LANES), lambda h, b: (0, h))],
        out_specs=lat,
        out_shape=jax.ShapeDtypeStruct((batch, SEQ, RET_WIDTH), BF16),
        scratch_shapes=[pltpu.VMEM((RET_CHUNK, RET_CHUNK), F32),
                        pltpu.VMEM((2, RET_CHUNK, LANES), F32),
                        pltpu.VMEM((2, 1, RET_CHUNK), F32),
                        pltpu.VMEM((nb, n_chunks, 2 * RET_DIM, RET_DIM), F32),
                        pltpu.VMEM((nb, n_chunks, 2 * RET_DIM, RET_DIM), BF16),
                        pltpu.VMEM((nb, 2, RET_CHUNK, RET_CHUNK), BF16)],
        compiler_params=_params(2),
        name="ret",
    )(log_gammas, rq, rkt, rv, rg, crkt, crv, gn_w)


N_DR = 2 * NA_KH - 1
N_DC = 2 * NA_KW - 1
PAIR_ROWS = 2 * GRID_W
NA_GROUP = 2


def _na_build_bias(rpb_ref, bias_ref, pair):
    qi = lax.broadcasted_iota(jnp.int32, (GRID_W, LANES), 0)
    li = lax.broadcasted_iota(jnp.int32, (GRID_W, LANES), 1)
    key_col = li & (GRID_W - 1)
    col_start = jnp.clip(qi - NA_KW // 2, 0, GRID_W - NA_KW)
    valid = (key_col >= col_start) & (key_col < col_start + NA_KW)
    code = jnp.where(valid, key_col - qi + (NA_KW - 1), -1)
    first_row = lax.broadcasted_iota(jnp.int32, (1, LANES), 1) < GRID_W

    def pattern(t, carry):
        for hd in range(2):
            head = 2 * pair + hd
            for blk in range(NA_KH // 2):
                base = (head * N_DR + (2 * blk + NA_KH - 1 - t)) * N_DC
                acc = jnp.full((GRID_W, LANES), NEG_INF, F32)
                for b in range(N_DC):
                    val = jnp.where(first_row, rpb_ref[base + b], rpb_ref[base + N_DC + b]) * LOG2_E
                    acc = jnp.where(code == b, val, acc)
                bias_ref[t, hd * GRID_W:(hd + 1) * GRID_W, blk * LANES:(blk + 1) * LANES] = acc
        return carry

    lax.fori_loop(0, N_PATTERNS, pattern, 0)


def _na_kernel(rpb_ref, q_ref, kt_ref, v_ref, kct_ref, vc_ref, o_ref,
               bias_ref, kt2_ref, q2_ref, s_ref):
    pair = pl.program_id(0)
    low = lax.broadcasted_iota(jnp.int32, (1, LANES), 1) < NA_DIM
    half = NA_KH // 2

    @pl.when(pl.program_id(1) == 0)
    def _bias():
        _na_build_bias(rpb_ref, bias_ref, pair)

    kt2_ref[0] = kt_ref[0]
    kt2_ref[1, :, 0:SEQ - LANES] = kt_ref[0, :, GRID_W:SEQ - GRID_W]
    kt2_ref[1, :, SEQ - LANES:SEQ] = jnp.zeros((LANES, LANES), BF16)

    def fill_q2(r, carry):
        q = q_ref[0, pl.ds(pl.multiple_of(r * GRID_W, GRID_W), GRID_W), :]
        zero = jnp.zeros_like(q)
        base = pl.multiple_of(r * PAIR_ROWS, PAIR_ROWS)
        q2_ref[pl.ds(base, GRID_W), :] = jnp.where(low, q, zero)
        q2_ref[pl.ds(base + GRID_W, GRID_W), :] = jnp.where(low, zero, q)
        return carry

    lax.fori_loop(0, GRID_ROWS, fill_q2, 0, unroll=4)

    def band_start(r):
        return jnp.clip(r - half, 0, GRID_ROWS - NA_KH)

    def scores(group, slot):
        rows = group * (NA_GROUP * PAIR_ROWS)
        group_q = pl.ds(pl.multiple_of(rows, NA_GROUP * PAIR_ROWS), NA_GROUP * PAIR_ROWS)
        s_ref[slot, :, NA_BAND:] = jnp.dot(q2_ref[group_q, :], kct_ref[0], preferred_element_type=F32)
        for j in range(NA_GROUP):
            r = group * NA_GROUP + j
            start = band_start(r)
            pattern = jnp.where(r < half, r, jnp.where(r > GRID_ROWS - half, r - (GRID_ROWS - NA_KH), half))
            lanes = pl.ds(pl.multiple_of((start >> 1) * LANES, LANES), NA_BAND)
            kb = kt2_ref[start & 1, :, lanes]
            q2 = q2_ref[pl.ds(pl.multiple_of(r * PAIR_ROWS, PAIR_ROWS), PAIR_ROWS), :]
            s_ref[slot, j * PAIR_ROWS:(j + 1) * PAIR_ROWS, 0:NA_BAND] = (
                jnp.dot(q2, kb, preferred_element_type=F32) + bias_ref[pattern])

    n_blocks = (NA_BAND + CTX_LEN) // LANES

    def outputs(group, slot):
        probs = []
        for j in range(NA_GROUP):
            rows = slice(j * PAIR_ROWS, (j + 1) * PAIR_ROWS)

            def block(b):
                return s_ref[slot, rows, b * LANES:(b + 1) * LANES]

            m = block(0)
            for b in range(1, n_blocks):
                m = jnp.maximum(m, block(b))
            m = jnp.max(m, axis=-1, keepdims=True)
            p = [jnp.exp2(block(b) - m) for b in range(n_blocks)]
            den = p[0]
            for pb in p[1:]:
                den = den + pb
            den = jnp.sum(den, axis=-1, keepdims=True)
            probs.append((jnp.concatenate([pb.astype(BF16) for pb in p], axis=1), den))
        ctx = jnp.dot(jnp.concatenate([p[:, NA_BAND:] for p, _ in probs], axis=0), vc_ref[0],
                      preferred_element_type=F32)
        for j, (p, den) in enumerate(probs):
            r = group * NA_GROUP + j
            vb = v_ref[0, pl.ds(pl.multiple_of(band_start(r) * GRID_W, GRID_W), NA_BAND), :]
            o = (jnp.dot(p[:, 0:NA_BAND], vb, preferred_element_type=F32)
                 + ctx[j * PAIR_ROWS:(j + 1) * PAIR_ROWS]) * (1.0 / den)
            o_ref[0, pl.ds(pl.multiple_of(r * GRID_W, GRID_W), GRID_W), :] = (
                jnp.where(low, o[:GRID_W], o[GRID_W:]).astype(BF16))

    n_groups = GRID_ROWS // NA_GROUP
    scores(0, 0)
    for g in range(n_groups):
        if g + 1 < n_groups:
            scores(g + 1, (g + 1) % 2)
        outputs(g, g % 2)


def _neighbourhood_attention(rpb, nq, nkt, nv, cnkt, cnv):
    batch = nq.shape[0]
    n_q2 = GRID_ROWS * PAIR_ROWS
    lat = pl.BlockSpec((1, SEQ, LANES), lambda p, b: (b, 0, p))
    cx = pl.BlockSpec((1, CTX_LEN, LANES), lambda p, b: (b, 0, p))
    return pl.pallas_call(
        _na_kernel,
        grid=(NA_HEADS // 2, batch),
        in_specs=[pl.BlockSpec(memory_space=pltpu.SMEM), lat,
                  pl.BlockSpec((1, LANES, SEQ), lambda p, b: (b, p, 0)), lat,
                  pl.BlockSpec((1, LANES, CTX_LEN), lambda p, b: (b, p, 0)), cx],
        out_specs=lat,
        out_shape=jax.ShapeDtypeStruct((batch, SEQ, NA_WIDTH), BF16),
        scratch_shapes=[pltpu.VMEM((N_PATTERNS, PAIR_ROWS, NA_BAND), F32),
                        pltpu.VMEM((2, LANES, SEQ), BF16),
                        pltpu.VMEM((n_q2, LANES), BF16),
                        pltpu.VMEM((2, NA_GROUP * PAIR_ROWS, NA_BAND + CTX_LEN), F32)],
        compiler_params=_params(2),
        name="na",
    )(rpb, nq, nkt, nv, cnkt, cnv)


def _out_kernel(x_ref, ret_ref, na_ref, mod_ref, gpm_ref, gpre_ref, gpost_ref, wo_ref, w1_ref, w2_ref, o_ref):
    gt1, sh2, sc2, gt2 = (mod_ref[0, i:i + 1, :] for i in (2, 3, 4, 5))
    subs = [slice(i * SUB_TILE, (i + 1) * SUB_TILE) for i in range(OUT_TILE // SUB_TILE)]
    mix = [jnp.dot(ret_ref[0, rows, :], wo_ref[0:RET_WIDTH, :], preferred_element_type=F32)
           + jnp.dot(na_ref[0, rows, :], wo_ref[RET_WIDTH:, :], preferred_element_type=F32) for rows in subs]
    x1 = [x_ref[0, rows, :] + gt1 * _rms(m, gpm_ref[...]) for rows, m in zip(subs, mix)]
    h2 = [_prenorm(x, gpre_ref[...], sh2, sc2) for x in x1]
    mlp = []
    for h in h2:
        acc = jnp.zeros((SUB_TILE, D_MODEL), F32)
        for ci in range(D_FF // FF_CHUNK):
            cols = slice(ci * FF_CHUNK, (ci + 1) * FF_CHUNK)
            a = jnp.maximum(jnp.dot(h, w1_ref[:, cols], preferred_element_type=F32), 0.0)
            acc = acc + jnp.dot((a * a).astype(BF16), w2_ref[cols, :], preferred_element_type=F32)
        mlp.append(acc)
    for rows, x, y in zip(subs, x1, mlp):
        o_ref[0, rows, :] = x + gt2 * _rms(y, gpost_ref[...])


def _out_proj_mlp(x, ret_lat, na_lat, mods, g_post_mix, g_pre_mlp, g_post_mlp, w_out, w_mlp1, w_mlp2):
    batch = x.shape[0]
    tok = lambda b, i: (b, i, 0)
    gain = pl.BlockSpec((1, D_MODEL), lambda b, i: (0, 0))
    return pl.pallas_call(
        _out_kernel,
        grid=(batch, SEQ // OUT_TILE),
        in_specs=[pl.BlockSpec((1, OUT_TILE, D_MODEL), tok),
                  pl.BlockSpec((1, OUT_TILE, RET_WIDTH), tok),
                  pl.BlockSpec((1, OUT_TILE, NA_WIDTH), tok),
                  pl.BlockSpec((1, N_MOD, D_MODEL), lambda b, i: (b, 0, 0)),
                  gain, gain, gain,
                  _resident((RET_WIDTH + NA_WIDTH, D_MODEL)),
                  _resident((D_MODEL, D_FF)),
                  _resident((D_FF, D_MODEL))],
        out_specs=pl.BlockSpec((1, OUT_TILE, D_MODEL), tok),
        out_shape=jax.ShapeDtypeStruct(x.shape, F32),
        compiler_params=_params(2),
        name="out_mlp",
    )(x, ret_lat, na_lat, mods, g_post_mix, g_pre_mlp, g_post_mlp, w_out, w_mlp1, w_mlp2)


def kernel(x, c, ctx, c_ctx, w_ada, b_ada, g_pre_mix, g_post_mix, g_pre_mlp, g_post_mlp,
           w_in, ret_decay, ret_gn, na_rpb, w_out, w_mlp1, w_mlp2):
    assert w_in.shape[0] == 1, "single-layer block: the context stream is never updated"
    batch = x.shape[0]
    pad = (-(batch + 1)) % 8
    cc = jnp.concatenate([c, c_ctx[None, :], jnp.zeros((pad, D_MODEL), F32)], axis=0)
    mods = _modulations(cc, w_ada[0], b_ada)
    mods_lat = mods[:batch].reshape(batch, N_MOD, D_MODEL)
    mods_ctx = mods[batch:batch + 1].reshape(1, N_MOD, D_MODEL)

    rq, rk, rv, rg, nq, nkt, nv = _in_proj_lat(x, mods_lat, g_pre_mix, w_in[0])
    crk, crv, cnkt, cnv = _in_proj_ctx(ctx, mods_ctx, g_pre_mix, w_in[0])

    log_gammas = jax.nn.log_sigmoid(ret_decay[0].astype(F32))
    ret_lat = _retention(log_gammas, rq, rk, rv, rg, crk, crv, ret_gn)
    na_lat = _neighbourhood_attention(na_rpb[0].astype(F32).reshape(-1), nq, nkt, nv, cnkt, cnv)

    return _out_proj_mlp(x, ret_lat, na_lat, mods_lat, g_post_mix, g_pre_mlp, g_post_mlp,
                         w_out[0].astype(BF16), w_mlp1[0].astype(BF16), w_mlp2[0].astype(BF16))
```

```python
import numpy as np
import jax
import jax.numpy as jnp
from jax import lax
from jax.experimental import pallas as pl
from jax.experimental.pallas import tpu as pltpu

D_MODEL = 1024
SEQ = 2048
CTX_LEN = 256
GRID_W = 64
GRID_ROWS = SEQ // GRID_W
RET_HEADS = 4
RET_DIM = 128
RET_WIDTH = RET_HEADS * RET_DIM
NA_HEADS = 8
NA_DIM = 64
NA_WIDTH = NA_HEADS * NA_DIM
NA_KH = 8
NA_KW = 16
N_GROUPS = 7
GROUP_W = 512
_K_GROUPS = (1, 5)
D_FF = 4 * D_MODEL
ROPE_BASE = 10000.0
NORM_EPS = 1e-6
N_MOD = 6
NEG_INF = -1e30
LOG2_E = 1.4426950408889634

LANES = 128
ROW_TILE = 1024
CTX_BATCHES = 2
OUT_TILE = 512
SUB_TILE = 256
FF_CHUNK = 1024
RET_CHUNK = 256
RET_BATCHES = 2
NA_BAND = NA_KH * GRID_W
N_PATTERNS = 8
VMEM_LIMIT = 56 * 1024 * 1024

F32 = jnp.float32
BF16 = jnp.bfloat16
_NT = (((1,), (1,)), ((), ()))
_TN = (((0,), (0,)), ((), ()))


def _silu(x):
    return x * (1.0 / (1.0 + jnp.exp(-x)))


def _rms(x, g):
    return x * lax.rsqrt(jnp.mean(x * x, axis=-1, keepdims=True) + NORM_EPS) * g


def _params(n_axes):
    return pltpu.CompilerParams(dimension_semantics=("arbitrary",) * n_axes,
                                vmem_limit_bytes=VMEM_LIMIT)


def _resident(shape):
    nd = len(shape)
    return pl.BlockSpec(shape, lambda *_: (0,) * nd, pipeline_mode=pl.Buffered(1))


def _mod_kernel(c_ref, w_ref, b_ref, o_ref):
    a = _silu(c_ref[...]).astype(BF16)
    o_ref[...] = jnp.dot(a, w_ref[...].astype(BF16), preferred_element_type=F32) + b_ref[...]


def _modulations(cc, w_ada, b_ada):
    rows = cc.shape[0]
    return pl.pallas_call(
        _mod_kernel,
        grid=(N_MOD,),
        in_specs=[pl.BlockSpec((rows, D_MODEL), lambda j: (0, 0)),
                  pl.BlockSpec((D_MODEL, D_MODEL), lambda j: (0, j)),
                  pl.BlockSpec((1, D_MODEL), lambda j: (0, j))],
        out_specs=pl.BlockSpec((rows, D_MODEL), lambda j: (0, j)),
        out_shape=jax.ShapeDtypeStruct((rows, N_MOD * D_MODEL), F32),
        compiler_params=_params(1),
        name="mod",
    )(cc, w_ada, b_ada)


def _prenorm(x, g, shift, scale):
    return (_rms(x, g) * (1.0 + scale) + shift).astype(BF16)


def _rope(blk, cos, sin_signed, first_half):
    partner = jnp.where(first_half, pltpu.roll(blk, LANES - 32, 1), pltpu.roll(blk, 32, 1))
    return blk * cos + partner * sin_signed


def _in_lat_kernel(x_ref, mod_ref, g_ref, w_ref, cq_ref, sq_ref, ck_ref, sk_ref, *out_refs):
    h = _prenorm(x_ref[0], g_ref[...], mod_ref[0, 0:1, :], mod_ref[0, 1:2, :])
    lane = lax.broadcasted_iota(jnp.int32, (1, LANES), 1)
    first_half = (lane & 32) == 0
    for gi, o_ref in enumerate(out_refs):
        w = w_ref[:, gi * GROUP_W:(gi + 1) * GROUP_W].astype(BF16)
        acc = jnp.dot(h, w, preferred_element_type=F32)
        if gi == 0:
            for hh in range(RET_HEADS):
                blk = _rope(acc[:, hh * LANES:(hh + 1) * LANES], cq_ref[...], sq_ref[...], first_half)
                o_ref[0, :, hh * LANES:(hh + 1) * LANES] = blk.astype(BF16)
        elif gi == 1:
            for hh in range(RET_HEADS):
                blk = _rope(acc[:, hh * LANES:(hh + 1) * LANES], ck_ref[...], sk_ref[...], first_half)
                o_ref[0, hh * LANES:(hh + 1) * LANES, :] = blk.T.astype(BF16)
        elif gi == 4:
            o_ref[0] = (acc * (NA_DIM ** -0.5 * LOG2_E)).astype(BF16)
        elif gi in _K_GROUPS:
            o_ref[0] = acc.T.astype(BF16)
        else:
            o_ref[0] = acc.astype(BF16)


_CTX_GROUPS = (1, 2, 5, 6)


def _in_ctx_kernel(x_ref, mod_ref, g_ref, w_ref, *out_refs):
    x = jnp.concatenate([x_ref[i] for i in range(CTX_BATCHES)], axis=0)
    h = _prenorm(x, g_ref[...], mod_ref[0, 0:1, :], mod_ref[0, 1:2, :])
    for gi, o_ref in zip(_CTX_GROUPS, out_refs):
        w = w_ref[:, gi * GROUP_W:(gi + 1) * GROUP_W].astype(BF16)
        acc = jnp.dot(h, w, preferred_element_type=F32)
        acc = (acc.T if gi in _K_GROUPS else acc).astype(BF16)
        for i in range(CTX_BATCHES):
            tokens = slice(i * CTX_LEN, (i + 1) * CTX_LEN)
            o_ref[i] = acc[:, tokens] if gi in _K_GROUPS else acc[tokens, :]


def _rope_tables():
    tok = np.arange(SEQ)
    n_freq = RET_DIM // 4
    inv = ROPE_BASE ** (-np.arange(n_freq, dtype=np.float64) / n_freq)
    ang_r = (tok // GRID_W)[:, None] * inv[None, :]
    ang_c = (tok % GRID_W)[:, None] * inv[None, :]
    cos = np.concatenate([np.cos(ang_r), np.cos(ang_r), np.cos(ang_c), np.cos(ang_c)], axis=-1)
    sin = np.concatenate([-np.sin(ang_r), np.sin(ang_r), -np.sin(ang_c), np.sin(ang_c)], axis=-1)
    qs = RET_DIM ** -0.5
    return [jnp.asarray(t, F32) for t in (cos * qs, sin * qs, cos, sin)]


def _in_proj_lat(x, mods, g_pre, w_in):
    batch = x.shape[0]
    tok = lambda b, i: (b, i, 0)
    tab = pl.BlockSpec((ROW_TILE, LANES), lambda b, i: (i, 0))
    out_shapes = [jax.ShapeDtypeStruct((batch, SEQ, GROUP_W), BF16)] * N_GROUPS
    out_specs = [pl.BlockSpec((1, ROW_TILE, GROUP_W), tok)] * N_GROUPS
    for gi in _K_GROUPS:
        out_shapes[gi] = jax.ShapeDtypeStruct((batch, GROUP_W, SEQ), BF16)
        out_specs[gi] = pl.BlockSpec((1, GROUP_W, ROW_TILE), lambda b, i: (b, 0, i))
    return pl.pallas_call(
        _in_lat_kernel,
        grid=(batch, SEQ // ROW_TILE),
        in_specs=[pl.BlockSpec((1, ROW_TILE, D_MODEL), tok),
                  pl.BlockSpec((1, N_MOD, D_MODEL), lambda b, i: (b, 0, 0)),
                  pl.BlockSpec((1, D_MODEL), lambda b, i: (0, 0)),
                  _resident((D_MODEL, N_GROUPS * GROUP_W)),
                  tab, tab, tab, tab],
        out_specs=out_specs,
        out_shape=out_shapes,
        compiler_params=_params(2),
        name="in_lat",
    )(x, mods, g_pre, w_in, *_rope_tables())


def _in_proj_ctx(ctx, mods_ctx, g_pre, w_in):
    batch = ctx.shape[0]
    tok = lambda b: (b, 0, 0)
    out_shapes = [jax.ShapeDtypeStruct((batch, GROUP_W, CTX_LEN) if gi in _K_GROUPS else (batch, CTX_LEN, GROUP_W),
                                       BF16) for gi in _CTX_GROUPS]
    out_specs = [pl.BlockSpec((CTX_BATCHES, GROUP_W, CTX_LEN) if gi in _K_GROUPS else (CTX_BATCHES, CTX_LEN, GROUP_W),
                              tok) for gi in _CTX_GROUPS]
    return pl.pallas_call(
        _in_ctx_kernel,
        grid=(batch // CTX_BATCHES,),
        in_specs=[pl.BlockSpec((CTX_BATCHES, CTX_LEN, D_MODEL), tok),
                  pl.BlockSpec((1, N_MOD, D_MODEL), lambda b: (0, 0, 0)),
                  pl.BlockSpec((1, D_MODEL), lambda b: (0, 0)),
                  _resident((D_MODEL, N_GROUPS * GROUP_W))],
        out_specs=out_specs,
        out_shape=out_shapes,
        compiler_params=_params(1),
        name="in_ctx",
    )(ctx, mods_ctx, g_pre, w_in)


def _ret_kernel(lg_ref, q_ref, kt_ref, v_ref, g_ref, kct_ref, vc_ref, gn_ref, o_ref,
                dec_ref, wq_ref, wk_ref, upd_ref, st_ref, sp_ref):
    head = pl.program_id(0)
    lgf = lg_ref[0, head]
    lgb = lg_ref[1, head]
    c = RET_CHUNK
    n_chunks = SEQ // c

    @pl.when(pl.program_id(1) == 0)
    def _tables():
        ii = lax.broadcasted_iota(jnp.int32, (c, c), 0)
        jj = lax.broadcasted_iota(jnp.int32, (c, c), 1)
        diff = (ii - jj).astype(F32)
        dec_ref[...] = jnp.exp(jnp.where(diff >= 0, lgf * diff, -lgb * diff))
        ri = lax.broadcasted_iota(jnp.int32, (c, LANES), 0).astype(F32)
        wq_ref[0] = jnp.exp(lgf * (ri + 1.0))
        wq_ref[1] = jnp.exp(lgb * (c - ri))
        ti = lax.broadcasted_iota(jnp.int32, (1, c), 1).astype(F32)
        wk_ref[0] = jnp.exp(lgf * (c - 1.0 - ti))
        wk_ref[1] = jnp.exp(lgb * ti)

    ones = jnp.ones((1, LANES), F32)
    chunk_f = jnp.exp(ones * (lgf * c))
    chunk_b = jnp.exp(ones * (lgb * c))

    def update(kt, v):
        ktf = kt.astype(F32)
        kw = jnp.concatenate([(ktf * wk_ref[0]).astype(BF16), (ktf * wk_ref[1]).astype(BF16)], axis=0)
        return jnp.dot(kw, v, preferred_element_type=F32)

    def chunk(n):
        return slice(n * c, (n + 1) * c)

    batches = range(RET_BATCHES)
    for b in batches:
        for n in range(n_chunks):
            upd_ref[b, n] = update(kt_ref[b, :, chunk(n)], v_ref[b, chunk(n), :])

    for b in batches:
        ctx_upd = update(kct_ref[b], vc_ref[b])
        state = ctx_upd[:RET_DIM]
        for n in range(n_chunks):
            st_ref[b, n, 0:RET_DIM, :] = state.astype(BF16)
            state = chunk_f * state + upd_ref[b, n, 0:RET_DIM, :]
        state = ctx_upd[RET_DIM:]
        for n in reversed(range(n_chunks)):
            st_ref[b, n, RET_DIM:, :] = state.astype(BF16)
            state = chunk_b * state + upd_ref[b, n, RET_DIM:, :]

    def scores(b, n):
        s = jnp.dot(q_ref[b, chunk(n), :], kt_ref[b, :, chunk(n)], preferred_element_type=F32)
        sp_ref[b, n % 2] = (s * dec_ref[...]).astype(BF16)

    def outputs(b, n):
        rows = chunk(n)
        qf = q_ref[b, rows, :].astype(F32)
        qw = jnp.concatenate([(qf * wq_ref[0]).astype(BF16), (qf * wq_ref[1]).astype(BF16)], axis=1)
        o = (jnp.dot(sp_ref[b, n % 2], v_ref[b, rows, :], preferred_element_type=F32)
             + jnp.dot(qw, st_ref[b, n], preferred_element_type=F32))
        d = o - jnp.mean(o, axis=-1, keepdims=True)
        y = d * lax.rsqrt(jnp.mean(d * d, axis=-1, keepdims=True) + NORM_EPS) * gn_ref[...]
        o_ref[b, rows, :] = (y * _silu(g_ref[b, rows, :].astype(F32))).astype(BF16)

    for b in batches:
        scores(b, 0)
    for n in range(n_chunks):
        for b in batches:
            if n + 1 < n_chunks:
                scores(b, n + 1)
        for b in batches:
            outputs(b, n)


def _retention(log_gammas, rq, rkt, rv, rg, crkt, crv, gn_w):
    assert CTX_LEN == RET_CHUNK
    batch = rq.shape[0]
    n_chunks = SEQ // RET_CHUNK
    nb = RET_BATCHES
    lat = pl.BlockSpec((nb, SEQ, LANES), lambda h, b: (b, 0, h))
    cx = pl.BlockSpec((nb, CTX_LEN, LANES), lambda h, b: (b, 0, h))
    return pl.pallas_call(
        _ret_kernel,
        grid=(RET_HEADS, batch // nb),
        in_specs=[pl.BlockSpec(memory_space=pltpu.SMEM), lat,
                  pl.BlockSpec((nb, LANES, SEQ), lambda h, b: (b, h, 0)), lat, lat,
                  pl.BlockSpec((nb, LANES, CTX_LEN), lambda h, b: (b, h, 0)), cx,
                  pl.BlockSpec((1, LANES), lambda h, b: (0, h))],
        out_specs=lat,
        out_shape=jax.ShapeDtypeStruct((batch, SEQ, RET_WIDTH), BF16),
        scratch_shapes=[pltpu.VMEM((RET_CHUNK, RET_CHUNK), F32),
                        pltpu.VMEM((2, RET_CHUNK, LANES), F32),
                        pltpu.VMEM((2, 1, RET_CHUNK), F32),
                        pltpu.VMEM((nb, n_chunks, 2 * RET_DIM, RET_DIM), F32),
                        pltpu.VMEM((nb, n_chunks, 2 * RET_DIM, RET_DIM), BF16),
                        pltpu.VMEM((nb, 2, RET_CHUNK, RET_CHUNK), BF16)],
        compiler_params=_params(2),
        name="ret",
    )(log_gammas, rq, rkt, rv, rg, crkt, crv, gn_w)


N_DR = 2 * NA_KH - 1
N_DC = 2 * NA_KW - 1
PAIR_ROWS = 2 * GRID_W
NA_GROUP = 2
NA_BATCHES = 2


def _na_build_bias(rpb_ref, bias_ref, pair):
    qi = lax.broadcasted_iota(jnp.int32, (GRID_W, LANES), 0)
    li = lax.broadcasted_iota(jnp.int32, (GRID_W, LANES), 1)
    key_col = li & (GRID_W - 1)
    col_start = jnp.clip(qi - NA_KW // 2, 0, GRID_W - NA_KW)
    valid = (key_col >= col_start) & (key_col < col_start + NA_KW)
    code = jnp.where(valid, key_col - qi + (NA_KW - 1), -1)
    first_row = lax.broadcasted_iota(jnp.int32, (1, LANES), 1) < GRID_W

    def pattern(t, carry):
        for hd in range(2):
            head = 2 * pair + hd
            for blk in range(NA_KH // 2):
                base = (head * N_DR + (2 * blk + NA_KH - 1 - t)) * N_DC
                acc = jnp.full((GRID_W, LANES), NEG_INF, F32)
                for b in range(N_DC):
                    val = jnp.where(first_row, rpb_ref[base + b], rpb_ref[base + N_DC + b]) * LOG2_E
                    acc = jnp.where(code == b, val, acc)
                bias_ref[t, hd * GRID_W:(hd + 1) * GRID_W, blk * LANES:(blk + 1) * LANES] = acc
        return carry

    lax.fori_loop(0, N_PATTERNS, pattern, 0)


def _na_kernel(rpb_ref, q_ref, kt_ref, v_ref, kct_ref, vc_ref, o_ref,
               bias_ref, kt2_ref, q2_ref, s_ref):
    pair = pl.program_id(0)
    low = lax.broadcasted_iota(jnp.int32, (1, LANES), 1) < NA_DIM
    half = NA_KH // 2

    @pl.when(pl.program_id(1) == 0)
    def _bias():
        _na_build_bias(rpb_ref, bias_ref, pair)

    batches = range(NA_BATCHES)
    for bb in batches:
        kt2_ref[bb, 0] = kt_ref[bb]
        kt2_ref[bb, 1, :, 0:SEQ - LANES] = kt_ref[bb, :, GRID_W:SEQ - GRID_W]
        kt2_ref[bb, 1, :, SEQ - LANES:SEQ] = jnp.zeros((LANES, LANES), BF16)

    def fill_q2(r, carry):
        base = pl.multiple_of(r * PAIR_ROWS, PAIR_ROWS)
        for bb in batches:
            q = q_ref[bb, pl.ds(pl.multiple_of(r * GRID_W, GRID_W), GRID_W), :]
            zero = jnp.zeros_like(q)
            q2_ref[bb, pl.ds(base, GRID_W), :] = jnp.where(low, q, zero)
            q2_ref[bb, pl.ds(base + GRID_W, GRID_W), :] = jnp.where(low, zero, q)
        return carry

    lax.fori_loop(0, GRID_ROWS, fill_q2, 0, unroll=4)

    def band_start(r):
        return jnp.clip(r - half, 0, GRID_ROWS - NA_KH)

    def scores(bb, group, slot):
        rows = group * (NA_GROUP * PAIR_ROWS)
        group_q = pl.ds(pl.multiple_of(rows, NA_GROUP * PAIR_ROWS), NA_GROUP * PAIR_ROWS)
        s_ref[bb, slot, :, NA_BAND:] = jnp.dot(q2_ref[bb, group_q, :], kct_ref[bb], preferred_element_type=F32)
        for j in range(NA_GROUP):
            r = group * NA_GROUP + j
            start = band_start(r)
            pattern = jnp.where(r < half, r, jnp.where(r > GRID_ROWS - half, r - (GRID_ROWS - NA_KH), half))
            lanes = pl.ds(pl.multiple_of((start >> 1) * LANES, LANES), NA_BAND)
            kb = kt2_ref[bb, start & 1, :, lanes]
            q2 = q2_ref[bb, pl.ds(pl.multiple_of(r * PAIR_ROWS, PAIR_ROWS), PAIR_ROWS), :]
            s_ref[bb, slot, j * PAIR_ROWS:(j + 1) * PAIR_ROWS, 0:NA_BAND] = (
                jnp.dot(q2, kb, preferred_element_type=F32) + bias_ref[pattern])

    n_blocks = (NA_BAND + CTX_LEN) // LANES

    def outputs(bb, group, slot):
        probs = []
        for j in range(NA_GROUP):
            rows = slice(j * PAIR_ROWS, (j + 1) * PAIR_ROWS)

            def block(b):
                return s_ref[bb, slot, rows, b * LANES:(b + 1) * LANES]

            m = block(0)
            for b in range(1, n_blocks):
                m = jnp.maximum(m, block(b))
            m = jnp.max(m, axis=-1, keepdims=True)
            p = [jnp.exp2(block(b) - m) for b in range(n_blocks)]
            den = p[0]
            for pb in p[1:]:
                den = den + pb
            den = jnp.sum(den, axis=-1, keepdims=True)
            probs.append((jnp.concatenate([pb.astype(BF16) for pb in p], axis=1), den))
        ctx = jnp.dot(jnp.concatenate([p[:, NA_BAND:] for p, _ in probs], axis=0), vc_ref[bb],
                      preferred_element_type=F32)
        for j, (p, den) in enumerate(probs):
            r = group * NA_GROUP + j
            vb = v_ref[bb, pl.ds(pl.multiple_of(band_start(r) * GRID_W, GRID_W), NA_BAND), :]
            o = (jnp.dot(p[:, 0:NA_BAND], vb, preferred_element_type=F32)
                 + ctx[j * PAIR_ROWS:(j + 1) * PAIR_ROWS]) * (1.0 / den)
            o_ref[bb, pl.ds(pl.multiple_of(r * GRID_W, GRID_W), GRID_W), :] = (
                jnp.where(low, o[:GRID_W], o[GRID_W:]).astype(BF16))

    n_groups = GRID_ROWS // NA_GROUP
    for bb in batches:
        scores(bb, 0, 0)
    for g in range(n_groups):
        for bb in batches:
            if g + 1 < n_groups:
                scores(bb, g + 1, (g + 1) % 2)
        for bb in batches:
            outputs(bb, g, g % 2)


def _neighbourhood_attention(rpb, nq, nkt, nv, cnkt, cnv):
    batch = nq.shape[0]
    n_q2 = GRID_ROWS * PAIR_ROWS
    nb = NA_BATCHES
    lat = pl.BlockSpec((nb, SEQ, LANES), lambda p, b: (b, 0, p))
    cx = pl.BlockSpec((nb, CTX_LEN, LANES), lambda p, b: (b, 0, p))
    return pl.pallas_call(
        _na_kernel,
        grid=(NA_HEADS // 2, batch // nb),
        in_specs=[pl.BlockSpec(memory_space=pltpu.SMEM), lat,
                  pl.BlockSpec((nb, LANES, SEQ), lambda p, b: (b, p, 0)), lat,
                  pl.BlockSpec((nb, LANES, CTX_LEN), lambda p, b: (b, p, 0)), cx],
        out_specs=lat,
        out_shape=jax.ShapeDtypeStruct((batch, SEQ, NA_WIDTH), BF16),
        scratch_shapes=[pltpu.VMEM((N_PATTERNS, PAIR_ROWS, NA_BAND), F32),
                        pltpu.VMEM((nb, 2, LANES, SEQ), BF16),
                        pltpu.VMEM((nb, n_q2, LANES), BF16),
                        pltpu.VMEM((nb, 2, NA_GROUP * PAIR_ROWS, NA_BAND + CTX_LEN), F32)],
        compiler_params=_params(2),
        name="na",
    )(rpb, nq, nkt, nv, cnkt, cnv)


def _out_kernel(x_ref, ret_ref, na_ref, mod_ref, gpm_ref, gpre_ref, gpost_ref, wo_ref, w1_ref, w2_ref, o_ref):
    gt1, sh2, sc2, gt2 = (mod_ref[0, i:i + 1, :] for i in (2, 3, 4, 5))
    subs = [slice(i * SUB_TILE, (i + 1) * SUB_TILE) for i in range(OUT_TILE // SUB_TILE)]
    mix = [jnp.dot(ret_ref[0, rows, :], wo_ref[0:RET_WIDTH, :], preferred_element_type=F32)
           + jnp.dot(na_ref[0, rows, :], wo_ref[RET_WIDTH:, :], preferred_element_type=F32) for rows in subs]
    x1 = [x_ref[0, rows, :] + gt1 * _rms(m, gpm_ref[...]) for rows, m in zip(subs, mix)]
    h2 = [_prenorm(x, gpre_ref[...], sh2, sc2) for x in x1]
    mlp = []
    for h in h2:
        acc = jnp.zeros((SUB_TILE, D_MODEL), F32)
        for ci in range(D_FF // FF_CHUNK):
            cols = slice(ci * FF_CHUNK, (ci + 1) * FF_CHUNK)
            a = jnp.maximum(jnp.dot(h, w1_ref[:, cols], preferred_element_type=F32), 0.0)
            acc = acc + jnp.dot((a * a).astype(BF16), w2_ref[cols, :], preferred_element_type=F32)
        mlp.append(acc)
    for rows, x, y in zip(subs, x1, mlp):
        o_ref[0, rows, :] = x + gt2 * _rms(y, gpost_ref[...])


def _out_proj_mlp(x, ret_lat, na_lat, mods, g_post_mix, g_pre_mlp, g_post_mlp, w_out, w_mlp1, w_mlp2):
    batch = x.shape[0]
    tok = lambda b, i: (b, i, 0)
    gain = pl.BlockSpec((1, D_MODEL), lambda b, i: (0, 0))
    return pl.pallas_call(
        _out_kernel,
        grid=(batch, SEQ // OUT_TILE),
        in_specs=[pl.BlockSpec((1, OUT_TILE, D_MODEL), tok),
                  pl.BlockSpec((1, OUT_TILE, RET_WIDTH), tok),
                  pl.BlockSpec((1, OUT_TILE, NA_WIDTH), tok),
                  pl.BlockSpec((1, N_MOD, D_MODEL), lambda b, i: (b, 0, 0)),
                  gain, gain, gain,
                  _resident((RET_WIDTH + NA_WIDTH, D_MODEL)),
                  _resident((D_MODEL, D_FF)),
                  _resident((D_FF, D_MODEL))],
        out_specs=pl.BlockSpec((1, OUT_TILE, D_MODEL), tok),
        out_shape=jax.ShapeDtypeStruct(x.shape, F32),
        compiler_params=_params(2),
        name="out_mlp",
    )(x, ret_lat, na_lat, mods, g_post_mix, g_pre_mlp, g_post_mlp, w_out, w_mlp1, w_mlp2)


def kernel(x, c, ctx, c_ctx, w_ada, b_ada, g_pre_mix, g_post_mix, g_pre_mlp, g_post_mlp,
           w_in, ret_decay, ret_gn, na_rpb, w_out, w_mlp1, w_mlp2):
    assert w_in.shape[0] == 1, "single-layer block: the context stream is never updated"
    batch = x.shape[0]
    pad = (-(batch + 1)) % 8
    cc = jnp.concatenate([c, c_ctx[None, :], jnp.zeros((pad, D_MODEL), F32)], axis=0)
    mods = _modulations(cc, w_ada[0], b_ada)
    mods_lat = mods[:batch].reshape(batch, N_MOD, D_MODEL)
    mods_ctx = mods[batch:batch + 1].reshape(1, N_MOD, D_MODEL)

    rq, rk, rv, rg, nq, nkt, nv = _in_proj_lat(x, mods_lat, g_pre_mix, w_in[0])
    crk, crv, cnkt, cnv = _in_proj_ctx(ctx, mods_ctx, g_pre_mix, w_in[0])

    log_gammas = jax.nn.log_sigmoid(ret_decay[0].astype(F32))
    ret_lat = _retention(log_gammas, rq, rk, rv, rg, crk, crv, ret_gn)
    na_lat = _neighbourhood_attention(na_rpb[0].astype(F32).reshape(-1), nq, nkt, nv, cnkt, cnv)

    return _out_proj_mlp(x, ret_lat, na_lat, mods_lat, g_post_mix, g_pre_mlp, g_post_mlp,
                         w_out[0].astype(BF16), w_mlp1[0].astype(BF16), w_mlp2[0].astype(BF16))
```

```python
import numpy as np
import jax
import jax.numpy as jnp
from jax import lax
from jax.experimental import pallas as pl
from jax.experimental.pallas import tpu as pltpu

D_MODEL = 1024
SEQ = 2048
CTX_LEN = 256
GRID_W = 64
GRID_ROWS = SEQ // GRID_W
RET_HEADS = 4
RET_DIM = 128
RET_WIDTH = RET_HEADS * RET_DIM
NA_HEADS = 8
NA_DIM = 64
NA_WIDTH = NA_HEADS * NA_DIM
NA_KH = 8
NA_KW = 16
N_GROUPS = 7
GROUP_W = 512
_K_GROUPS = (1, 5)
D_FF = 4 * D_MODEL
ROPE_BASE = 10000.0
NORM_EPS = 1e-6
N_MOD = 6
NEG_INF = -1e30
LOG2_E = 1.4426950408889634

LANES = 128
ROW_TILE = 1024
CTX_BATCHES = 2
OUT_TILE = 512
SUB_TILE = 256
FF_CHUNK = 1024
RET_CHUNK = 256
RET_BATCHES = 2
NA_BAND = NA_KH * GRID_W
N_PATTERNS = 8
VMEM_LIMIT = 56 * 1024 * 1024

F32 = jnp.float32
BF16 = jnp.bfloat16
_NT = (((1,), (1,)), ((), ()))
_TN = (((0,), (0,)), ((), ()))


def _silu(x):
    return x * (1.0 / (1.0 + jnp.exp(-x)))


def _rms(x, g):
    return x * lax.rsqrt(jnp.mean(x * x, axis=-1, keepdims=True) + NORM_EPS) * g


def _params(n_axes):
    return pltpu.CompilerParams(dimension_semantics=("arbitrary",) * n_axes,
                                vmem_limit_bytes=VMEM_LIMIT)


def _resident(shape):
    nd = len(shape)
    return pl.BlockSpec(shape, lambda *_: (0,) * nd, pipeline_mode=pl.Buffered(1))


def _mod_kernel(c_ref, w_ref, b_ref, o_ref):
    a = _silu(c_ref[...]).astype(BF16)
    o_ref[...] = jnp.dot(a, w_ref[...].astype(BF16), preferred_element_type=F32) + b_ref[...]


def _modulations(cc, w_ada, b_ada):
    rows = cc.shape[0]
    return pl.pallas_call(
        _mod_kernel,
        grid=(N_MOD,),
        in_specs=[pl.BlockSpec((rows, D_MODEL), lambda j: (0, 0)),
                  pl.BlockSpec((D_MODEL, D_MODEL), lambda j: (0, j)),
                  pl.BlockSpec((1, D_MODEL), lambda j: (0, j))],
        out_specs=pl.BlockSpec((rows, D_MODEL), lambda j: (0, j)),
        out_shape=jax.ShapeDtypeStruct((rows, N_MOD * D_MODEL), F32),
        compiler_params=_params(1),
        name="mod",
    )(cc, w_ada, b_ada)


def _prenorm(x, g, shift, scale):
    return (_rms(x, g) * (1.0 + scale) + shift).astype(BF16)


def _rope(blk, cos, sin_signed, first_half):
    partner = jnp.where(first_half, pltpu.roll(blk, LANES - 32, 1), pltpu.roll(blk, 32, 1))
    return blk * cos + partner * sin_signed


def _in_lat_kernel(x_ref, mod_ref, g_ref, w_ref, cq_ref, sq_ref, ck_ref, sk_ref, *out_refs):
    h = _prenorm(x_ref[0], g_ref[...], mod_ref[0, 0:1, :], mod_ref[0, 1:2, :])
    lane = lax.broadcasted_iota(jnp.int32, (1, LANES), 1)
    first_half = (lane & 32) == 0
    for gi, o_ref in enumerate(out_refs):
        w = w_ref[:, gi * GROUP_W:(gi + 1) * GROUP_W].astype(BF16)
        acc = jnp.dot(h, w, preferred_element_type=F32)
        if gi == 0:
            for hh in range(RET_HEADS):
                blk = _rope(acc[:, hh * LANES:(hh + 1) * LANES], cq_ref[...], sq_ref[...], first_half)
                o_ref[0, :, hh * LANES:(hh + 1) * LANES] = blk.astype(BF16)
        elif gi == 1:
            for hh in range(RET_HEADS):
                blk = _rope(acc[:, hh * LANES:(hh + 1) * LANES], ck_ref[...], sk_ref[...], first_half)
                o_ref[0, hh * LANES:(hh + 1) * LANES, :] = blk.T.astype(BF16)
        elif gi == 4:
            q = (acc * (NA_DIM ** -0.5 * LOG2_E)).astype(BF16)
            zero = jnp.zeros((ROW_TILE, LANES), BF16)
            for pair in range(NA_HEADS // 2):
                qp = q[:, pair * LANES:(pair + 1) * LANES]
                first = jnp.where(lane < NA_DIM, qp, zero)
                second = jnp.where(lane < NA_DIM, zero, qp)
                for rr in range(ROW_TILE // GRID_W):
                    tokens = slice(rr * GRID_W, (rr + 1) * GRID_W)
                    o_ref[0, pair, rr * PAIR_ROWS:rr * PAIR_ROWS + GRID_W, :] = first[tokens]
                    o_ref[0, pair, rr * PAIR_ROWS + GRID_W:(rr + 1) * PAIR_ROWS, :] = second[tokens]
        elif gi in _K_GROUPS:
            o_ref[0] = acc.T.astype(BF16)
        else:
            o_ref[0] = acc.astype(BF16)


_CTX_GROUPS = (1, 2, 5, 6)


def _in_ctx_kernel(x_ref, mod_ref, g_ref, w_ref, *out_refs):
    x = jnp.concatenate([x_ref[i] for i in range(CTX_BATCHES)], axis=0)
    h = _prenorm(x, g_ref[...], mod_ref[0, 0:1, :], mod_ref[0, 1:2, :])
    for gi, o_ref in zip(_CTX_GROUPS, out_refs):
        w = w_ref[:, gi * GROUP_W:(gi + 1) * GROUP_W].astype(BF16)
        acc = jnp.dot(h, w, preferred_element_type=F32)
        acc = (acc.T if gi in _K_GROUPS else acc).astype(BF16)
        for i in range(CTX_BATCHES):
            tokens = slice(i * CTX_LEN, (i + 1) * CTX_LEN)
            o_ref[i] = acc[:, tokens] if gi in _K_GROUPS else acc[tokens, :]


def _rope_tables():
    tok = np.arange(SEQ)
    n_freq = RET_DIM // 4
    inv = ROPE_BASE ** (-np.arange(n_freq, dtype=np.float64) / n_freq)
    ang_r = (tok // GRID_W)[:, None] * inv[None, :]
    ang_c = (tok % GRID_W)[:, None] * inv[None, :]
    cos = np.concatenate([np.cos(ang_r), np.cos(ang_r), np.cos(ang_c), np.cos(ang_c)], axis=-1)
    sin = np.concatenate([-np.sin(ang_r), np.sin(ang_r), -np.sin(ang_c), np.sin(ang_c)], axis=-1)
    qs = RET_DIM ** -0.5
    return [jnp.asarray(t, F32) for t in (cos * qs, sin * qs, cos, sin)]


def _in_proj_lat(x, mods, g_pre, w_in):
    batch = x.shape[0]
    tok = lambda b, i: (b, i, 0)
    tab = pl.BlockSpec((ROW_TILE, LANES), lambda b, i: (i, 0))
    out_shapes = [jax.ShapeDtypeStruct((batch, SEQ, GROUP_W), BF16)] * N_GROUPS
    out_specs = [pl.BlockSpec((1, ROW_TILE, GROUP_W), tok)] * N_GROUPS
    for gi in _K_GROUPS:
        out_shapes[gi] = jax.ShapeDtypeStruct((batch, GROUP_W, SEQ), BF16)
        out_specs[gi] = pl.BlockSpec((1, GROUP_W, ROW_TILE), lambda b, i: (b, 0, i))
    n_pairs = NA_HEADS // 2
    out_shapes[4] = jax.ShapeDtypeStruct((batch, n_pairs, GRID_ROWS * PAIR_ROWS, LANES), BF16)
    out_specs[4] = pl.BlockSpec((1, n_pairs, ROW_TILE // GRID_W * PAIR_ROWS, LANES), lambda b, i: (b, 0, i, 0))
    return pl.pallas_call(
        _in_lat_kernel,
        grid=(batch, SEQ // ROW_TILE),
        in_specs=[pl.BlockSpec((1, ROW_TILE, D_MODEL), tok),
                  pl.BlockSpec((1, N_MOD, D_MODEL), lambda b, i: (b, 0, 0)),
                  pl.BlockSpec((1, D_MODEL), lambda b, i: (0, 0)),
                  _resident((D_MODEL, N_GROUPS * GROUP_W)),
                  tab, tab, tab, tab],
        out_specs=out_specs,
        out_shape=out_shapes,
        compiler_params=_params(2),
        name="in_lat",
    )(x, mods, g_pre, w_in, *_rope_tables())


def _in_proj_ctx(ctx, mods_ctx, g_pre, w_in):
    batch = ctx.shape[0]
    tok = lambda b: (b, 0, 0)
    out_shapes = [jax.ShapeDtypeStruct((batch, GROUP_W, CTX_LEN) if gi in _K_GROUPS else (batch, CTX_LEN, GROUP_W),
                                       BF16) for gi in _CTX_GROUPS]
    out_specs = [pl.BlockSpec((CTX_BATCHES, GROUP_W, CTX_LEN) if gi in _K_GROUPS else (CTX_BATCHES, CTX_LEN, GROUP_W),
                              tok) for gi in _CTX_GROUPS]
    return pl.pallas_call(
        _in_ctx_kernel,
        grid=(batch // CTX_BATCHES,),
        in_specs=[pl.BlockSpec((CTX_BATCHES, CTX_LEN, D_MODEL), tok),
                  pl.BlockSpec((1, N_MOD, D_MODEL), lambda b: (0, 0, 0)),
                  pl.BlockSpec((1, D_MODEL), lambda b: (0, 0)),
                  _resident((D_MODEL, N_GROUPS * GROUP_W))],
        out_specs=out_specs,
        out_shape=out_shapes,
        compiler_params=_params(1),
        name="in_ctx",
    )(ctx, mods_ctx, g_pre, w_in)


def _ret_kernel(lg_ref, q_ref, kt_ref, v_ref, g_ref, kct_ref, vc_ref, gn_ref, o_ref,
                dec_ref, wq_ref, wk_ref, upd_ref, st_ref, sp_ref):
    head = pl.program_id(0)
    lgf = lg_ref[0, head]
    lgb = lg_ref[1, head]
    c = RET_CHUNK
    n_chunks = SEQ // c

    @pl.when(pl.program_id(1) == 0)
    def _tables():
        ii = lax.broadcasted_iota(jnp.int32, (c, c), 0)
        jj = lax.broadcasted_iota(jnp.int32, (c, c), 1)
        diff = (ii - jj).astype(F32)
        dec_ref[...] = jnp.exp(jnp.where(diff >= 0, lgf * diff, -lgb * diff))
        ri = lax.broadcasted_iota(jnp.int32, (c, LANES), 0).astype(F32)
        wq_ref[0] = jnp.exp(lgf * (ri + 1.0))
        wq_ref[1] = jnp.exp(lgb * (c - ri))
        ti = lax.broadcasted_iota(jnp.int32, (1, c), 1).astype(F32)
        wk_ref[0] = jnp.exp(lgf * (c - 1.0 - ti))
        wk_ref[1] = jnp.exp(lgb * ti)

    ones = jnp.ones((1, LANES), F32)
    chunk_f = jnp.exp(ones * (lgf * c))
    chunk_b = jnp.exp(ones * (lgb * c))

    def update(kt, v):
        ktf = kt.astype(F32)
        kw = jnp.concatenate([(ktf * wk_ref[0]).astype(BF16), (ktf * wk_ref[1]).astype(BF16)], axis=0)
        return jnp.dot(kw, v, preferred_element_type=F32)

    def chunk(n):
        return slice(n * c, (n + 1) * c)

    batches = range(RET_BATCHES)
    for b in batches:
        for n in range(n_chunks):
            upd_ref[b, n] = update(kt_ref[b, :, chunk(n)], v_ref[b, chunk(n), :])

    for b in batches:
        ctx_upd = update(kct_ref[b], vc_ref[b])
        state = ctx_upd[:RET_DIM]
        for n in range(n_chunks):
            st_ref[b, n, 0:RET_DIM, :] = state.astype(BF16)
            state = chunk_f * state + upd_ref[b, n, 0:RET_DIM, :]
        state = ctx_upd[RET_DIM:]
        for n in reversed(range(n_chunks)):
            st_ref[b, n, RET_DIM:, :] = state.astype(BF16)
            state = chunk_b * state + upd_ref[b, n, RET_DIM:, :]

    def scores(b, n):
        s = jnp.dot(q_ref[b, chunk(n), :], kt_ref[b, :, chunk(n)], preferred_element_type=F32)
        sp_ref[b, n % 2] = (s * dec_ref[...]).astype(BF16)

    def outputs(b, n):
        rows = chunk(n)
        qf = q_ref[b, rows, :].astype(F32)
        qw = jnp.concatenate([(qf * wq_ref[0]).astype(BF16), (qf * wq_ref[1]).astype(BF16)], axis=1)
        o = (jnp.dot(sp_ref[b, n % 2], v_ref[b, rows, :], preferred_element_type=F32)
             + jnp.dot(qw, st_ref[b, n], preferred_element_type=F32))
        d = o - jnp.mean(o, axis=-1, keepdims=True)
        y = d * lax.rsqrt(jnp.mean(d * d, axis=-1, keepdims=True) + NORM_EPS) * gn_ref[...]
        o_ref[b, rows, :] = (y * _silu(g_ref[b, rows, :].astype(F32))).astype(BF16)

    for b in batches:
        scores(b, 0)
    for n in range(n_chunks):
        for b in batches:
            if n + 1 < n_chunks:
                scores(b, n + 1)
        for b in batches:
            outputs(b, n)


def _retention(log_gammas, rq, rkt, rv, rg, crkt, crv, gn_w):
    assert CTX_LEN == RET_CHUNK
    batch = rq.shape[0]
    n_chunks = SEQ // RET_CHUNK
    nb = RET_BATCHES
    lat = pl.BlockSpec((nb, SEQ, LANES), lambda h, b: (b, 0, h))
    cx = pl.BlockSpec((nb, CTX_LEN, LANES), lambda h, b: (b, 0, h))
    return pl.pallas_call(
        _ret_kernel,
        grid=(RET_HEADS, batch // nb),
        in_specs=[pl.BlockSpec(memory_space=pltpu.SMEM), lat,
                  pl.BlockSpec((nb, LANES, SEQ), lambda h, b: (b, h, 0)), lat, lat,
                  pl.BlockSpec((nb, LANES, CTX_LEN), lambda h, b: (b, h, 0)), cx,
                  pl.BlockSpec((1, LANES), lambda h, b: (0, h))],
        out_specs=lat,
        out_shape=jax.ShapeDtypeStruct((batch, SEQ, RET_WIDTH), BF16),
        scratch_shapes=[pltpu.VMEM((RET_CHUNK, RET_CHUNK), F32),
                        pltpu.VMEM((2, RET_CHUNK, LANES), F32),
                        pltpu.VMEM((2, 1, RET_CHUNK), F32),
                        pltpu.VMEM((nb, n_chunks, 2 * RET_DIM, RET_DIM), F32),
                        pltpu.VMEM((nb, n_chunks, 2 * RET_DIM, RET_DIM), BF16),
                        pltpu.VMEM((nb, 2, RET_CHUNK, RET_CHUNK), BF16)],
        compiler_params=_params(2),
        name="ret",
    )(log_gammas, rq, rkt, rv, rg, crkt, crv, gn_w)


N_DR = 2 * NA_KH - 1
N_DC = 2 * NA_KW - 1
PAIR_ROWS = 2 * GRID_W
NA_GROUP = 2
NA_BATCHES = 2


def _na_build_bias(rpb_ref, bias_ref, pair):
    qi = lax.broadcasted_iota(jnp.int32, (GRID_W, LANES), 0)
    li = lax.broadcasted_iota(jnp.int32, (GRID_W, LANES), 1)
    key_col = li & (GRID_W - 1)
    col_start = jnp.clip(qi - NA_KW // 2, 0, GRID_W - NA_KW)
    valid = (key_col >= col_start) & (key_col < col_start + NA_KW)
    code = jnp.where(valid, key_col - qi + (NA_KW - 1), -1)
    first_row = lax.broadcasted_iota(jnp.int32, (1, LANES), 1) < GRID_W

    def pattern(t, carry):
        for hd in range(2):
            head = 2 * pair + hd
            for blk in range(NA_KH // 2):
                base = (head * N_DR + (2 * blk + NA_KH - 1 - t)) * N_DC
                acc = jnp.full((GRID_W, LANES), NEG_INF, F32)
                for b in range(N_DC):
                    val = jnp.where(first_row, rpb_ref[base + b], rpb_ref[base + N_DC + b]) * LOG2_E
                    acc = jnp.where(code == b, val, acc)
                bias_ref[t, hd * GRID_W:(hd + 1) * GRID_W, blk * LANES:(blk + 1) * LANES] = acc
        return carry

    lax.fori_loop(0, N_PATTERNS, pattern, 0)


def _na_kernel(rpb_ref, q2_ref, kt_ref, v_ref, kct_ref, vc_ref, o_ref,
               bias_ref, kt2_ref, s_ref):
    pair = pl.program_id(0)
    low = lax.broadcasted_iota(jnp.int32, (1, LANES), 1) < NA_DIM
    half = NA_KH // 2

    @pl.when(pl.program_id(1) == 0)
    def _bias():
        _na_build_bias(rpb_ref, bias_ref, pair)

    batches = range(NA_BATCHES)
    for bb in batches:
        kt2_ref[bb, 0] = kt_ref[bb]
        kt2_ref[bb, 1, :, 0:SEQ - LANES] = kt_ref[bb, :, GRID_W:SEQ - GRID_W]
        kt2_ref[bb, 1, :, SEQ - LANES:SEQ] = jnp.zeros((LANES, LANES), BF16)

    def band_start(r):
        return jnp.clip(r - half, 0, GRID_ROWS - NA_KH)

    def scores(bb, group, slot):
        rows = group * (NA_GROUP * PAIR_ROWS)
        group_q = pl.ds(pl.multiple_of(rows, NA_GROUP * PAIR_ROWS), NA_GROUP * PAIR_ROWS)
        s_ref[bb, slot, :, NA_BAND:] = jnp.dot(q2_ref[bb, 0, group_q, :], kct_ref[bb], preferred_element_type=F32)
        for j in range(NA_GROUP):
            r = group * NA_GROUP + j
            start = band_start(r)
            pattern = jnp.where(r < half, r, jnp.where(r > GRID_ROWS - half, r - (GRID_ROWS - NA_KH), half))
            lanes = pl.ds(pl.multiple_of((start >> 1) * LANES, LANES), NA_BAND)
            kb = kt2_ref[bb, start & 1, :, lanes]
            q2 = q2_ref[bb, 0, pl.ds(pl.multiple_of(r * PAIR_ROWS, PAIR_ROWS), PAIR_ROWS), :]
            s_ref[bb, slot, j * PAIR_ROWS:(j + 1) * PAIR_ROWS, 0:NA_BAND] = (
                jnp.dot(q2, kb, preferred_element_type=F32) + bias_ref[pattern])

    n_blocks = (NA_BAND + CTX_LEN) // LANES

    def outputs(bb, group, slot):
        probs = []
        for j in range(NA_GROUP):
            rows = slice(j * PAIR_ROWS, (j + 1) * PAIR_ROWS)

            def block(b):
                return s_ref[bb, slot, rows, b * LANES:(b + 1) * LANES]

            m = block(0)
            for b in range(1, n_blocks):
                m = jnp.maximum(m, block(b))
            m = jnp.max(m, axis=-1, keepdims=True)
            p = [jnp.exp2(block(b) - m) for b in range(n_blocks)]
            den = p[0]
            for pb in p[1:]:
                den = den + pb
            den = jnp.sum(den, axis=-1, keepdims=True)
            probs.append((jnp.concatenate([pb.astype(BF16) for pb in p], axis=1), den))
        ctx = jnp.dot(jnp.concatenate([p[:, NA_BAND:] for p, _ in probs], axis=0), vc_ref[bb],
                      preferred_element_type=F32)
        for j, (p, den) in enumerate(probs):
            r = group * NA_GROUP + j
            vb = v_ref[bb, pl.ds(pl.multiple_of(band_start(r) * GRID_W, GRID_W), NA_BAND), :]
            o = (jnp.dot(p[:, 0:NA_BAND], vb, preferred_element_type=F32)
                 + ctx[j * PAIR_ROWS:(j + 1) * PAIR_ROWS]) * (1.0 / den)
            o_ref[bb, pl.ds(pl.multiple_of(r * GRID_W, GRID_W), GRID_W), :] = (
                jnp.where(low, o[:GRID_W], o[GRID_W:]).astype(BF16))

    n_groups = GRID_ROWS // NA_GROUP
    for bb in batches:
        scores(bb, 0, 0)
    for g in range(n_groups):
        for bb in batches:
            if g + 1 < n_groups:
                scores(bb, g + 1, (g + 1) % 2)
        for bb in batches:
            outputs(bb, g, g % 2)


def _neighbourhood_attention(rpb, nq, nkt, nv, cnkt, cnv):
    batch = nq.shape[0]
    n_q2 = GRID_ROWS * PAIR_ROWS
    nb = NA_BATCHES
    lat = pl.BlockSpec((nb, SEQ, LANES), lambda p, b: (b, 0, p))
    cx = pl.BlockSpec((nb, CTX_LEN, LANES), lambda p, b: (b, 0, p))
    return pl.pallas_call(
        _na_kernel,
        grid=(NA_HEADS // 2, batch // nb),
        in_specs=[pl.BlockSpec(memory_space=pltpu.SMEM),
                  pl.BlockSpec((nb, 1, n_q2, LANES), lambda p, b: (b, p, 0, 0)),
                  pl.BlockSpec((nb, LANES, SEQ), lambda p, b: (b, p, 0)), lat,
                  pl.BlockSpec((nb, LANES, CTX_LEN), lambda p, b: (b, p, 0)), cx],
        out_specs=lat,
        out_shape=jax.ShapeDtypeStruct((batch, SEQ, NA_WIDTH), BF16),
        scratch_shapes=[pltpu.VMEM((N_PATTERNS, PAIR_ROWS, NA_BAND), F32),
                        pltpu.VMEM((nb, 2, LANES, SEQ), BF16),
                        pltpu.VMEM((nb, 2, NA_GROUP * PAIR_ROWS, NA_BAND + CTX_LEN), F32)],
        compiler_params=_params(2),
        name="na",
    )(rpb, nq, nkt, nv, cnkt, cnv)


def _out_kernel(x_ref, ret_ref, na_ref, mod_ref, gpm_ref, gpre_ref, gpost_ref, wo_ref, w1_ref, w2_ref, o_ref):
    gt1, sh2, sc2, gt2 = (mod_ref[0, i:i + 1, :] for i in (2, 3, 4, 5))
    subs = [slice(i * SUB_TILE, (i + 1) * SUB_TILE) for i in range(OUT_TILE // SUB_TILE)]
    mix = [jnp.dot(ret_ref[0, rows, :], wo_ref[0:RET_WIDTH, :], preferred_element_type=F32)
           + jnp.dot(na_ref[0, rows, :], wo_ref[RET_WIDTH:, :], preferred_element_type=F32) for rows in subs]
    x1 = [x_ref[0, rows, :] + gt1 * _rms(m, gpm_ref[...]) for rows, m in zip(subs, mix)]
    h2 = [_prenorm(x, gpre_ref[...], sh2, sc2) for x in x1]
    mlp = []
    for h in h2:
        acc = jnp.zeros((SUB_TILE, D_MODEL), F32)
        for ci in range(D_FF // FF_CHUNK):
            cols = slice(ci * FF_CHUNK, (ci + 1) * FF_CHUNK)
            a = jnp.maximum(jnp.dot(h, w1_ref[:, cols], preferred_element_type=F32), 0.0)
            acc = acc + jnp.dot((a * a).astype(BF16), w2_ref[cols, :], preferred_element_type=F32)
        mlp.append(acc)
    for rows, x, y in zip(subs, x1, mlp):
        o_ref[0, rows, :] = x + gt2 * _rms(y, gpost_ref[...])


def _out_proj_mlp(x, ret_lat, na_lat, mods, g_post_mix, g_pre_mlp, g_post_mlp, w_out, w_mlp1, w_mlp2):
    batch = x.shape[0]
    tok = lambda b, i: (b, i, 0)
    gain = pl.BlockSpec((1, D_MODEL), lambda b, i: (0, 0))
    return pl.pallas_call(
        _out_kernel,
        grid=(batch, SEQ // OUT_TILE),
        in_specs=[pl.BlockSpec((1, OUT_TILE, D_MODEL), tok),
                  pl.BlockSpec((1, OUT_TILE, RET_WIDTH), tok),
                  pl.BlockSpec((1, OUT_TILE, NA_WIDTH), tok),
                  pl.BlockSpec((1, N_MOD, D_MODEL), lambda b, i: (b, 0, 0)),
                  gain, gain, gain,
                  _resident((RET_WIDTH + NA_WIDTH, D_MODEL)),
                  _resident((D_MODEL, D_FF)),
                  _resident((D_FF, D_MODEL))],
        out_specs=pl.BlockSpec((1, OUT_TILE, D_MODEL), tok),
        out_shape=jax.ShapeDtypeStruct(x.shape, F32),
        compiler_params=_params(2),
        name="out_mlp",
    )(x, ret_lat, na_lat, mods, g_post_mix, g_pre_mlp, g_post_mlp, w_out, w_mlp1, w_mlp2)


def kernel(x, c, ctx, c_ctx, w_ada, b_ada, g_pre_mix, g_post_mix, g_pre_mlp, g_post_mlp,
           w_in, ret_decay, ret_gn, na_rpb, w_out, w_mlp1, w_mlp2):
    assert w_in.shape[0] == 1, "single-layer block: the context stream is never updated"
    batch = x.shape[0]
    pad = (-(batch + 1)) % 8
    cc = jnp.concatenate([c, c_ctx[None, :], jnp.zeros((pad, D_MODEL), F32)], axis=0)
    mods = _modulations(cc, w_ada[0], b_ada)
    mods_lat = mods[:batch].reshape(batch, N_MOD, D_MODEL)
    mods_ctx = mods[batch:batch + 1].reshape(1, N_MOD, D_MODEL)

    rq, rk, rv, rg, nq, nkt, nv = _in_proj_lat(x, mods_lat, g_pre_mix, w_in[0])
    crk, crv, cnkt, cnv = _in_proj_ctx(ctx, mods_ctx, g_pre_mix, w_in[0])

    log_gammas = jax.nn.log_sigmoid(ret_decay[0].astype(F32))
    ret_lat = _retention(log_gammas, rq, rk, rv, rg, crk, crv, ret_gn)
    na_lat = _neighbourhood_attention(na_rpb[0].astype(F32).reshape(-1), nq, nkt, nv, cnkt, cnv)

    return _out_proj_mlp(x, ret_lat, na_lat, mods_lat, g_post_mix, g_pre_mlp, g_post_mlp,
                         w_out[0].astype(BF16), w_mlp1[0].astype(BF16), w_mlp2[0].astype(BF16))
```

```python
import numpy as np
import jax
import jax.numpy as jnp
from jax import lax
from jax.experimental import pallas as pl
from jax.experimental.pallas import tpu as pltpu

D_MODEL = 1024
SEQ = 2048
CTX_LEN = 256
GRID_W = 64
GRID_ROWS = SEQ // GRID_W
RET_HEADS = 4
RET_DIM = 128
RET_WIDTH = RET_HEADS * RET_DIM
NA_HEADS = 8
NA_DIM = 64
NA_WIDTH = NA_HEADS * NA_DIM
NA_KH = 8
NA_KW = 16
N_GROUPS = 7
GROUP_W = 512
_K_GROUPS = (1, 5)
D_FF = 4 * D_MODEL
ROPE_BASE = 10000.0
NORM_EPS = 1e-6
N_MOD = 6
NEG_INF = -1e30
LOG2_E = 1.4426950408889634

LANES = 128
ROW_TILE = 1024
CTX_BATCHES = 2
OUT_TILE = 1024
SUB_TILE = 256
FF_CHUNK = 1024
RET_CHUNK = 256
RET_BATCHES = 2
NA_BAND = NA_KH * GRID_W
N_PATTERNS = 8
VMEM_LIMIT = 56 * 1024 * 1024

F32 = jnp.float32
BF16 = jnp.bfloat16
_NT = (((1,), (1,)), ((), ()))
_TN = (((0,), (0,)), ((), ()))


def _silu(x):
    return x * (1.0 / (1.0 + jnp.exp(-x)))


def _rms(x, g):
    return x * lax.rsqrt(jnp.mean(x * x, axis=-1, keepdims=True) + NORM_EPS) * g


def _params(n_axes):
    return pltpu.CompilerParams(dimension_semantics=("arbitrary",) * n_axes,
                                vmem_limit_bytes=VMEM_LIMIT)


def _resident(shape):
    nd = len(shape)
    return pl.BlockSpec(shape, lambda *_: (0,) * nd, pipeline_mode=pl.Buffered(1))


def _mod_kernel(c_ref, w_ref, b_ref, o_ref):
    a = _silu(c_ref[...]).astype(BF16)
    o_ref[...] = jnp.dot(a, w_ref[...].astype(BF16), preferred_element_type=F32) + b_ref[...]


def _modulations(cc, w_ada, b_ada):
    rows = cc.shape[0]
    return pl.pallas_call(
        _mod_kernel,
        grid=(N_MOD,),
        in_specs=[pl.BlockSpec((rows, D_MODEL), lambda j: (0, 0)),
                  pl.BlockSpec((D_MODEL, D_MODEL), lambda j: (0, j)),
                  pl.BlockSpec((1, D_MODEL), lambda j: (0, j))],
        out_specs=pl.BlockSpec((rows, D_MODEL), lambda j: (0, j)),
        out_shape=jax.ShapeDtypeStruct((rows, N_MOD * D_MODEL), F32),
        compiler_params=_params(1),
        name="mod",
    )(cc, w_ada, b_ada)


def _prenorm(x, g, shift, scale):
    return (_rms(x, g) * (1.0 + scale) + shift).astype(BF16)


def _rope(blk, cos, sin_signed, first_half):
    partner = jnp.where(first_half, pltpu.roll(blk, LANES - 32, 1), pltpu.roll(blk, 32, 1))
    return blk * cos + partner * sin_signed


def _in_lat_kernel(x_ref, mod_ref, g_ref, w_ref, cq_ref, sq_ref, ck_ref, sk_ref, *out_refs):
    h = _prenorm(x_ref[0], g_ref[...], mod_ref[0, 0:1, :], mod_ref[0, 1:2, :])
    lane = lax.broadcasted_iota(jnp.int32, (1, LANES), 1)
    first_half = (lane & 32) == 0
    for gi, o_ref in enumerate(out_refs):
        w = w_ref[:, gi * GROUP_W:(gi + 1) * GROUP_W].astype(BF16)
        acc = jnp.dot(h, w, preferred_element_type=F32)
        if gi == 0:
            for hh in range(RET_HEADS):
                blk = _rope(acc[:, hh * LANES:(hh + 1) * LANES], cq_ref[...], sq_ref[...], first_half)
                o_ref[0, :, hh * LANES:(hh + 1) * LANES] = blk.astype(BF16)
        elif gi == 1:
            for hh in range(RET_HEADS):
                blk = _rope(acc[:, hh * LANES:(hh + 1) * LANES], ck_ref[...], sk_ref[...], first_half)
                o_ref[0, hh * LANES:(hh + 1) * LANES, :] = blk.T.astype(BF16)
        elif gi == 4:
            q = (acc * (NA_DIM ** -0.5 * LOG2_E)).astype(BF16)
            zero = jnp.zeros((ROW_TILE, LANES), BF16)
            for pair in range(NA_HEADS // 2):
                qp = q[:, pair * LANES:(pair + 1) * LANES]
                first = jnp.where(lane < NA_DIM, qp, zero)
                second = jnp.where(lane < NA_DIM, zero, qp)
                for rr in range(ROW_TILE // GRID_W):
                    tokens = slice(rr * GRID_W, (rr + 1) * GRID_W)
                    o_ref[0, pair, rr * PAIR_ROWS:rr * PAIR_ROWS + GRID_W, :] = first[tokens]
                    o_ref[0, pair, rr * PAIR_ROWS + GRID_W:(rr + 1) * PAIR_ROWS, :] = second[tokens]
        elif gi in _K_GROUPS:
            o_ref[0] = acc.T.astype(BF16)
        else:
            o_ref[0] = acc.astype(BF16)


_CTX_GROUPS = (1, 2, 5, 6)


def _in_ctx_kernel(x_ref, mod_ref, g_ref, w_ref, *out_refs):
    x = jnp.concatenate([x_ref[i] for i in range(CTX_BATCHES)], axis=0)
    h = _prenorm(x, g_ref[...], mod_ref[0, 0:1, :], mod_ref[0, 1:2, :])
    for gi, o_ref in zip(_CTX_GROUPS, out_refs):
        w = w_ref[:, gi * GROUP_W:(gi + 1) * GROUP_W].astype(BF16)
        acc = jnp.dot(h, w, preferred_element_type=F32)
        acc = (acc.T if gi in _K_GROUPS else acc).astype(BF16)
        for i in range(CTX_BATCHES):
            tokens = slice(i * CTX_LEN, (i + 1) * CTX_LEN)
            o_ref[i] = acc[:, tokens] if gi in _K_GROUPS else acc[tokens, :]


def _rope_tables():
    tok = np.arange(SEQ)
    n_freq = RET_DIM // 4
    inv = ROPE_BASE ** (-np.arange(n_freq, dtype=np.float64) / n_freq)
    ang_r = (tok // GRID_W)[:, None] * inv[None, :]
    ang_c = (tok % GRID_W)[:, None] * inv[None, :]
    cos = np.concatenate([np.cos(ang_r), np.cos(ang_r), np.cos(ang_c), np.cos(ang_c)], axis=-1)
    sin = np.concatenate([-np.sin(ang_r), np.sin(ang_r), -np.sin(ang_c), np.sin(ang_c)], axis=-1)
    qs = RET_DIM ** -0.5
    return [jnp.asarray(t, F32) for t in (cos * qs, sin * qs, cos, sin)]


def _in_proj_lat(x, mods, g_pre, w_in):
    batch = x.shape[0]
    tok = lambda b, i: (b, i, 0)
    tab = pl.BlockSpec((ROW_TILE, LANES), lambda b, i: (i, 0))
    out_shapes = [jax.ShapeDtypeStruct((batch, SEQ, GROUP_W), BF16)] * N_GROUPS
    out_specs = [pl.BlockSpec((1, ROW_TILE, GROUP_W), tok)] * N_GROUPS
    for gi in _K_GROUPS:
        out_shapes[gi] = jax.ShapeDtypeStruct((batch, GROUP_W, SEQ), BF16)
        out_specs[gi] = pl.BlockSpec((1, GROUP_W, ROW_TILE), lambda b, i: (b, 0, i))
    n_pairs = NA_HEADS // 2
    out_shapes[4] = jax.ShapeDtypeStruct((batch, n_pairs, GRID_ROWS * PAIR_ROWS, LANES), BF16)
    out_specs[4] = pl.BlockSpec((1, n_pairs, ROW_TILE // GRID_W * PAIR_ROWS, LANES), lambda b, i: (b, 0, i, 0))
    return pl.pallas_call(
        _in_lat_kernel,
        grid=(batch, SEQ // ROW_TILE),
        in_specs=[pl.BlockSpec((1, ROW_TILE, D_MODEL), tok),
                  pl.BlockSpec((1, N_MOD, D_MODEL), lambda b, i: (b, 0, 0)),
                  pl.BlockSpec((1, D_MODEL), lambda b, i: (0, 0)),
                  _resident((D_MODEL, N_GROUPS * GROUP_W)),
                  tab, tab, tab, tab],
        out_specs=out_specs,
        out_shape=out_shapes,
        compiler_params=_params(2),
        name="in_lat",
    )(x, mods, g_pre, w_in, *_rope_tables())


def _in_proj_ctx(ctx, mods_ctx, g_pre, w_in):
    batch = ctx.shape[0]
    tok = lambda b: (b, 0, 0)
    out_shapes = [jax.ShapeDtypeStruct((batch, GROUP_W, CTX_LEN) if gi in _K_GROUPS else (batch, CTX_LEN, GROUP_W),
                                       BF16) for gi in _CTX_GROUPS]
    out_specs = [pl.BlockSpec((CTX_BATCHES, GROUP_W, CTX_LEN) if gi in _K_GROUPS else (CTX_BATCHES, CTX_LEN, GROUP_W),
                              tok) for gi in _CTX_GROUPS]
    return pl.pallas_call(
        _in_ctx_kernel,
        grid=(batch // CTX_BATCHES,),
        in_specs=[pl.BlockSpec((CTX_BATCHES, CTX_LEN, D_MODEL), tok),
                  pl.BlockSpec((1, N_MOD, D_MODEL), lambda b: (0, 0, 0)),
                  pl.BlockSpec((1, D_MODEL), lambda b: (0, 0)),
                  _resident((D_MODEL, N_GROUPS * GROUP_W))],
        out_specs=out_specs,
        out_shape=out_shapes,
        compiler_params=_params(1),
        name="in_ctx",
    )(ctx, mods_ctx, g_pre, w_in)


def _ret_kernel(lg_ref, q_ref, kt_ref, v_ref, g_ref, kct_ref, vc_ref, gn_ref, o_ref,
                dec_ref, wq_ref, wk_ref, upd_ref, st_ref, sp_ref):
    head = pl.program_id(0)
    lgf = lg_ref[0, head]
    lgb = lg_ref[1, head]
    c = RET_CHUNK
    n_chunks = SEQ // c

    @pl.when(pl.program_id(1) == 0)
    def _tables():
        ii = lax.broadcasted_iota(jnp.int32, (c, c), 0)
        jj = lax.broadcasted_iota(jnp.int32, (c, c), 1)
        diff = (ii - jj).astype(F32)
        dec_ref[...] = jnp.exp(jnp.where(diff >= 0, lgf * diff, -lgb * diff))
        ri = lax.broadcasted_iota(jnp.int32, (c, LANES), 0).astype(F32)
        wq_ref[0] = jnp.exp(lgf * (ri + 1.0))
        wq_ref[1] = jnp.exp(lgb * (c - ri))
        ti = lax.broadcasted_iota(jnp.int32, (1, c), 1).astype(F32)
        wk_ref[0] = jnp.exp(lgf * (c - 1.0 - ti))
        wk_ref[1] = jnp.exp(lgb * ti)

    ones = jnp.ones((1, LANES), F32)
    chunk_f = jnp.exp(ones * (lgf * c))
    chunk_b = jnp.exp(ones * (lgb * c))

    def update(kt, v):
        ktf = kt.astype(F32)
        kw = jnp.concatenate([(ktf * wk_ref[0]).astype(BF16), (ktf * wk_ref[1]).astype(BF16)], axis=0)
        return jnp.dot(kw, v, preferred_element_type=F32)

    def chunk(n):
        return slice(n * c, (n + 1) * c)

    batches = range(RET_BATCHES)
    for b in batches:
        for n in range(n_chunks):
            upd_ref[b, n] = update(kt_ref[b, :, chunk(n)], v_ref[b, chunk(n), :])

    for b in batches:
        ctx_upd = update(kct_ref[b], vc_ref[b])
        state = ctx_upd[:RET_DIM]
        for n in range(n_chunks):
            st_ref[b, n, 0:RET_DIM, :] = state.astype(BF16)
            state = chunk_f * state + upd_ref[b, n, 0:RET_DIM, :]
        state = ctx_upd[RET_DIM:]
        for n in reversed(range(n_chunks)):
            st_ref[b, n, RET_DIM:, :] = state.astype(BF16)
            state = chunk_b * state + upd_ref[b, n, RET_DIM:, :]

    def scores(b, n):
        s = jnp.dot(q_ref[b, chunk(n), :], kt_ref[b, :, chunk(n)], preferred_element_type=F32)
        sp_ref[b, n % 2] = (s * dec_ref[...]).astype(BF16)

    def outputs(b, n):
        rows = chunk(n)
        qf = q_ref[b, rows, :].astype(F32)
        qw = jnp.concatenate([(qf * wq_ref[0]).astype(BF16), (qf * wq_ref[1]).astype(BF16)], axis=1)
        o = (jnp.dot(sp_ref[b, n % 2], v_ref[b, rows, :], preferred_element_type=F32)
             + jnp.dot(qw, st_ref[b, n], preferred_element_type=F32))
        d = o - jnp.mean(o, axis=-1, keepdims=True)
        y = d * lax.rsqrt(jnp.mean(d * d, axis=-1, keepdims=True) + NORM_EPS) * gn_ref[...]
        o_ref[b, rows, :] = (y * _silu(g_ref[b, rows, :].astype(F32))).astype(BF16)

    for b in batches:
        scores(b, 0)
    for n in range(n_chunks):
        for b in batches:
            if n + 1 < n_chunks:
                scores(b, n + 1)
        for b in batches:
            outputs(b, n)


def _retention(log_gammas, rq, rkt, rv, rg, crkt, crv, gn_w):
    assert CTX_LEN == RET_CHUNK
    batch = rq.shape[0]
    n_chunks = SEQ // RET_CHUNK
    nb = RET_BATCHES
    lat = pl.BlockSpec((nb, SEQ, LANES), lambda h, b: (b, 0, h))
    cx = pl.BlockSpec((nb, CTX_LEN, LANES), lambda h, b: (b, 0, h))
    return pl.pallas_call(
        _ret_kernel,
        grid=(RET_HEADS, batch // nb),
        in_specs=[pl.BlockSpec(memory_space=pltpu.SMEM), lat,
                  pl.BlockSpec((nb, LANES, SEQ), lambda h, b: (b, h, 0)), lat, lat,
                  pl.BlockSpec((nb, LANES, CTX_LEN), lambda h, b: (b, h, 0)), cx,
                  pl.BlockSpec((1, LANES), lambda h, b: (0, h))],
        out_specs=lat,
        out_shape=jax.ShapeDtypeStruct((batch, SEQ, RET_WIDTH), BF16),
        scratch_shapes=[pltpu.VMEM((RET_CHUNK, RET_CHUNK), F32),
                        pltpu.VMEM((2, RET_CHUNK, LANES), F32),
                        pltpu.VMEM((2, 1, RET_CHUNK), F32),
                        pltpu.VMEM((nb, n_chunks, 2 * RET_DIM, RET_DIM), F32),
                        pltpu.VMEM((nb, n_chunks, 2 * RET_DIM, RET_DIM), BF16),
                        pltpu.VMEM((nb, 2, RET_CHUNK, RET_CHUNK), BF16)],
        compiler_params=_params(2),
        name="ret",
    )(log_gammas, rq, rkt, rv, rg, crkt, crv, gn_w)


N_DR = 2 * NA_KH - 1
N_DC = 2 * NA_KW - 1
PAIR_ROWS = 2 * GRID_W
NA_GROUP = 2
NA_BATCHES = 2


def _na_build_bias(rpb_ref, bias_ref, pair):
    qi = lax.broadcasted_iota(jnp.int32, (GRID_W, LANES), 0)
    li = lax.broadcasted_iota(jnp.int32, (GRID_W, LANES), 1)
    key_col = li & (GRID_W - 1)
    col_start = jnp.clip(qi - NA_KW // 2, 0, GRID_W - NA_KW)
    valid = (key_col >= col_start) & (key_col < col_start + NA_KW)
    code = jnp.where(valid, key_col - qi + (NA_KW - 1), -1)
    first_row = lax.broadcasted_iota(jnp.int32, (1, LANES), 1) < GRID_W

    def pattern(t, carry):
        for hd in range(2):
            head = 2 * pair + hd
            for blk in range(NA_KH // 2):
                base = (head * N_DR + (2 * blk + NA_KH - 1 - t)) * N_DC
                acc = jnp.full((GRID_W, LANES), NEG_INF, F32)
                for b in range(N_DC):
                    val = jnp.where(first_row, rpb_ref[base + b], rpb_ref[base + N_DC + b]) * LOG2_E
                    acc = jnp.where(code == b, val, acc)
                bias_ref[t, hd * GRID_W:(hd + 1) * GRID_W, blk * LANES:(blk + 1) * LANES] = acc
        return carry

    lax.fori_loop(0, N_PATTERNS, pattern, 0)


def _na_kernel(rpb_ref, q2_ref, kt_ref, v_ref, kct_ref, vc_ref, o_ref,
               bias_ref, kt2_ref, s_ref):
    pair = pl.program_id(0)
    low = lax.broadcasted_iota(jnp.int32, (1, LANES), 1) < NA_DIM
    half = NA_KH // 2

    @pl.when(pl.program_id(1) == 0)
    def _bias():
        _na_build_bias(rpb_ref, bias_ref, pair)

    batches = range(NA_BATCHES)
    for bb in batches:
        kt2_ref[bb, 0] = kt_ref[bb]
        kt2_ref[bb, 1, :, 0:SEQ - LANES] = kt_ref[bb, :, GRID_W:SEQ - GRID_W]
        kt2_ref[bb, 1, :, SEQ - LANES:SEQ] = jnp.zeros((LANES, LANES), BF16)

    def band_start(r):
        return jnp.clip(r - half, 0, GRID_ROWS - NA_KH)

    def scores(bb, group, slot):
        rows = group * (NA_GROUP * PAIR_ROWS)
        group_q = pl.ds(pl.multiple_of(rows, NA_GROUP * PAIR_ROWS), NA_GROUP * PAIR_ROWS)
        s_ref[bb, slot, :, NA_BAND:] = jnp.dot(q2_ref[bb, 0, group_q, :], kct_ref[bb], preferred_element_type=F32)
        for j in range(NA_GROUP):
            r = group * NA_GROUP + j
            start = band_start(r)
            pattern = jnp.where(r < half, r, jnp.where(r > GRID_ROWS - half, r - (GRID_ROWS - NA_KH), half))
            lanes = pl.ds(pl.multiple_of((start >> 1) * LANES, LANES), NA_BAND)
            kb = kt2_ref[bb, start & 1, :, lanes]
            q2 = q2_ref[bb, 0, pl.ds(pl.multiple_of(r * PAIR_ROWS, PAIR_ROWS), PAIR_ROWS), :]
            s_ref[bb, slot, j * PAIR_ROWS:(j + 1) * PAIR_ROWS, 0:NA_BAND] = (
                jnp.dot(q2, kb, preferred_element_type=F32) + bias_ref[pattern])

    n_blocks = (NA_BAND + CTX_LEN) // LANES

    def outputs(bb, group, slot):
        probs = []
        for j in range(NA_GROUP):
            rows = slice(j * PAIR_ROWS, (j + 1) * PAIR_ROWS)

            def block(b):
                return s_ref[bb, slot, rows, b * LANES:(b + 1) * LANES]

            m = block(0)
            for b in range(1, n_blocks):
                m = jnp.maximum(m, block(b))
            m = jnp.max(m, axis=-1, keepdims=True)
            p = [jnp.exp2(block(b) - m) for b in range(n_blocks)]
            den = p[0]
            for pb in p[1:]:
                den = den + pb
            den = jnp.sum(den, axis=-1, keepdims=True)
            probs.append((jnp.concatenate([pb.astype(BF16) for pb in p], axis=1), den))
        ctx = jnp.dot(jnp.concatenate([p[:, NA_BAND:] for p, _ in probs], axis=0), vc_ref[bb],
                      preferred_element_type=F32)
        for j, (p, den) in enumerate(probs):
            r = group * NA_GROUP + j
            vb = v_ref[bb, pl.ds(pl.multiple_of(band_start(r) * GRID_W, GRID_W), NA_BAND), :]
            o = (jnp.dot(p[:, 0:NA_BAND], vb, preferred_element_type=F32)
                 + ctx[j * PAIR_ROWS:(j + 1) * PAIR_ROWS]) * (1.0 / den)
            o_ref[bb, pl.ds(pl.multiple_of(r * GRID_W, GRID_W), GRID_W), :] = (
                jnp.where(low, o[:GRID_W], o[GRID_W:]).astype(BF16))

    n_groups = GRID_ROWS // NA_GROUP
    for bb in batches:
        scores(bb, 0, 0)
    for g in range(n_groups):
        for bb in batches:
            if g + 1 < n_groups:
                scores(bb, g + 1, (g + 1) % 2)
        for bb in batches:
            outputs(bb, g, g % 2)


def _neighbourhood_attention(rpb, nq, nkt, nv, cnkt, cnv):
    batch = nq.shape[0]
    n_q2 = GRID_ROWS * PAIR_ROWS
    nb = NA_BATCHES
    lat = pl.BlockSpec((nb, SEQ, LANES), lambda p, b: (b, 0, p))
    cx = pl.BlockSpec((nb, CTX_LEN, LANES), lambda p, b: (b, 0, p))
    return pl.pallas_call(
        _na_kernel,
        grid=(NA_HEADS // 2, batch // nb),
        in_specs=[pl.BlockSpec(memory_space=pltpu.SMEM),
                  pl.BlockSpec((nb, 1, n_q2, LANES), lambda p, b: (b, p, 0, 0)),
                  pl.BlockSpec((nb, LANES, SEQ), lambda p, b: (b, p, 0)), lat,
                  pl.BlockSpec((nb, LANES, CTX_LEN), lambda p, b: (b, p, 0)), cx],
        out_specs=lat,
        out_shape=jax.ShapeDtypeStruct((batch, SEQ, NA_WIDTH), BF16),
        scratch_shapes=[pltpu.VMEM((N_PATTERNS, PAIR_ROWS, NA_BAND), F32),
                        pltpu.VMEM((nb, 2, LANES, SEQ), BF16),
                        pltpu.VMEM((nb, 2, NA_GROUP * PAIR_ROWS, NA_BAND + CTX_LEN), F32)],
        compiler_params=_params(2),
        name="na",
    )(rpb, nq, nkt, nv, cnkt, cnv)


def _out_kernel(x_ref, ret_ref, na_ref, mod_ref, gpm_ref, gpre_ref, gpost_ref, wo_ref, w1_ref, w2_ref, o_ref):
    gt1, sh2, sc2, gt2 = (mod_ref[0, i:i + 1, :] for i in (2, 3, 4, 5))
    subs = [slice(i * SUB_TILE, (i + 1) * SUB_TILE) for i in range(OUT_TILE // SUB_TILE)]
    mix = [jnp.dot(ret_ref[0, rows, :], wo_ref[0:RET_WIDTH, :], preferred_element_type=F32)
           + jnp.dot(na_ref[0, rows, :], wo_ref[RET_WIDTH:, :], preferred_element_type=F32) for rows in subs]
    h2 = []
    for rows, m in zip(subs, mix):
        x1 = x_ref[0, rows, :] + gt1 * _rms(m, gpm_ref[...])
        o_ref[0, rows, :] = x1
        h2.append(_prenorm(x1, gpre_ref[...], sh2, sc2))
    for rows, h in zip(subs, h2):
        acc = jnp.zeros((SUB_TILE, D_MODEL), F32)
        for ci in range(D_FF // FF_CHUNK):
            cols = slice(ci * FF_CHUNK, (ci + 1) * FF_CHUNK)
            a = jnp.maximum(jnp.dot(h, w1_ref[:, cols], preferred_element_type=F32), 0.0)
            acc = acc + jnp.dot((a * a).astype(BF16), w2_ref[cols, :], preferred_element_type=F32)
        o_ref[0, rows, :] = o_ref[0, rows, :] + gt2 * _rms(acc, gpost_ref[...])


def _out_proj_mlp(x, ret_lat, na_lat, mods, g_post_mix, g_pre_mlp, g_post_mlp, w_out, w_mlp1, w_mlp2):
    batch = x.shape[0]
    tok = lambda b, i: (b, i, 0)
    gain = pl.BlockSpec((1, D_MODEL), lambda b, i: (0, 0))
    return pl.pallas_call(
        _out_kernel,
        grid=(batch, SEQ // OUT_TILE),
        in_specs=[pl.BlockSpec((1, OUT_TILE, D_MODEL), tok),
                  pl.BlockSpec((1, OUT_TILE, RET_WIDTH), tok),
                  pl.BlockSpec((1, OUT_TILE, NA_WIDTH), tok),
                  pl.BlockSpec((1, N_MOD, D_MODEL), lambda b, i: (b, 0, 0)),
                  gain, gain, gain,
                  _resident((RET_WIDTH + NA_WIDTH, D_MODEL)),
                  _resident((D_MODEL, D_FF)),
                  _resident((D_FF, D_MODEL))],
        out_specs=pl.BlockSpec((1, OUT_TILE, D_MODEL), tok),
        out_shape=jax.ShapeDtypeStruct(x.shape, F32),
        compiler_params=_params(2),
        name="out_mlp",
    )(x, ret_lat, na_lat, mods, g_post_mix, g_pre_mlp, g_post_mlp, w_out, w_mlp1, w_mlp2)


def kernel(x, c, ctx, c_ctx, w_ada, b_ada, g_pre_mix, g_post_mix, g_pre_mlp, g_post_mlp,
           w_in, ret_decay, ret_gn, na_rpb, w_out, w_mlp1, w_mlp2):
    assert w_in.shape[0] == 1, "single-layer block: the context stream is never updated"
    batch = x.shape[0]
    pad = (-(batch + 1)) % 8
    cc = jnp.concatenate([c, c_ctx[None, :], jnp.zeros((pad, D_MODEL), F32)], axis=0)
    mods = _modulations(cc, w_ada[0], b_ada)
    mods_lat = mods[:batch].reshape(batch, N_MOD, D_MODEL)
    mods_ctx = mods[batch:batch + 1].reshape(1, N_MOD, D_MODEL)

    rq, rk, rv, rg, nq, nkt, nv = _in_proj_lat(x, mods_lat, g_pre_mix, w_in[0])
    crk, crv, cnkt, cnv = _in_proj_ctx(ctx, mods_ctx, g_pre_mix, w_in[0])

    log_gammas = jax.nn.log_sigmoid(ret_decay[0].astype(F32))
    ret_lat = _retention(log_gammas, rq, rk, rv, rg, crk, crv, ret_gn)
    na_lat = _neighbourhood_attention(na_rpb[0].astype(F32).reshape(-1), nq, nkt, nv, cnkt, cnv)

    return _out_proj_mlp(x, ret_lat, na_lat, mods_lat, g_post_mix, g_pre_mlp, g_post_mlp,
                         w_out[0].astype(BF16), w_mlp1[0].astype(BF16), w_mlp2[0].astype(BF16))
```

```python
import numpy as np
import jax
import jax.numpy as jnp
from jax import lax
from jax.experimental import pallas as pl
from jax.experimental.pallas import tpu as pltpu

D_MODEL = 1024
SEQ = 2048
CTX_LEN = 256
GRID_W = 64
GRID_ROWS = SEQ // GRID_W
RET_HEADS = 4
RET_DIM = 128
RET_WIDTH = RET_HEADS * RET_DIM
NA_HEADS = 8
NA_DIM = 64
NA_WIDTH = NA_HEADS * NA_DIM
NA_KH = 8
NA_KW = 16
N_GROUPS = 7
GROUP_W = 512
_K_GROUPS = (1, 5)
D_FF = 4 * D_MODEL
ROPE_BASE = 10000.0
NORM_EPS = 1e-6
N_MOD = 6
NEG_INF = -1e30
LOG2_E = 1.4426950408889634

LANES = 128
ROW_TILE = 1024
CTX_BATCHES = 2
OUT_TILE = 1024
SUB_TILE = 256
FF_CHUNK = 1024
RET_CHUNK = 256
RET_BATCHES = 4
NA_BAND = NA_KH * GRID_W
N_PATTERNS = 8
VMEM_LIMIT = 56 * 1024 * 1024

F32 = jnp.float32
BF16 = jnp.bfloat16


def _silu(x):
    return x * (1.0 / (1.0 + jnp.exp(-x)))


def _rms(x, g):
    return x * lax.rsqrt(jnp.mean(x * x, axis=-1, keepdims=True) + NORM_EPS) * g


def _params(n_axes):
    return pltpu.CompilerParams(dimension_semantics=("arbitrary",) * n_axes,
                                vmem_limit_bytes=VMEM_LIMIT)


def _resident(shape):
    nd = len(shape)
    return pl.BlockSpec(shape, lambda *_: (0,) * nd, pipeline_mode=pl.Buffered(1))


def _mod_kernel(c_ref, w_ref, b_ref, o_ref):
    a = _silu(c_ref[...]).astype(BF16)
    o_ref[...] = jnp.dot(a, w_ref[...].astype(BF16), preferred_element_type=F32) + b_ref[...]


def _modulations(cc, w_ada, b_ada):
    rows = cc.shape[0]
    return pl.pallas_call(
        _mod_kernel,
        grid=(N_MOD,),
        in_specs=[pl.BlockSpec((rows, D_MODEL), lambda j: (0, 0)),
                  pl.BlockSpec((D_MODEL, D_MODEL), lambda j: (0, j)),
                  pl.BlockSpec((1, D_MODEL), lambda j: (0, j))],
        out_specs=pl.BlockSpec((rows, D_MODEL), lambda j: (0, j)),
        out_shape=jax.ShapeDtypeStruct((rows, N_MOD * D_MODEL), F32),
        compiler_params=_params(1),
        name="mod",
    )(cc, w_ada, b_ada)


def _prenorm(x, g, shift, scale):
    return (_rms(x, g) * (1.0 + scale) + shift).astype(BF16)


def _rope(blk, cos, sin_signed, first_half):
    partner = jnp.where(first_half, pltpu.roll(blk, LANES - 32, 1), pltpu.roll(blk, 32, 1))
    return blk * cos + partner * sin_signed


def _in_lat_kernel(x_ref, mod_ref, g_ref, w_ref, cq_ref, sq_ref, ck_ref, sk_ref, *out_refs):
    h = _prenorm(x_ref[0], g_ref[...], mod_ref[0, 0:1, :], mod_ref[0, 1:2, :])
    lane = lax.broadcasted_iota(jnp.int32, (1, LANES), 1)
    first_half = (lane & 32) == 0
    for gi, o_ref in enumerate(out_refs):
        w = w_ref[:, gi * GROUP_W:(gi + 1) * GROUP_W].astype(BF16)
        acc = jnp.dot(h, w, preferred_element_type=F32)
        if gi == 0:
            for hh in range(RET_HEADS):
                blk = _rope(acc[:, hh * LANES:(hh + 1) * LANES], cq_ref[...], sq_ref[...], first_half)
                o_ref[0, :, hh * LANES:(hh + 1) * LANES] = blk.astype(BF16)
        elif gi == 1:
            for hh in range(RET_HEADS):
                blk = _rope(acc[:, hh * LANES:(hh + 1) * LANES], ck_ref[...], sk_ref[...], first_half)
                o_ref[0, hh * LANES:(hh + 1) * LANES, :] = blk.T.astype(BF16)
        elif gi == 4:
            q = (acc * (NA_DIM ** -0.5 * LOG2_E)).astype(BF16)
            zero = jnp.zeros((ROW_TILE, LANES), BF16)
            for pair in range(NA_HEADS // 2):
                qp = q[:, pair * LANES:(pair + 1) * LANES]
                first = jnp.where(lane < NA_DIM, qp, zero)
                second = jnp.where(lane < NA_DIM, zero, qp)
                for rr in range(ROW_TILE // GRID_W):
                    tokens = slice(rr * GRID_W, (rr + 1) * GRID_W)
                    o_ref[0, pair, rr * PAIR_ROWS:rr * PAIR_ROWS + GRID_W, :] = first[tokens]
                    o_ref[0, pair, rr * PAIR_ROWS + GRID_W:(rr + 1) * PAIR_ROWS, :] = second[tokens]
        elif gi in _K_GROUPS:
            o_ref[0] = acc.T.astype(BF16)
        else:
            o_ref[0] = acc.astype(BF16)


_CTX_GROUPS = (1, 2, 5, 6)


def _in_ctx_kernel(x_ref, mod_ref, g_ref, w_ref, *out_refs):
    x = jnp.concatenate([x_ref[i] for i in range(CTX_BATCHES)], axis=0)
    h = _prenorm(x, g_ref[...], mod_ref[0, 0:1, :], mod_ref[0, 1:2, :])
    for gi, o_ref in zip(_CTX_GROUPS, out_refs):
        w = w_ref[:, gi * GROUP_W:(gi + 1) * GROUP_W].astype(BF16)
        acc = jnp.dot(h, w, preferred_element_type=F32)
        acc = (acc.T if gi in _K_GROUPS else acc).astype(BF16)
        for i in range(CTX_BATCHES):
            tokens = slice(i * CTX_LEN, (i + 1) * CTX_LEN)
            o_ref[i] = acc[:, tokens] if gi in _K_GROUPS else acc[tokens, :]


def _rope_tables():
    tok = np.arange(SEQ)
    n_freq = RET_DIM // 4
    inv = ROPE_BASE ** (-np.arange(n_freq, dtype=np.float64) / n_freq)
    ang_r = (tok // GRID_W)[:, None] * inv[None, :]
    ang_c = (tok % GRID_W)[:, None] * inv[None, :]
    cos = np.concatenate([np.cos(ang_r), np.cos(ang_r), np.cos(ang_c), np.cos(ang_c)], axis=-1)
    sin = np.concatenate([-np.sin(ang_r), np.sin(ang_r), -np.sin(ang_c), np.sin(ang_c)], axis=-1)
    qs = RET_DIM ** -0.5
    return [jnp.asarray(t, F32) for t in (cos * qs, sin * qs, cos, sin)]


def _in_proj_lat(x, mods, g_pre, w_in):
    batch = x.shape[0]
    tok = lambda b, i: (b, i, 0)
    tab = pl.BlockSpec((ROW_TILE, LANES), lambda b, i: (i, 0))
    out_shapes = [jax.ShapeDtypeStruct((batch, SEQ, GROUP_W), BF16)] * N_GROUPS
    out_specs = [pl.BlockSpec((1, ROW_TILE, GROUP_W), tok)] * N_GROUPS
    for gi in _K_GROUPS:
        out_shapes[gi] = jax.ShapeDtypeStruct((batch, GROUP_W, SEQ), BF16)
        out_specs[gi] = pl.BlockSpec((1, GROUP_W, ROW_TILE), lambda b, i: (b, 0, i))
    n_pairs = NA_HEADS // 2
    out_shapes[4] = jax.ShapeDtypeStruct((batch, n_pairs, GRID_ROWS * PAIR_ROWS, LANES), BF16)
    out_specs[4] = pl.BlockSpec((1, n_pairs, ROW_TILE // GRID_W * PAIR_ROWS, LANES), lambda b, i: (b, 0, i, 0))
    return pl.pallas_call(
        _in_lat_kernel,
        grid=(batch, SEQ // ROW_TILE),
        in_specs=[pl.BlockSpec((1, ROW_TILE, D_MODEL), tok),
                  pl.BlockSpec((1, N_MOD, D_MODEL), lambda b, i: (b, 0, 0)),
                  pl.BlockSpec((1, D_MODEL), lambda b, i: (0, 0)),
                  _resident((D_MODEL, N_GROUPS * GROUP_W)),
                  tab, tab, tab, tab],
        out_specs=out_specs,
        out_shape=out_shapes,
        compiler_params=_params(2),
        name="in_lat",
    )(x, mods, g_pre, w_in, *_rope_tables())


def _in_proj_ctx(ctx, mods_ctx, g_pre, w_in):
    batch = ctx.shape[0]
    tok = lambda b: (b, 0, 0)
    out_shapes = [jax.ShapeDtypeStruct((batch, GROUP_W, CTX_LEN) if gi in _K_GROUPS else (batch, CTX_LEN, GROUP_W),
                                       BF16) for gi in _CTX_GROUPS]
    out_specs = [pl.BlockSpec((CTX_BATCHES, GROUP_W, CTX_LEN) if gi in _K_GROUPS else (CTX_BATCHES, CTX_LEN, GROUP_W),
                              tok) for gi in _CTX_GROUPS]
    return pl.pallas_call(
        _in_ctx_kernel,
        grid=(batch // CTX_BATCHES,),
        in_specs=[pl.BlockSpec((CTX_BATCHES, CTX_LEN, D_MODEL), tok),
                  pl.BlockSpec((1, N_MOD, D_MODEL), lambda b: (0, 0, 0)),
                  pl.BlockSpec((1, D_MODEL), lambda b: (0, 0)),
                  _resident((D_MODEL, N_GROUPS * GROUP_W))],
        out_specs=out_specs,
        out_shape=out_shapes,
        compiler_params=_params(1),
        name="in_ctx",
    )(ctx, mods_ctx, g_pre, w_in)


def _ret_kernel(lg_ref, q_ref, kt_ref, v_ref, g_ref, kct_ref, vc_ref, gn_ref, o_ref,
                dec_ref, wq_ref, wk_ref, upd_ref, st_ref, sp_ref):
    head = pl.program_id(0)
    lgf = lg_ref[0, head]
    lgb = lg_ref[1, head]
    c = RET_CHUNK
    n_chunks = SEQ // c

    @pl.when(pl.program_id(1) == 0)
    def _tables():
        ii = lax.broadcasted_iota(jnp.int32, (c, c), 0)
        jj = lax.broadcasted_iota(jnp.int32, (c, c), 1)
        diff = (ii - jj).astype(F32)
        dec_ref[...] = jnp.exp(jnp.where(diff >= 0, lgf * diff, -lgb * diff))
        ri = lax.broadcasted_iota(jnp.int32, (c, LANES), 0).astype(F32)
        wq_ref[0] = jnp.exp(lgf * (ri + 1.0))
        wq_ref[1] = jnp.exp(lgb * (c - ri))
        ti = lax.broadcasted_iota(jnp.int32, (1, c), 1).astype(F32)
        wk_ref[0] = jnp.exp(lgf * (c - 1.0 - ti))
        wk_ref[1] = jnp.exp(lgb * ti)

    ones = jnp.ones((1, LANES), F32)
    chunk_f = jnp.exp(ones * (lgf * c))
    chunk_b = jnp.exp(ones * (lgb * c))

    def update(kt, v):
        ktf = kt.astype(F32)
        kw = jnp.concatenate([(ktf * wk_ref[0]).astype(BF16), (ktf * wk_ref[1]).astype(BF16)], axis=0)
        return jnp.dot(kw, v, preferred_element_type=F32)

    def chunk(n):
        return slice(n * c, (n + 1) * c)

    batches = range(RET_BATCHES)
    for b in batches:
        for n in range(n_chunks):
            upd_ref[b, n] = update(kt_ref[b, :, chunk(n)], v_ref[b, chunk(n), :])

    for b in batches:
        ctx_upd = update(kct_ref[b], vc_ref[b])
        state = ctx_upd[:RET_DIM]
        for n in range(n_chunks):
            st_ref[b, n, 0:RET_DIM, :] = state.astype(BF16)
            state = chunk_f * state + upd_ref[b, n, 0:RET_DIM, :]
        state = ctx_upd[RET_DIM:]
        for n in reversed(range(n_chunks)):
            st_ref[b, n, RET_DIM:, :] = state.astype(BF16)
            state = chunk_b * state + upd_ref[b, n, RET_DIM:, :]

    def scores(b, n):
        s = jnp.dot(q_ref[b, chunk(n), :], kt_ref[b, :, chunk(n)], preferred_element_type=F32)
        sp_ref[b, n % 2] = (s * dec_ref[...]).astype(BF16)

    def outputs(b, n):
        rows = chunk(n)
        qf = q_ref[b, rows, :].astype(F32)
        qw = jnp.concatenate([(qf * wq_ref[0]).astype(BF16), (qf * wq_ref[1]).astype(BF16)], axis=1)
        o = (jnp.dot(sp_ref[b, n % 2], v_ref[b, rows, :], preferred_element_type=F32)
             + jnp.dot(qw, st_ref[b, n], preferred_element_type=F32))
        d = o - jnp.mean(o, axis=-1, keepdims=True)
        y = d * lax.rsqrt(jnp.mean(d * d, axis=-1, keepdims=True) + NORM_EPS) * gn_ref[...]
        o_ref[b, rows, :] = (y * _silu(g_ref[b, rows, :].astype(F32))).astype(BF16)

    for b in batches:
        scores(b, 0)
    for n in range(n_chunks):
        for b in batches:
            if n + 1 < n_chunks:
                scores(b, n + 1)
        for b in batches:
            outputs(b, n)


def _retention(log_gammas, rq, rkt, rv, rg, crkt, crv, gn_w):
    assert CTX_LEN == RET_CHUNK
    batch = rq.shape[0]
    n_chunks = SEQ // RET_CHUNK
    nb = RET_BATCHES
    lat = pl.BlockSpec((nb, SEQ, LANES), lambda h, b: (b, 0, h))
    cx = pl.BlockSpec((nb, CTX_LEN, LANES), lambda h, b: (b, 0, h))
    return pl.pallas_call(
        _ret_kernel,
        grid=(RET_HEADS, batch // nb),
        in_specs=[pl.BlockSpec(memory_space=pltpu.SMEM), lat,
                  pl.BlockSpec((nb, LANES, SEQ), lambda h, b: (b, h, 0)), lat, lat,
                  pl.BlockSpec((nb, LANES, CTX_LEN), lambda h, b: (b, h, 0)), cx,
                  pl.BlockSpec((1, LANES), lambda h, b: (0, h))],
        out_specs=lat,
        out_shape=jax.ShapeDtypeStruct((batch, SEQ, RET_WIDTH), BF16),
        scratch_shapes=[pltpu.VMEM((RET_CHUNK, RET_CHUNK), F32),
                        pltpu.VMEM((2, RET_CHUNK, LANES), F32),
                        pltpu.VMEM((2, 1, RET_CHUNK), F32),
                        pltpu.VMEM((nb, n_chunks, 2 * RET_DIM, RET_DIM), F32),
                        pltpu.VMEM((nb, n_chunks, 2 * RET_DIM, RET_DIM), BF16),
                        pltpu.VMEM((nb, 2, RET_CHUNK, RET_CHUNK), BF16)],
        compiler_params=_params(2),
        name="ret",
    )(log_gammas, rq, rkt, rv, rg, crkt, crv, gn_w)


N_DR = 2 * NA_KH - 1
N_DC = 2 * NA_KW - 1
PAIR_ROWS = 2 * GRID_W
NA_GROUP = 2
NA_BATCHES = 2


def _na_build_bias(rpb_ref, bias_ref, pair):
    qi = lax.broadcasted_iota(jnp.int32, (GRID_W, LANES), 0)
    li = lax.broadcasted_iota(jnp.int32, (GRID_W, LANES), 1)
    key_col = li & (GRID_W - 1)
    col_start = jnp.clip(qi - NA_KW // 2, 0, GRID_W - NA_KW)
    valid = (key_col >= col_start) & (key_col < col_start + NA_KW)
    code = jnp.where(valid, key_col - qi + (NA_KW - 1), -1)
    first_row = lax.broadcasted_iota(jnp.int32, (1, LANES), 1) < GRID_W

    def pattern(t, carry):
        for hd in range(2):
            head = 2 * pair + hd
            for blk in range(NA_KH // 2):
                base = (head * N_DR + (2 * blk + NA_KH - 1 - t)) * N_DC
                acc = jnp.full((GRID_W, LANES), NEG_INF, F32)
                for b in range(N_DC):
                    val = jnp.where(first_row, rpb_ref[base + b], rpb_ref[base + N_DC + b]) * LOG2_E
                    acc = jnp.where(code == b, val, acc)
                bias_ref[t, hd * GRID_W:(hd + 1) * GRID_W, blk * LANES:(blk + 1) * LANES] = acc
        return carry

    lax.fori_loop(0, N_PATTERNS, pattern, 0)


def _na_kernel(rpb_ref, q2_ref, kt_ref, v_ref, kct_ref, vc_ref, o_ref,
               bias_ref, kt2_ref, s_ref):
    pair = pl.program_id(0)
    low = lax.broadcasted_iota(jnp.int32, (1, LANES), 1) < NA_DIM
    half = NA_KH // 2

    @pl.when(pl.program_id(1) == 0)
    def _bias():
        _na_build_bias(rpb_ref, bias_ref, pair)

    batches = range(NA_BATCHES)
    for bb in batches:
        kt2_ref[bb, 0] = kt_ref[bb]
        kt2_ref[bb, 1, :, 0:SEQ - LANES] = kt_ref[bb, :, GRID_W:SEQ - GRID_W]
        kt2_ref[bb, 1, :, SEQ - LANES:SEQ] = jnp.zeros((LANES, LANES), BF16)

    def band_start(r):
        return jnp.clip(r - half, 0, GRID_ROWS - NA_KH)

    def scores(bb, group, slot):
        rows = group * (NA_GROUP * PAIR_ROWS)
        group_q = pl.ds(pl.multiple_of(rows, NA_GROUP * PAIR_ROWS), NA_GROUP * PAIR_ROWS)
        s_ref[bb, slot, :, NA_BAND:] = jnp.dot(q2_ref[bb, 0, group_q, :], kct_ref[bb], preferred_element_type=F32)
        for j in range(NA_GROUP):
            r = group * NA_GROUP + j
            start = band_start(r)
            pattern = jnp.where(r < half, r, jnp.where(r > GRID_ROWS - half, r - (GRID_ROWS - NA_KH), half))
            lanes = pl.ds(pl.multiple_of((start >> 1) * LANES, LANES), NA_BAND)
            kb = kt2_ref[bb, start & 1, :, lanes]
            q2 = q2_ref[bb, 0, pl.ds(pl.multiple_of(r * PAIR_ROWS, PAIR_ROWS), PAIR_ROWS), :]
            s_ref[bb, slot, j * PAIR_ROWS:(j + 1) * PAIR_ROWS, 0:NA_BAND] = (
                jnp.dot(q2, kb, preferred_element_type=F32) + bias_ref[pattern])

    n_blocks = (NA_BAND + CTX_LEN) // LANES

    def outputs(bb, group, slot):
        probs = []
        for j in range(NA_GROUP):
            rows = slice(j * PAIR_ROWS, (j + 1) * PAIR_ROWS)

            def block(b):
                return s_ref[bb, slot, rows, b * LANES:(b + 1) * LANES]

            m = block(0)
            for b in range(1, n_blocks):
                m = jnp.maximum(m, block(b))
            m = jnp.max(m, axis=-1, keepdims=True)
            p = [jnp.exp2(block(b) - m) for b in range(n_blocks)]
            den = p[0]
            for pb in p[1:]:
                den = den + pb
            den = jnp.sum(den, axis=-1, keepdims=True)
            probs.append((jnp.concatenate([pb.astype(BF16) for pb in p], axis=1), den))
        ctx = jnp.dot(jnp.concatenate([p[:, NA_BAND:] for p, _ in probs], axis=0), vc_ref[bb],
                      preferred_element_type=F32)
        for j, (p, den) in enumerate(probs):
            r = group * NA_GROUP + j
            vb = v_ref[bb, pl.ds(pl.multiple_of(band_start(r) * GRID_W, GRID_W), NA_BAND), :]
            o = (jnp.dot(p[:, 0:NA_BAND], vb, preferred_element_type=F32)
                 + ctx[j * PAIR_ROWS:(j + 1) * PAIR_ROWS]) * (1.0 / den)
            o_ref[bb, pl.ds(pl.multiple_of(r * GRID_W, GRID_W), GRID_W), :] = (
                jnp.where(low, o[:GRID_W], o[GRID_W:]).astype(BF16))

    n_groups = GRID_ROWS // NA_GROUP
    for bb in batches:
        scores(bb, 0, 0)
    for g in range(n_groups):
        for bb in batches:
            if g + 1 < n_groups:
                scores(bb, g + 1, (g + 1) % 2)
        for bb in batches:
            outputs(bb, g, g % 2)


def _neighbourhood_attention(rpb, nq, nkt, nv, cnkt, cnv):
    batch = nq.shape[0]
    n_q2 = GRID_ROWS * PAIR_ROWS
    nb = NA_BATCHES
    lat = pl.BlockSpec((nb, SEQ, LANES), lambda p, b: (b, 0, p))
    cx = pl.BlockSpec((nb, CTX_LEN, LANES), lambda p, b: (b, 0, p))
    return pl.pallas_call(
        _na_kernel,
        grid=(NA_HEADS // 2, batch // nb),
        in_specs=[pl.BlockSpec(memory_space=pltpu.SMEM),
                  pl.BlockSpec((nb, 1, n_q2, LANES), lambda p, b: (b, p, 0, 0)),
                  pl.BlockSpec((nb, LANES, SEQ), lambda p, b: (b, p, 0)), lat,
                  pl.BlockSpec((nb, LANES, CTX_LEN), lambda p, b: (b, p, 0)), cx],
        out_specs=lat,
        out_shape=jax.ShapeDtypeStruct((batch, SEQ, NA_WIDTH), BF16),
        scratch_shapes=[pltpu.VMEM((N_PATTERNS, PAIR_ROWS, NA_BAND), F32),
                        pltpu.VMEM((nb, 2, LANES, SEQ), BF16),
                        pltpu.VMEM((nb, 2, NA_GROUP * PAIR_ROWS, NA_BAND + CTX_LEN), F32)],
        compiler_params=_params(2),
        name="na",
    )(rpb, nq, nkt, nv, cnkt, cnv)


def _out_kernel(x_ref, ret_ref, na_ref, mod_ref, gpm_ref, gpre_ref, gpost_ref, wo_ref, w1_ref, w2_ref, o_ref):
    gt1, sh2, sc2, gt2 = (mod_ref[0, i:i + 1, :] for i in (2, 3, 4, 5))
    subs = [slice(i * SUB_TILE, (i + 1) * SUB_TILE) for i in range(OUT_TILE // SUB_TILE)]
    mix = [jnp.dot(ret_ref[0, rows, :], wo_ref[0:RET_WIDTH, :], preferred_element_type=F32)
           + jnp.dot(na_ref[0, rows, :], wo_ref[RET_WIDTH:, :], preferred_element_type=F32) for rows in subs]
    h2 = []
    for rows, m in zip(subs, mix):
        x1 = x_ref[0, rows, :] + gt1 * _rms(m, gpm_ref[...])
        o_ref[0, rows, :] = x1
        h2.append(_prenorm(x1, gpre_ref[...], sh2, sc2))
    for rows, h in zip(subs, h2):
        acc = jnp.zeros((SUB_TILE, D_MODEL), F32)
        for ci in range(D_FF // FF_CHUNK):
            cols = slice(ci * FF_CHUNK, (ci + 1) * FF_CHUNK)
            a = jnp.maximum(jnp.dot(h, w1_ref[:, cols], preferred_element_type=F32), 0.0)
            acc = acc + jnp.dot((a * a).astype(BF16), w2_ref[cols, :], preferred_element_type=F32)
        o_ref[0, rows, :] = o_ref[0, rows, :] + gt2 * _rms(acc, gpost_ref[...])


def _out_proj_mlp(x, ret_lat, na_lat, mods, g_post_mix, g_pre_mlp, g_post_mlp, w_out, w_mlp1, w_mlp2):
    batch = x.shape[0]
    tok = lambda b, i: (b, i, 0)
    gain = pl.BlockSpec((1, D_MODEL), lambda b, i: (0, 0))
    return pl.pallas_call(
        _out_kernel,
        grid=(batch, SEQ // OUT_TILE),
        in_specs=[pl.BlockSpec((1, OUT_TILE, D_MODEL), tok),
                  pl.BlockSpec((1, OUT_TILE, RET_WIDTH), tok),
                  pl.BlockSpec((1, OUT_TILE, NA_WIDTH), tok),
                  pl.BlockSpec((1, N_MOD, D_MODEL), lambda b, i: (b, 0, 0)),
                  gain, gain, gain,
                  _resident((RET_WIDTH + NA_WIDTH, D_MODEL)),
                  _resident((D_MODEL, D_FF)),
                  _resident((D_FF, D_MODEL))],
        out_specs=pl.BlockSpec((1, OUT_TILE, D_MODEL), tok),
        out_shape=jax.ShapeDtypeStruct(x.shape, F32),
        compiler_params=_params(2),
        name="out_mlp",
    )(x, ret_lat, na_lat, mods, g_post_mix, g_pre_mlp, g_post_mlp, w_out, w_mlp1, w_mlp2)


def kernel(x, c, ctx, c_ctx, w_ada, b_ada, g_pre_mix, g_post_mix, g_pre_mlp, g_post_mlp,
           w_in, ret_decay, ret_gn, na_rpb, w_out, w_mlp1, w_mlp2):
    assert w_in.shape[0] == 1, "single-layer block: the context stream is never updated"
    batch = x.shape[0]
    pad = (-(batch + 1)) % 8
    cc = jnp.concatenate([c, c_ctx[None, :], jnp.zeros((pad, D_MODEL), F32)], axis=0)
    mods = _modulations(cc, w_ada[0], b_ada)
    mods_lat = mods[:batch].reshape(batch, N_MOD, D_MODEL)
    mods_ctx = mods[batch:batch + 1].reshape(1, N_MOD, D_MODEL)

    rq, rk, rv, rg, nq, nkt, nv = _in_proj_lat(x, mods_lat, g_pre_mix, w_in[0])
    crk, crv, cnkt, cnv = _in_proj_ctx(ctx, mods_ctx, g_pre_mix, w_in[0])

    log_gammas = jax.nn.log_sigmoid(ret_decay[0].astype(F32))
    ret_lat = _retention(log_gammas, rq, rk, rv, rg, crk, crv, ret_gn)
    na_lat = _neighbourhood_attention(na_rpb[0].astype(F32).reshape(-1), nq, nkt, nv, cnkt, cnv)

    return _out_proj_mlp(x, ret_lat, na_lat, mods_lat, g_post_mix, g_pre_mlp, g_post_mlp,
                         w_out[0].astype(BF16), w_mlp1[0].astype(BF16), w_mlp2[0].astype(BF16))
```

```python
import numpy as np
import jax
import jax.numpy as jnp
from jax import lax
from jax.experimental import pallas as pl
from jax.experimental.pallas import tpu as pltpu

D_MODEL = 1024
SEQ = 2048
CTX_LEN = 256
GRID_W = 64
GRID_ROWS = SEQ // GRID_W
RET_HEADS = 4
RET_DIM = 128
RET_WIDTH = RET_HEADS * RET_DIM
NA_HEADS = 8
NA_DIM = 64
NA_WIDTH = NA_HEADS * NA_DIM
NA_KH = 8
NA_KW = 16
N_GROUPS = 7
GROUP_W = 512
_K_GROUPS = (1, 5)
D_FF = 4 * D_MODEL
ROPE_BASE = 10000.0
NORM_EPS = 1e-6
N_MOD = 6
NEG_INF = -1e30
LOG2_E = 1.4426950408889634

LANES = 128
ROW_TILE = 1024
CTX_BATCHES = 2
OUT_TILE = 1024
SUB_TILE = 256
FF_CHUNK = 1024
RET_CHUNK = 256
RET_BATCHES = 2
NA_BAND = NA_KH * GRID_W
N_PATTERNS = 8
VMEM_LIMIT = 56 * 1024 * 1024

F32 = jnp.float32
BF16 = jnp.bfloat16


def _silu(x):
    return x * (1.0 / (1.0 + jnp.exp(-x)))


def _rms(x, g):
    return x * lax.rsqrt(jnp.mean(x * x, axis=-1, keepdims=True) + NORM_EPS) * g


def _params(n_axes):
    return pltpu.CompilerParams(dimension_semantics=("arbitrary",) * n_axes,
                                vmem_limit_bytes=VMEM_LIMIT)


def _resident(shape):
    nd = len(shape)
    return pl.BlockSpec(shape, lambda *_: (0,) * nd, pipeline_mode=pl.Buffered(1))


def _mod_kernel(c_ref, w_ref, b_ref, o_ref):
    a = _silu(c_ref[...]).astype(BF16)
    o_ref[...] = jnp.dot(a, w_ref[...].astype(BF16), preferred_element_type=F32) + b_ref[...]


def _modulations(cc, w_ada, b_ada):
    rows = cc.shape[0]
    return pl.pallas_call(
        _mod_kernel,
        grid=(N_MOD,),
        in_specs=[pl.BlockSpec((rows, D_MODEL), lambda j: (0, 0)),
                  pl.BlockSpec((D_MODEL, D_MODEL), lambda j: (0, j)),
                  pl.BlockSpec((1, D_MODEL), lambda j: (0, j))],
        out_specs=pl.BlockSpec((rows, D_MODEL), lambda j: (0, j)),
        out_shape=jax.ShapeDtypeStruct((rows, N_MOD * D_MODEL), F32),
        compiler_params=_params(1),
        name="mod",
    )(cc, w_ada, b_ada)


def _prenorm(x, g, shift, scale):
    return (_rms(x, g) * (1.0 + scale) + shift).astype(BF16)


def _rope(blk, cos, sin_signed, first_half):
    partner = jnp.where(first_half, pltpu.roll(blk, LANES - 32, 1), pltpu.roll(blk, 32, 1))
    return blk * cos + partner * sin_signed


def _in_lat_kernel(x_ref, mod_ref, g_ref, w_ref, cq_ref, sq_ref, ck_ref, sk_ref, *out_refs):
    h = _prenorm(x_ref[0], g_ref[...], mod_ref[0, 0:1, :], mod_ref[0, 1:2, :])
    lane = lax.broadcasted_iota(jnp.int32, (1, LANES), 1)
    first_half = (lane & 32) == 0
    for gi, o_ref in enumerate(out_refs):
        w = w_ref[:, gi * GROUP_W:(gi + 1) * GROUP_W].astype(BF16)
        acc = jnp.dot(h, w, preferred_element_type=F32)
        if gi == 0:
            for hh in range(RET_HEADS):
                blk = _rope(acc[:, hh * LANES:(hh + 1) * LANES], cq_ref[...], sq_ref[...], first_half)
                o_ref[0, :, hh * LANES:(hh + 1) * LANES] = blk.astype(BF16)
        elif gi == 1:
            for hh in range(RET_HEADS):
                blk = _rope(acc[:, hh * LANES:(hh + 1) * LANES], ck_ref[...], sk_ref[...], first_half)
                o_ref[0, hh * LANES:(hh + 1) * LANES, :] = blk.T.astype(BF16)
        elif gi == 4:
            q = (acc * (NA_DIM ** -0.5 * LOG2_E)).astype(BF16)
            zero = jnp.zeros((ROW_TILE, LANES), BF16)
            for pair in range(NA_HEADS // 2):
                qp = q[:, pair * LANES:(pair + 1) * LANES]
                first = jnp.where(lane < NA_DIM, qp, zero)
                second = jnp.where(lane < NA_DIM, zero, qp)
                for rr in range(ROW_TILE // GRID_W):
                    tokens = slice(rr * GRID_W, (rr + 1) * GRID_W)
                    o_ref[0, pair, rr * PAIR_ROWS:rr * PAIR_ROWS + GRID_W, :] = first[tokens]
                    o_ref[0, pair, rr * PAIR_ROWS + GRID_W:(rr + 1) * PAIR_ROWS, :] = second[tokens]
        elif gi == 3:
            o_ref[0] = _silu(acc).astype(BF16)
        elif gi in _K_GROUPS:
            o_ref[0] = acc.T.astype(BF16)
        else:
            o_ref[0] = acc.astype(BF16)


_CTX_GROUPS = (1, 2, 5, 6)


def _in_ctx_kernel(x_ref, mod_ref, g_ref, w_ref, *out_refs):
    x = jnp.concatenate([x_ref[i] for i in range(CTX_BATCHES)], axis=0)
    h = _prenorm(x, g_ref[...], mod_ref[0, 0:1, :], mod_ref[0, 1:2, :])
    for gi, o_ref in zip(_CTX_GROUPS, out_refs):
        w = w_ref[:, gi * GROUP_W:(gi + 1) * GROUP_W].astype(BF16)
        acc = jnp.dot(h, w, preferred_element_type=F32)
        acc = (acc.T if gi in _K_GROUPS else acc).astype(BF16)
        for i in range(CTX_BATCHES):
            tokens = slice(i * CTX_LEN, (i + 1) * CTX_LEN)
            o_ref[i] = acc[:, tokens] if gi in _K_GROUPS else acc[tokens, :]


def _rope_tables():
    tok = np.arange(SEQ)
    n_freq = RET_DIM // 4
    inv = ROPE_BASE ** (-np.arange(n_freq, dtype=np.float64) / n_freq)
    ang_r = (tok // GRID_W)[:, None] * inv[None, :]
    ang_c = (tok % GRID_W)[:, None] * inv[None, :]
    cos = np.concatenate([np.cos(ang_r), np.cos(ang_r), np.cos(ang_c), np.cos(ang_c)], axis=-1)
    sin = np.concatenate([-np.sin(ang_r), np.sin(ang_r), -np.sin(ang_c), np.sin(ang_c)], axis=-1)
    qs = RET_DIM ** -0.5
    return [jnp.asarray(t, F32) for t in (cos * qs, sin * qs, cos, sin)]


def _in_proj_lat(x, mods, g_pre, w_in):
    batch = x.shape[0]
    tok = lambda b, i: (b, i, 0)
    tab = pl.BlockSpec((ROW_TILE, LANES), lambda b, i: (i, 0))
    out_shapes = [jax.ShapeDtypeStruct((batch, SEQ, GROUP_W), BF16)] * N_GROUPS
    out_specs = [pl.BlockSpec((1, ROW_TILE, GROUP_W), tok)] * N_GROUPS
    for gi in _K_GROUPS:
        out_shapes[gi] = jax.ShapeDtypeStruct((batch, GROUP_W, SEQ), BF16)
        out_specs[gi] = pl.BlockSpec((1, GROUP_W, ROW_TILE), lambda b, i: (b, 0, i))
    n_pairs = NA_HEADS // 2
    out_shapes[4] = jax.ShapeDtypeStruct((batch, n_pairs, GRID_ROWS * PAIR_ROWS, LANES), BF16)
    out_specs[4] = pl.BlockSpec((1, n_pairs, ROW_TILE // GRID_W * PAIR_ROWS, LANES), lambda b, i: (b, 0, i, 0))
    return pl.pallas_call(
        _in_lat_kernel,
        grid=(batch, SEQ // ROW_TILE),
        in_specs=[pl.BlockSpec((1, ROW_TILE, D_MODEL), tok),
                  pl.BlockSpec((1, N_MOD, D_MODEL), lambda b, i: (b, 0, 0)),
                  pl.BlockSpec((1, D_MODEL), lambda b, i: (0, 0)),
                  _resident((D_MODEL, N_GROUPS * GROUP_W)),
                  tab, tab, tab, tab],
        out_specs=out_specs,
        out_shape=out_shapes,
        compiler_params=_params(2),
        name="in_lat",
    )(x, mods, g_pre, w_in, *_rope_tables())


def _in_proj_ctx(ctx, mods_ctx, g_pre, w_in):
    batch = ctx.shape[0]
    tok = lambda b: (b, 0, 0)
    out_shapes = [jax.ShapeDtypeStruct((batch, GROUP_W, CTX_LEN) if gi in _K_GROUPS else (batch, CTX_LEN, GROUP_W),
                                       BF16) for gi in _CTX_GROUPS]
    out_specs = [pl.BlockSpec((CTX_BATCHES, GROUP_W, CTX_LEN) if gi in _K_GROUPS else (CTX_BATCHES, CTX_LEN, GROUP_W),
                              tok) for gi in _CTX_GROUPS]
    return pl.pallas_call(
        _in_ctx_kernel,
        grid=(batch // CTX_BATCHES,),
        in_specs=[pl.BlockSpec((CTX_BATCHES, CTX_LEN, D_MODEL), tok),
                  pl.BlockSpec((1, N_MOD, D_MODEL), lambda b: (0, 0, 0)),
                  pl.BlockSpec((1, D_MODEL), lambda b: (0, 0)),
                  _resident((D_MODEL, N_GROUPS * GROUP_W))],
        out_specs=out_specs,
        out_shape=out_shapes,
        compiler_params=_params(1),
        name="in_ctx",
    )(ctx, mods_ctx, g_pre, w_in)


def _ret_kernel(lg_ref, q_ref, kt_ref, v_ref, g_ref, kct_ref, vc_ref, gn_ref, o_ref,
                dec_ref, wq_ref, wk_ref, upd_ref, st_ref, sp_ref):
    head = pl.program_id(0)
    lgf = lg_ref[0, head]
    lgb = lg_ref[1, head]
    c = RET_CHUNK
    n_chunks = SEQ // c

    @pl.when(pl.program_id(1) == 0)
    def _tables():
        ii = lax.broadcasted_iota(jnp.int32, (c, c), 0)
        jj = lax.broadcasted_iota(jnp.int32, (c, c), 1)
        diff = (ii - jj).astype(F32)
        dec_ref[...] = jnp.exp(jnp.where(diff >= 0, lgf * diff, -lgb * diff))
        ri = lax.broadcasted_iota(jnp.int32, (c, LANES), 0).astype(F32)
        wq_ref[0] = jnp.exp(lgf * (ri + 1.0))
        wq_ref[1] = jnp.exp(lgb * (c - ri))
        ti = lax.broadcasted_iota(jnp.int32, (1, c), 1).astype(F32)
        wk_ref[0] = jnp.exp(lgf * (c - 1.0 - ti))
        wk_ref[1] = jnp.exp(lgb * ti)

    ones = jnp.ones((1, LANES), F32)
    chunk_f = jnp.exp(ones * (lgf * c))
    chunk_b = jnp.exp(ones * (lgb * c))

    def update(kt, v):
        ktf = kt.astype(F32)
        kw = jnp.concatenate([(ktf * wk_ref[0]).astype(BF16), (ktf * wk_ref[1]).astype(BF16)], axis=0)
        return jnp.dot(kw, v, preferred_element_type=F32)

    def chunk(n):
        return slice(n * c, (n + 1) * c)

    batches = range(RET_BATCHES)
    for b in batches:
        for n in range(n_chunks):
            upd_ref[b, n] = update(kt_ref[b, :, chunk(n)], v_ref[b, chunk(n), :])

    for b in batches:
        ctx_upd = update(kct_ref[b], vc_ref[b])
        state = ctx_upd[:RET_DIM]
        for n in range(n_chunks):
            st_ref[b, n, 0:RET_DIM, :] = state.astype(BF16)
            state = chunk_f * state + upd_ref[b, n, 0:RET_DIM, :]
        state = ctx_upd[RET_DIM:]
        for n in reversed(range(n_chunks)):
            st_ref[b, n, RET_DIM:, :] = state.astype(BF16)
            state = chunk_b * state + upd_ref[b, n, RET_DIM:, :]

    def scores(b, n):
        s = jnp.dot(q_ref[b, chunk(n), :], kt_ref[b, :, chunk(n)], preferred_element_type=F32)
        sp_ref[b, n % 2] = (s * dec_ref[...]).astype(BF16)

    def outputs(b, n):
        rows = chunk(n)
        qf = q_ref[b, rows, :].astype(F32)
        qw = jnp.concatenate([(qf * wq_ref[0]).astype(BF16), (qf * wq_ref[1]).astype(BF16)], axis=1)
        o = (jnp.dot(sp_ref[b, n % 2], v_ref[b, rows, :], preferred_element_type=F32)
             + jnp.dot(qw, st_ref[b, n], preferred_element_type=F32))
        d = o - jnp.mean(o, axis=-1, keepdims=True)
        y = d * lax.rsqrt(jnp.mean(d * d, axis=-1, keepdims=True) + NORM_EPS) * gn_ref[...]
        o_ref[b, rows, :] = (y * g_ref[b, rows, :].astype(F32)).astype(BF16)

    for b in batches:
        scores(b, 0)
    for n in range(n_chunks):
        for b in batches:
            if n + 1 < n_chunks:
                scores(b, n + 1)
        for b in batches:
            outputs(b, n)


def _retention(log_gammas, rq, rkt, rv, rg, crkt, crv, gn_w):
    assert CTX_LEN == RET_CHUNK
    batch = rq.shape[0]
    n_chunks = SEQ // RET_CHUNK
    nb = RET_BATCHES
    lat = pl.BlockSpec((nb, SEQ, LANES), lambda h, b: (b, 0, h))
    cx = pl.BlockSpec((nb, CTX_LEN, LANES), lambda h, b: (b, 0, h))
    return pl.pallas_call(
        _ret_kernel,
        grid=(RET_HEADS, batch // nb),
        in_specs=[pl.BlockSpec(memory_space=pltpu.SMEM), lat,
                  pl.BlockSpec((nb, LANES, SEQ), lambda h, b: (b, h, 0)), lat, lat,
                  pl.BlockSpec((nb, LANES, CTX_LEN), lambda h, b: (b, h, 0)), cx,
                  pl.BlockSpec((1, LANES), lambda h, b: (0, h))],
        out_specs=lat,
        out_shape=jax.ShapeDtypeStruct((batch, SEQ, RET_WIDTH), BF16),
        scratch_shapes=[pltpu.VMEM((RET_CHUNK, RET_CHUNK), F32),
                        pltpu.VMEM((2, RET_CHUNK, LANES), F32),
                        pltpu.VMEM((2, 1, RET_CHUNK), F32),
                        pltpu.VMEM((nb, n_chunks, 2 * RET_DIM, RET_DIM), F32),
                        pltpu.VMEM((nb, n_chunks, 2 * RET_DIM, RET_DIM), BF16),
                        pltpu.VMEM((nb, 2, RET_CHUNK, RET_CHUNK), BF16)],
        compiler_params=_params(2),
        name="ret",
    )(log_gammas, rq, rkt, rv, rg, crkt, crv, gn_w)


N_DR = 2 * NA_KH - 1
N_DC = 2 * NA_KW - 1
PAIR_ROWS = 2 * GRID_W
NA_GROUP = 2
NA_BATCHES = 2


def _na_build_bias(rpb_ref, bias_ref, pair):
    qi = lax.broadcasted_iota(jnp.int32, (GRID_W, LANES), 0)
    li = lax.broadcasted_iota(jnp.int32, (GRID_W, LANES), 1)
    key_col = li & (GRID_W - 1)
    col_start = jnp.clip(qi - NA_KW // 2, 0, GRID_W - NA_KW)
    valid = (key_col >= col_start) & (key_col < col_start + NA_KW)
    code = jnp.where(valid, key_col - qi + (NA_KW - 1), -1)
    first_row = lax.broadcasted_iota(jnp.int32, (1, LANES), 1) < GRID_W

    def pattern(t, carry):
        for hd in range(2):
            head = 2 * pair + hd
            for blk in range(NA_KH // 2):
                base = (head * N_DR + (2 * blk + NA_KH - 1 - t)) * N_DC
                acc = jnp.full((GRID_W, LANES), NEG_INF, F32)
                for b in range(N_DC):
                    val = jnp.where(first_row, rpb_ref[base + b], rpb_ref[base + N_DC + b]) * LOG2_E
                    acc = jnp.where(code == b, val, acc)
                bias_ref[t, hd * GRID_W:(hd + 1) * GRID_W, blk * LANES:(blk + 1) * LANES] = acc
        return carry

    lax.fori_loop(0, N_PATTERNS, pattern, 0)


def _na_kernel(rpb_ref, q2_ref, kt_ref, v_ref, kct_ref, vc_ref, o_ref,
               bias_ref, kt2_ref, s_ref):
    pair = pl.program_id(0)
    low = lax.broadcasted_iota(jnp.int32, (1, LANES), 1) < NA_DIM
    half = NA_KH // 2

    @pl.when(pl.program_id(1) == 0)
    def _bias():
        _na_build_bias(rpb_ref, bias_ref, pair)

    batches = range(NA_BATCHES)
    for bb in batches:
        kt2_ref[bb, 0] = kt_ref[bb]
        kt2_ref[bb, 1, :, 0:SEQ - LANES] = kt_ref[bb, :, GRID_W:SEQ - GRID_W]
        kt2_ref[bb, 1, :, SEQ - LANES:SEQ] = jnp.zeros((LANES, LANES), BF16)

    def band_start(r):
        return jnp.clip(r - half, 0, GRID_ROWS - NA_KH)

    def scores(bb, group, slot):
        rows = group * (NA_GROUP * PAIR_ROWS)
        group_q = pl.ds(pl.multiple_of(rows, NA_GROUP * PAIR_ROWS), NA_GROUP * PAIR_ROWS)
        s_ref[bb, slot, :, NA_BAND:] = jnp.dot(q2_ref[bb, 0, group_q, :], kct_ref[bb], preferred_element_type=F32)
        for j in range(NA_GROUP):
            r = group * NA_GROUP + j
            start = band_start(r)
            pattern = jnp.where(r < half, r, jnp.where(r > GRID_ROWS - half, r - (GRID_ROWS - NA_KH), half))
            lanes = pl.ds(pl.multiple_of((start >> 1) * LANES, LANES), NA_BAND)
            kb = kt2_ref[bb, start & 1, :, lanes]
            q2 = q2_ref[bb, 0, pl.ds(pl.multiple_of(r * PAIR_ROWS, PAIR_ROWS), PAIR_ROWS), :]
            s_ref[bb, slot, j * PAIR_ROWS:(j + 1) * PAIR_ROWS, 0:NA_BAND] = (
                jnp.dot(q2, kb, preferred_element_type=F32) + bias_ref[pattern])

    n_blocks = (NA_BAND + CTX_LEN) // LANES

    def outputs(bb, group, slot):
        probs = []
        for j in range(NA_GROUP):
            rows = slice(j * PAIR_ROWS, (j + 1) * PAIR_ROWS)

            def block(b):
                return s_ref[bb, slot, rows, b * LANES:(b + 1) * LANES]

            m = block(0)
            for b in range(1, n_blocks):
                m = jnp.maximum(m, block(b))
            m = jnp.max(m, axis=-1, keepdims=True)
            p = [jnp.exp2(block(b) - m) for b in range(n_blocks)]
            den = p[0]
            for pb in p[1:]:
                den = den + pb
            den = jnp.sum(den, axis=-1, keepdims=True)
            probs.append((jnp.concatenate([pb.astype(BF16) for pb in p], axis=1), den))
        ctx = jnp.dot(jnp.concatenate([p[:, NA_BAND:] for p, _ in probs], axis=0), vc_ref[bb],
                      preferred_element_type=F32)
        for j, (p, den) in enumerate(probs):
            r = group * NA_GROUP + j
            vb = v_ref[bb, pl.ds(pl.multiple_of(band_start(r) * GRID_W, GRID_W), NA_BAND), :]
            o = (jnp.dot(p[:, 0:NA_BAND], vb, preferred_element_type=F32)
                 + ctx[j * PAIR_ROWS:(j + 1) * PAIR_ROWS]) * (1.0 / den)
            o_ref[bb, pl.ds(pl.multiple_of(r * GRID_W, GRID_W), GRID_W), :] = (
                jnp.where(low, o[:GRID_W], o[GRID_W:]).astype(BF16))

    n_groups = GRID_ROWS // NA_GROUP
    for bb in batches:
        scores(bb, 0, 0)
    for g in range(n_groups):
        for bb in batches:
            if g + 1 < n_groups:
                scores(bb, g + 1, (g + 1) % 2)
        for bb in batches:
            outputs(bb, g, g % 2)


def _neighbourhood_attention(rpb, nq, nkt, nv, cnkt, cnv):
    batch = nq.shape[0]
    n_q2 = GRID_ROWS * PAIR_ROWS
    nb = NA_BATCHES
    lat = pl.BlockSpec((nb, SEQ, LANES), lambda p, b: (b, 0, p))
    cx = pl.BlockSpec((nb, CTX_LEN, LANES), lambda p, b: (b, 0, p))
    return pl.pallas_call(
        _na_kernel,
        grid=(NA_HEADS // 2, batch // nb),
        in_specs=[pl.BlockSpec(memory_space=pltpu.SMEM),
                  pl.BlockSpec((nb, 1, n_q2, LANES), lambda p, b: (b, p, 0, 0)),
                  pl.BlockSpec((nb, LANES, SEQ), lambda p, b: (b, p, 0)), lat,
                  pl.BlockSpec((nb, LANES, CTX_LEN), lambda p, b: (b, p, 0)), cx],
        out_specs=lat,
        out_shape=jax.ShapeDtypeStruct((batch, SEQ, NA_WIDTH), BF16),
        scratch_shapes=[pltpu.VMEM((N_PATTERNS, PAIR_ROWS, NA_BAND), F32),
                        pltpu.VMEM((nb, 2, LANES, SEQ), BF16),
                        pltpu.VMEM((nb, 2, NA_GROUP * PAIR_ROWS, NA_BAND + CTX_LEN), F32)],
        compiler_params=_params(2),
        name="na",
    )(rpb, nq, nkt, nv, cnkt, cnv)


def _out_kernel(x_ref, ret_ref, na_ref, mod_ref, gpm_ref, gpre_ref, gpost_ref, wo_ref, w1_ref, w2_ref, o_ref):
    gt1, sh2, sc2, gt2 = (mod_ref[0, i:i + 1, :] for i in (2, 3, 4, 5))
    subs = [slice(i * SUB_TILE, (i + 1) * SUB_TILE) for i in range(OUT_TILE // SUB_TILE)]
    mix = [jnp.dot(ret_ref[0, rows, :], wo_ref[0:RET_WIDTH, :], preferred_element_type=F32)
           + jnp.dot(na_ref[0, rows, :], wo_ref[RET_WIDTH:, :], preferred_element_type=F32) for rows in subs]
    h2 = []
    for rows, m in zip(subs, mix):
        x1 = x_ref[0, rows, :] + gt1 * _rms(m, gpm_ref[...])
        o_ref[0, rows, :] = x1
        h2.append(_prenorm(x1, gpre_ref[...], sh2, sc2))
    for rows, h in zip(subs, h2):
        acc = jnp.zeros((SUB_TILE, D_MODEL), F32)
        for ci in range(D_FF // FF_CHUNK):
            cols = slice(ci * FF_CHUNK, (ci + 1) * FF_CHUNK)
            a = jnp.maximum(jnp.dot(h, w1_ref[:, cols], preferred_element_type=F32), 0.0)
            acc = acc + jnp.dot((a * a).astype(BF16), w2_ref[cols, :], preferred_element_type=F32)
        o_ref[0, rows, :] = o_ref[0, rows, :] + gt2 * _rms(acc, gpost_ref[...])


def _out_proj_mlp(x, ret_lat, na_lat, mods, g_post_mix, g_pre_mlp, g_post_mlp, w_out, w_mlp1, w_mlp2):
    batch = x.shape[0]
    tok = lambda b, i: (b, i, 0)
    gain = pl.BlockSpec((1, D_MODEL), lambda b, i: (0, 0))
    return pl.pallas_call(
        _out_kernel,
        grid=(batch, SEQ // OUT_TILE),
        in_specs=[pl.BlockSpec((1, OUT_TILE, D_MODEL), tok),
                  pl.BlockSpec((1, OUT_TILE, RET_WIDTH), tok),
                  pl.BlockSpec((1, OUT_TILE, NA_WIDTH), tok),
                  pl.BlockSpec((1, N_MOD, D_MODEL), lambda b, i: (b, 0, 0)),
                  gain, gain, gain,
                  _resident((RET_WIDTH + NA_WIDTH, D_MODEL)),
                  _resident((D_MODEL, D_FF)),
                  _resident((D_FF, D_MODEL))],
        out_specs=pl.BlockSpec((1, OUT_TILE, D_MODEL), tok),
        out_shape=jax.ShapeDtypeStruct(x.shape, F32),
        compiler_params=_params(2),
        name="out_mlp",
    )(x, ret_lat, na_lat, mods, g_post_mix, g_pre_mlp, g_post_mlp, w_out, w_mlp1, w_mlp2)


def kernel(x, c, ctx, c_ctx, w_ada, b_ada, g_pre_mix, g_post_mix, g_pre_mlp, g_post_mlp,
           w_in, ret_decay, ret_gn, na_rpb, w_out, w_mlp1, w_mlp2):
    assert w_in.shape[0] == 1, "single-layer block: the context stream is never updated"
    batch = x.shape[0]
    pad = (-(batch + 1)) % 8
    cc = jnp.concatenate([c, c_ctx[None, :], jnp.zeros((pad, D_MODEL), F32)], axis=0)
    mods = _modulations(cc, w_ada[0], b_ada)
    mods_lat = mods[:batch].reshape(batch, N_MOD, D_MODEL)
    mods_ctx = mods[batch:batch + 1].reshape(1, N_MOD, D_MODEL)

    rq, rk, rv, rg, nq, nkt, nv = _in_proj_lat(x, mods_lat, g_pre_mix, w_in[0])
    crk, crv, cnkt, cnv = _in_proj_ctx(ctx, mods_ctx, g_pre_mix, w_in[0])

    log_gammas = jax.nn.log_sigmoid(ret_decay[0].astype(F32))
    ret_lat = _retention(log_gammas, rq, rk, rv, rg, crk, crv, ret_gn)
    na_lat = _neighbourhood_attention(na_rpb[0].astype(F32).reshape(-1), nq, nkt, nv, cnkt, cnv)

    return _out_proj_mlp(x, ret_lat, na_lat, mods_lat, g_post_mix, g_pre_mlp, g_post_mlp,
                         w_out[0].astype(BF16), w_mlp1[0].astype(BF16), w_mlp2[0].astype(BF16))
```

```python
import numpy as np
import jax
import jax.numpy as jnp
from jax import lax
from jax.experimental import pallas as pl
from jax.experimental.pallas import tpu as pltpu

D_MODEL = 1024
SEQ = 2048
CTX_LEN = 256
GRID_W = 64
GRID_ROWS = SEQ // GRID_W
RET_HEADS = 4
RET_DIM = 128
RET_WIDTH = RET_HEADS * RET_DIM
NA_HEADS = 8
NA_DIM = 64
NA_WIDTH = NA_HEADS * NA_DIM
NA_KH = 8
NA_KW = 16
N_GROUPS = 7
GROUP_W = 512
_K_GROUPS = (1, 5)
D_FF = 4 * D_MODEL
ROPE_BASE = 10000.0
NORM_EPS = 1e-6
N_MOD = 6
NEG_INF = -1e30
LOG2_E = 1.4426950408889634

LANES = 128
ROW_TILE = 1024
CTX_BATCHES = 2
OUT_TILE = 1024
SUB_TILE = 256
FF_CHUNK = 1024
RET_CHUNK = 256
RET_BATCHES = 2
NA_BAND = NA_KH * GRID_W
N_PATTERNS = 8
VMEM_LIMIT = 56 * 1024 * 1024

F32 = jnp.float32
BF16 = jnp.bfloat16


def _silu(x):
    return x * (1.0 / (1.0 + jnp.exp(-x)))


def _rms(x, g):
    return x * lax.rsqrt(jnp.mean(x * x, axis=-1, keepdims=True) + NORM_EPS) * g


def _params(n_axes):
    return pltpu.CompilerParams(dimension_semantics=("arbitrary",) * n_axes,
                                vmem_limit_bytes=VMEM_LIMIT)


def _resident(shape):
    nd = len(shape)
    return pl.BlockSpec(shape, lambda *_: (0,) * nd, pipeline_mode=pl.Buffered(1))


def _mod_kernel(c_ref, w_ref, b_ref, o_ref):
    a = _silu(c_ref[...]).astype(BF16)
    o_ref[...] = jnp.dot(a, w_ref[...].astype(BF16), preferred_element_type=F32) + b_ref[...]


def _modulations(cc, w_ada, b_ada):
    rows = cc.shape[0]
    return pl.pallas_call(
        _mod_kernel,
        grid=(N_MOD,),
        in_specs=[pl.BlockSpec((rows, D_MODEL), lambda j: (0, 0)),
                  pl.BlockSpec((D_MODEL, D_MODEL), lambda j: (0, j)),
                  pl.BlockSpec((1, D_MODEL), lambda j: (0, j))],
        out_specs=pl.BlockSpec((rows, D_MODEL), lambda j: (0, j)),
        out_shape=jax.ShapeDtypeStruct((rows, N_MOD * D_MODEL), F32),
        compiler_params=_params(1),
        name="mod",
    )(cc, w_ada, b_ada)


def _prenorm(x, g, shift, scale):
    return (_rms(x, g) * (1.0 + scale) + shift).astype(BF16)


def _rope(blk, cos, sin_signed, first_half):
    partner = jnp.where(first_half, pltpu.roll(blk, LANES - 32, 1), pltpu.roll(blk, 32, 1))
    return blk * cos + partner * sin_signed


def _in_lat_kernel(x_ref, mod_ref, g_ref, w_ref, cq_ref, sq_ref, ck_ref, sk_ref, *out_refs):
    h = _prenorm(x_ref[0], g_ref[...], mod_ref[0, 0:1, :], mod_ref[0, 1:2, :])
    lane = lax.broadcasted_iota(jnp.int32, (1, LANES), 1)
    first_half = (lane & 32) == 0
    for gi, o_ref in enumerate(out_refs):
        w = w_ref[:, gi * GROUP_W:(gi + 1) * GROUP_W].astype(BF16)
        acc = jnp.dot(h, w, preferred_element_type=F32)
        if gi == 0:
            for hh in range(RET_HEADS):
                blk = _rope(acc[:, hh * LANES:(hh + 1) * LANES], cq_ref[...], sq_ref[...], first_half)
                o_ref[0, :, hh * LANES:(hh + 1) * LANES] = blk.astype(BF16)
        elif gi == 1:
            for hh in range(RET_HEADS):
                blk = _rope(acc[:, hh * LANES:(hh + 1) * LANES], ck_ref[...], sk_ref[...], first_half)
                o_ref[0, hh * LANES:(hh + 1) * LANES, :] = blk.T.astype(BF16)
        elif gi == 4:
            q = (acc * (NA_DIM ** -0.5 * LOG2_E)).astype(BF16)
            zero = jnp.zeros((ROW_TILE, LANES), BF16)
            for pair in range(NA_HEADS // 2):
                qp = q[:, pair * LANES:(pair + 1) * LANES]
                first = jnp.where(lane < NA_DIM, qp, zero)
                second = jnp.where(lane < NA_DIM, zero, qp)
                for rr in range(ROW_TILE // GRID_W):
                    tokens = slice(rr * GRID_W, (rr + 1) * GRID_W)
                    o_ref[0, pair, rr * PAIR_ROWS:rr * PAIR_ROWS + GRID_W, :] = first[tokens]
                    o_ref[0, pair, rr * PAIR_ROWS + GRID_W:(rr + 1) * PAIR_ROWS, :] = second[tokens]
        elif gi in _K_GROUPS:
            o_ref[0] = acc.T.astype(BF16)
        else:
            o_ref[0] = acc.astype(BF16)


_CTX_GROUPS = (1, 2, 5, 6)


def _in_ctx_kernel(x_ref, mod_ref, g_ref, w_ref, *out_refs):
    x = jnp.concatenate([x_ref[i] for i in range(CTX_BATCHES)], axis=0)
    h = _prenorm(x, g_ref[...], mod_ref[0, 0:1, :], mod_ref[0, 1:2, :])
    for gi, o_ref in zip(_CTX_GROUPS, out_refs):
        w = w_ref[:, gi * GROUP_W:(gi + 1) * GROUP_W].astype(BF16)
        acc = jnp.dot(h, w, preferred_element_type=F32)
        acc = (acc.T if gi in _K_GROUPS else acc).astype(BF16)
        for i in range(CTX_BATCHES):
            tokens = slice(i * CTX_LEN, (i + 1) * CTX_LEN)
            o_ref[i] = acc[:, tokens] if gi in _K_GROUPS else acc[tokens, :]


def _rope_tables():
    tok = np.arange(SEQ)
    n_freq = RET_DIM // 4
    inv = ROPE_BASE ** (-np.arange(n_freq, dtype=np.float64) / n_freq)
    ang_r = (tok // GRID_W)[:, None] * inv[None, :]
    ang_c = (tok % GRID_W)[:, None] * inv[None, :]
    cos = np.concatenate([np.cos(ang_r), np.cos(ang_r), np.cos(ang_c), np.cos(ang_c)], axis=-1)
    sin = np.concatenate([-np.sin(ang_r), np.sin(ang_r), -np.sin(ang_c), np.sin(ang_c)], axis=-1)
    qs = RET_DIM ** -0.5
    return [jnp.asarray(t, F32) for t in (cos * qs, sin * qs, cos, sin)]


def _in_proj_lat(x, mods, g_pre, w_in):
    batch = x.shape[0]
    tok = lambda b, i: (b, i, 0)
    tab = pl.BlockSpec((ROW_TILE, LANES), lambda b, i: (i, 0))
    out_shapes = [jax.ShapeDtypeStruct((batch, SEQ, GROUP_W), BF16)] * N_GROUPS
    out_specs = [pl.BlockSpec((1, ROW_TILE, GROUP_W), tok)] * N_GROUPS
    for gi in _K_GROUPS:
        out_shapes[gi] = jax.ShapeDtypeStruct((batch, GROUP_W, SEQ), BF16)
        out_specs[gi] = pl.BlockSpec((1, GROUP_W, ROW_TILE), lambda b, i: (b, 0, i))
    n_pairs = NA_HEADS // 2
    out_shapes[4] = jax.ShapeDtypeStruct((batch, n_pairs, GRID_ROWS * PAIR_ROWS, LANES), BF16)
    out_specs[4] = pl.BlockSpec((1, n_pairs, ROW_TILE // GRID_W * PAIR_ROWS, LANES), lambda b, i: (b, 0, i, 0))
    return pl.pallas_call(
        _in_lat_kernel,
        grid=(batch, SEQ // ROW_TILE),
        in_specs=[pl.BlockSpec((1, ROW_TILE, D_MODEL), tok),
                  pl.BlockSpec((1, N_MOD, D_MODEL), lambda b, i: (b, 0, 0)),
                  pl.BlockSpec((1, D_MODEL), lambda b, i: (0, 0)),
                  _resident((D_MODEL, N_GROUPS * GROUP_W)),
                  tab, tab, tab, tab],
        out_specs=out_specs,
        out_shape=out_shapes,
        compiler_params=_params(2),
        name="in_lat",
    )(x, mods, g_pre, w_in, *_rope_tables())


def _in_proj_ctx(ctx, mods_ctx, g_pre, w_in):
    batch = ctx.shape[0]
    tok = lambda b: (b, 0, 0)
    out_shapes = [jax.ShapeDtypeStruct((batch, GROUP_W, CTX_LEN) if gi in _K_GROUPS else (batch, CTX_LEN, GROUP_W),
                                       BF16) for gi in _CTX_GROUPS]
    out_specs = [pl.BlockSpec((CTX_BATCHES, GROUP_W, CTX_LEN) if gi in _K_GROUPS else (CTX_BATCHES, CTX_LEN, GROUP_W),
                              tok) for gi in _CTX_GROUPS]
    return pl.pallas_call(
        _in_ctx_kernel,
        grid=(batch // CTX_BATCHES,),
        in_specs=[pl.BlockSpec((CTX_BATCHES, CTX_LEN, D_MODEL), tok),
                  pl.BlockSpec((1, N_MOD, D_MODEL), lambda b: (0, 0, 0)),
                  pl.BlockSpec((1, D_MODEL), lambda b: (0, 0)),
                  _resident((D_MODEL, N_GROUPS * GROUP_W))],
        out_specs=out_specs,
        out_shape=out_shapes,
        compiler_params=_params(1),
        name="in_ctx",
    )(ctx, mods_ctx, g_pre, w_in)


def _ret_kernel(lg_ref, q_ref, kt_ref, v_ref, g_ref, kct_ref, vc_ref, gn_ref, o_ref,
                dec_ref, wq_ref, wk_ref, upd_ref, st_ref, sp_ref):
    head = pl.program_id(0)
    lgf = lg_ref[0, head]
    lgb = lg_ref[1, head]
    c = RET_CHUNK
    n_chunks = SEQ // c

    @pl.when(pl.program_id(1) == 0)
    def _tables():
        ii = lax.broadcasted_iota(jnp.int32, (c, c), 0)
        jj = lax.broadcasted_iota(jnp.int32, (c, c), 1)
        diff = (ii - jj).astype(F32)
        dec_ref[...] = jnp.exp(jnp.where(diff >= 0, lgf * diff, -lgb * diff))
        ri = lax.broadcasted_iota(jnp.int32, (c, LANES), 0).astype(F32)
        wq_ref[0] = jnp.exp(lgf * (ri + 1.0))
        wq_ref[1] = jnp.exp(lgb * (c - ri))
        ti = lax.broadcasted_iota(jnp.int32, (1, c), 1).astype(F32)
        wk_ref[0] = jnp.exp(lgf * (c - 1.0 - ti))
        wk_ref[1] = jnp.exp(lgb * ti)

    ones = jnp.ones((1, LANES), F32)
    chunk_f = jnp.exp(ones * (lgf * c))
    chunk_b = jnp.exp(ones * (lgb * c))

    def update(kt, v):
        ktf = kt.astype(F32)
        kw = jnp.concatenate([(ktf * wk_ref[0]).astype(BF16), (ktf * wk_ref[1]).astype(BF16)], axis=0)
        return jnp.dot(kw, v, preferred_element_type=F32)

    def chunk(n):
        return slice(n * c, (n + 1) * c)

    batches = range(RET_BATCHES)
    for b in batches:
        for n in range(n_chunks):
            upd_ref[b, n] = update(kt_ref[b, :, chunk(n)], v_ref[b, chunk(n), :])

    for b in batches:
        ctx_upd = update(kct_ref[b], vc_ref[b])
        state = ctx_upd[:RET_DIM]
        for n in range(n_chunks):
            st_ref[b, n, 0:RET_DIM, :] = state.astype(BF16)
            state = chunk_f * state + upd_ref[b, n, 0:RET_DIM, :]
        state = ctx_upd[RET_DIM:]
        for n in reversed(range(n_chunks)):
            st_ref[b, n, RET_DIM:, :] = state.astype(BF16)
            state = chunk_b * state + upd_ref[b, n, RET_DIM:, :]

    def scores(b, n):
        s = jnp.dot(q_ref[b, chunk(n), :], kt_ref[b, :, chunk(n)], preferred_element_type=F32)
        sp_ref[b, n % 2] = (s * dec_ref[...]).astype(BF16)

    def outputs(b, n):
        rows = chunk(n)
        qf = q_ref[b, rows, :].astype(F32)
        qw = jnp.concatenate([(qf * wq_ref[0]).astype(BF16), (qf * wq_ref[1]).astype(BF16)], axis=1)
        o = (jnp.dot(sp_ref[b, n % 2], v_ref[b, rows, :], preferred_element_type=F32)
             + jnp.dot(qw, st_ref[b, n], preferred_element_type=F32))
        d = o - jnp.mean(o, axis=-1, keepdims=True)
        y = d * lax.rsqrt(jnp.mean(d * d, axis=-1, keepdims=True) + NORM_EPS) * gn_ref[...]
        o_ref[b, rows, :] = (y * _silu(g_ref[b, rows, :].astype(F32))).astype(BF16)

    for b in batches:
        scores(b, 0)
    for n in range(n_chunks):
        for b in batches:
            if n + 1 < n_chunks:
                scores(b, n + 1)
        for b in batches:
            outputs(b, n)


def _retention(log_gammas, rq, rkt, rv, rg, crkt, crv, gn_w):
    assert CTX_LEN == RET_CHUNK
    batch = rq.shape[0]
    n_chunks = SEQ // RET_CHUNK
    nb = RET_BATCHES
    lat = pl.BlockSpec((nb, SEQ, LANES), lambda h, b: (b, 0, h))
    cx = pl.BlockSpec((nb, CTX_LEN, LANES), lambda h, b: (b, 0, h))
    return pl.pallas_call(
        _ret_kernel,
        grid=(RET_HEADS, batch // nb),
        in_specs=[pl.BlockSpec(memory_space=pltpu.SMEM), lat,
                  pl.BlockSpec((nb, LANES, SEQ), lambda h, b: (b, h, 0)), lat, lat,
                  pl.BlockSpec((nb, LANES, CTX_LEN), lambda h, b: (b, h, 0)), cx,
                  pl.BlockSpec((1, LANES), lambda h, b: (0, h))],
        out_specs=lat,
        out_shape=jax.ShapeDtypeStruct((batch, SEQ, RET_WIDTH), BF16),
        scratch_shapes=[pltpu.VMEM((RET_CHUNK, RET_CHUNK), F32),
                        pltpu.VMEM((2, RET_CHUNK, LANES), F32),
                        pltpu.VMEM((2, 1, RET_CHUNK), F32),
                        pltpu.VMEM((nb, n_chunks, 2 * RET_DIM, RET_DIM), F32),
                        pltpu.VMEM((nb, n_chunks, 2 * RET_DIM, RET_DIM), BF16),
                        pltpu.VMEM((nb, 2, RET_CHUNK, RET_CHUNK), BF16)],
        compiler_params=_params(2),
        name="ret",
    )(log_gammas, rq, rkt, rv, rg, crkt, crv, gn_w)


N_DR = 2 * NA_KH - 1
N_DC = 2 * NA_KW - 1
PAIR_ROWS = 2 * GRID_W
NA_GROUP = 2
NA_BATCHES = 2


def _na_build_bias(rpb_ref, bias_ref, pair):
    qi = lax.broadcasted_iota(jnp.int32, (GRID_W, LANES), 0)
    li = lax.broadcasted_iota(jnp.int32, (GRID_W, LANES), 1)
    key_col = li & (GRID_W - 1)
    col_start = jnp.clip(qi - NA_KW // 2, 0, GRID_W - NA_KW)
    valid = (key_col >= col_start) & (key_col < col_start + NA_KW)
    code = jnp.where(valid, key_col - qi + (NA_KW - 1), -1)
    first_row = lax.broadcasted_iota(jnp.int32, (1, LANES), 1) < GRID_W

    def pattern(t, carry):
        for hd in range(2):
            head = 2 * pair + hd
            for blk in range(NA_KH // 2):
                base = (head * N_DR + (2 * blk + NA_KH - 1 - t)) * N_DC
                acc = jnp.full((GRID_W, LANES), NEG_INF, F32)
                for b in range(N_DC):
                    val = jnp.where(first_row, rpb_ref[base + b], rpb_ref[base + N_DC + b]) * LOG2_E
                    acc = jnp.where(code == b, val, acc)
                bias_ref[t, hd * GRID_W:(hd + 1) * GRID_W, blk * LANES:(blk + 1) * LANES] = acc
        return carry

    lax.fori_loop(0, N_PATTERNS, pattern, 0)


def _na_kernel(rpb_ref, q2_ref, kt_ref, v_ref, kct_ref, vc_ref, o_ref,
               bias_ref, kt2_ref, s_ref):
    pair = pl.program_id(0)
    low = lax.broadcasted_iota(jnp.int32, (1, LANES), 1) < NA_DIM
    half = NA_KH // 2

    @pl.when(pl.program_id(1) == 0)
    def _bias():
        _na_build_bias(rpb_ref, bias_ref, pair)

    batches = range(NA_BATCHES)
    for bb in batches:
        kt2_ref[bb, 0] = kt_ref[bb]
        kt2_ref[bb, 1, :, 0:SEQ - LANES] = kt_ref[bb, :, GRID_W:SEQ - GRID_W]
        kt2_ref[bb, 1, :, SEQ - LANES:SEQ] = jnp.zeros((LANES, LANES), BF16)

    def band_start(r):
        return jnp.clip(r - half, 0, GRID_ROWS - NA_KH)

    def scores(bb, group, slot):
        rows = group * (NA_GROUP * PAIR_ROWS)
        group_q = pl.ds(pl.multiple_of(rows, NA_GROUP * PAIR_ROWS), NA_GROUP * PAIR_ROWS)
        s_ref[bb, slot, :, NA_BAND:] = jnp.dot(q2_ref[bb, 0, group_q, :], kct_ref[bb], preferred_element_type=F32)
        for j in range(NA_GROUP):
            r = group * NA_GROUP + j
            start = band_start(r)
            pattern = jnp.where(r < half, r, jnp.where(r > GRID_ROWS - half, r - (GRID_ROWS - NA_KH), half))
            lanes = pl.ds(pl.multiple_of((start >> 1) * LANES, LANES), NA_BAND)
            kb = kt2_ref[bb, start & 1, :, lanes]
            q2 = q2_ref[bb, 0, pl.ds(pl.multiple_of(r * PAIR_ROWS, PAIR_ROWS), PAIR_ROWS), :]
            s_ref[bb, slot, j * PAIR_ROWS:(j + 1) * PAIR_ROWS, 0:NA_BAND] = (
                jnp.dot(q2, kb, preferred_element_type=F32) + bias_ref[pattern])

    n_blocks = (NA_BAND + CTX_LEN) // LANES

    def outputs(bb, group, slot):
        probs = []
        for j in range(NA_GROUP):
            rows = slice(j * PAIR_ROWS, (j + 1) * PAIR_ROWS)

            def block(b):
                return s_ref[bb, slot, rows, b * LANES:(b + 1) * LANES]

            m = block(0)
            for b in range(1, n_blocks):
                m = jnp.maximum(m, block(b))
            m = jnp.max(m, axis=-1, keepdims=True)
            p = [jnp.exp2(block(b) - m) for b in range(n_blocks)]
            den = p[0]
            for pb in p[1:]:
                den = den + pb
            den = jnp.sum(den, axis=-1, keepdims=True)
            probs.append((jnp.concatenate([pb.astype(BF16) for pb in p], axis=1), den))
        ctx = jnp.dot(jnp.concatenate([p[:, NA_BAND:] for p, _ in probs], axis=0), vc_ref[bb],
                      preferred_element_type=F32)
        for j, (p, den) in enumerate(probs):
            r = group * NA_GROUP + j
            vb = v_ref[bb, pl.ds(pl.multiple_of(band_start(r) * GRID_W, GRID_W), NA_BAND), :]
            o = (jnp.dot(p[:, 0:NA_BAND], vb, preferred_element_type=F32)
                 + ctx[j * PAIR_ROWS:(j + 1) * PAIR_ROWS]) * (1.0 / den)
            o_ref[bb, pl.ds(pl.multiple_of(r * GRID_W, GRID_W), GRID_W), :] = (
                jnp.where(low, o[:GRID_W], o[GRID_W:]).astype(BF16))

    n_groups = GRID_ROWS // NA_GROUP
    for bb in batches:
        scores(bb, 0, 0)
    for g in range(n_groups):
        for bb in batches:
            if g + 1 < n_groups:
                scores(bb, g + 1, (g + 1) % 2)
        for bb in batches:
            outputs(bb, g, g % 2)


def _neighbourhood_attention(rpb, nq, nkt, nv, cnkt, cnv):
    batch = nq.shape[0]
    n_q2 = GRID_ROWS * PAIR_ROWS
    nb = NA_BATCHES
    lat = pl.BlockSpec((nb, SEQ, LANES), lambda p, b: (b, 0, p))
    cx = pl.BlockSpec((nb, CTX_LEN, LANES), lambda p, b: (b, 0, p))
    return pl.pallas_call(
        _na_kernel,
        grid=(NA_HEADS // 2, batch // nb),
        in_specs=[pl.BlockSpec(memory_space=pltpu.SMEM),
                  pl.BlockSpec((nb, 1, n_q2, LANES), lambda p, b: (b, p, 0, 0)),
                  pl.BlockSpec((nb, LANES, SEQ), lambda p, b: (b, p, 0)), lat,
                  pl.BlockSpec((nb, LANES, CTX_LEN), lambda p, b: (b, p, 0)), cx],
        out_specs=lat,
        out_shape=jax.ShapeDtypeStruct((batch, SEQ, NA_WIDTH), BF16),
        scratch_shapes=[pltpu.VMEM((N_PATTERNS, PAIR_ROWS, NA_BAND), F32),
                        pltpu.VMEM((nb, 2, LANES, SEQ), BF16),
                        pltpu.VMEM((nb, 2, NA_GROUP * PAIR_ROWS, NA_BAND + CTX_LEN), F32)],
        compiler_params=_params(2),
        name="na",
    )(rpb, nq, nkt, nv, cnkt, cnv)


def _out_kernel(x_ref, ret_ref, na_ref, mod_ref, gpm_ref, gpre_ref, gpost_ref, wo_ref, w1_ref, w2_ref, o_ref):
    gt1, sh2, sc2, gt2 = (mod_ref[0, i:i + 1, :] for i in (2, 3, 4, 5))
    subs = [slice(i * SUB_TILE, (i + 1) * SUB_TILE) for i in range(OUT_TILE // SUB_TILE)]
    mix = [jnp.dot(ret_ref[0, rows, :], wo_ref[0:RET_WIDTH, :], preferred_element_type=F32)
           + jnp.dot(na_ref[0, rows, :], wo_ref[RET_WIDTH:, :], preferred_element_type=F32) for rows in subs]
    h2 = []
    for rows, m in zip(subs, mix):
        x1 = x_ref[0, rows, :] + gt1 * _rms(m, gpm_ref[...])
        o_ref[0, rows, :] = x1
        h2.append(_prenorm(x1, gpre_ref[...], sh2, sc2))
    for rows, h in zip(subs, h2):
        acc = jnp.zeros((SUB_TILE, D_MODEL), F32)
        for ci in range(D_FF // FF_CHUNK):
            cols = slice(ci * FF_CHUNK, (ci + 1) * FF_CHUNK)
            a = jnp.maximum(jnp.dot(h, w1_ref[:, cols], preferred_element_type=F32), 0.0)
            acc = acc + jnp.dot((a * a).astype(BF16), w2_ref[cols, :], preferred_element_type=F32)
        o_ref[0, rows, :] = o_ref[0, rows, :] + gt2 * _rms(acc, gpost_ref[...])


def _out_proj_mlp(x, ret_lat, na_lat, mods, g_post_mix, g_pre_mlp, g_post_mlp, w_out, w_mlp1, w_mlp2):
    batch = x.shape[0]
    tok = lambda b, i: (b, i, 0)
    gain = pl.BlockSpec((1, D_MODEL), lambda b, i: (0, 0))
    return pl.pallas_call(
        _out_kernel,
        grid=(batch, SEQ // OUT_TILE),
        in_specs=[pl.BlockSpec((1, OUT_TILE, D_MODEL), tok),
                  pl.BlockSpec((1, OUT_TILE, RET_WIDTH), tok),
                  pl.BlockSpec((1, OUT_TILE, NA_WIDTH), tok),
                  pl.BlockSpec((1, N_MOD, D_MODEL), lambda b, i: (b, 0, 0)),
                  gain, gain, gain,
                  _resident((RET_WIDTH + NA_WIDTH, D_MODEL)),
                  _resident((D_MODEL, D_FF)),
                  _resident((D_FF, D_MODEL))],
        out_specs=pl.BlockSpec((1, OUT_TILE, D_MODEL), tok),
        out_shape=jax.ShapeDtypeStruct(x.shape, F32),
        compiler_params=_params(2),
        name="out_mlp",
    )(x, ret_lat, na_lat, mods, g_post_mix, g_pre_mlp, g_post_mlp, w_out, w_mlp1, w_mlp2)


def kernel(x, c, ctx, c_ctx, w_ada, b_ada, g_pre_mix, g_post_mix, g_pre_mlp, g_post_mlp,
           w_in, ret_decay, ret_gn, na_rpb, w_out, w_mlp1, w_mlp2):
    assert w_in.shape[0] == 1, "single-layer block: the context stream is never updated"
    batch = x.shape[0]
    pad = (-(batch + 1)) % 8
    cc = jnp.concatenate([c, c_ctx[None, :], jnp.zeros((pad, D_MODEL), F32)], axis=0)
    mods = _modulations(cc, w_ada[0], b_ada)
    mods_lat = mods[:batch].reshape(batch, N_MOD, D_MODEL)
    mods_ctx = mods[batch:batch + 1].reshape(1, N_MOD, D_MODEL)

    rq, rk, rv, rg, nq, nkt, nv = _in_proj_lat(x, mods_lat, g_pre_mix, w_in[0])
    crk, crv, cnkt, cnv = _in_proj_ctx(ctx, mods_ctx, g_pre_mix, w_in[0])

    log_gammas = jax.nn.log_sigmoid(ret_decay[0].astype(F32))
    ret_lat = _retention(log_gammas, rq, rk, rv, rg, crk, crv, ret_gn)
    na_lat = _neighbourhood_attention(na_rpb[0].astype(F32).reshape(-1), nq, nkt, nv, cnkt, cnv)

    return _out_proj_mlp(x, ret_lat, na_lat, mods_lat, g_post_mix, g_pre_mlp, g_post_mlp,
                         w_out[0].astype(BF16), w_mlp1[0].astype(BF16), w_mlp2[0].astype(BF16))
```

```python
import numpy as np
import jax
import jax.numpy as jnp
from jax import lax
from jax.experimental import pallas as pl
from jax.experimental.pallas import tpu as pltpu

D_MODEL = 1024
SEQ = 2048
CTX_LEN = 256
GRID_W = 64
GRID_ROWS = SEQ // GRID_W
RET_HEADS = 4
RET_DIM = 128
RET_WIDTH = RET_HEADS * RET_DIM
NA_HEADS = 8
NA_DIM = 64
NA_WIDTH = NA_HEADS * NA_DIM
NA_KH = 8
NA_KW = 16
N_GROUPS = 7
GROUP_W = 512
_K_GROUPS = (1, 5)
D_FF = 4 * D_MODEL
ROPE_BASE = 10000.0
NORM_EPS = 1e-6
N_MOD = 6
NEG_INF = -1e30
LOG2_E = 1.4426950408889634

LANES = 128
ROW_TILE = 1024
CTX_BATCHES = 2
OUT_TILE = 1024
SUB_TILE = 512
FF_CHUNK = 1024
RET_CHUNK = 256
RET_BATCHES = 2
NA_BAND = NA_KH * GRID_W
N_PATTERNS = 8
VMEM_LIMIT = 56 * 1024 * 1024

F32 = jnp.float32
BF16 = jnp.bfloat16


def _silu(x):
    return x * (1.0 / (1.0 + jnp.exp(-x)))


def _rms(x, g):
    return x * lax.rsqrt(jnp.mean(x * x, axis=-1, keepdims=True) + NORM_EPS) * g


def _params(n_axes):
    return pltpu.CompilerParams(dimension_semantics=("arbitrary",) * n_axes,
                                vmem_limit_bytes=VMEM_LIMIT)


def _resident(shape):
    nd = len(shape)
    return pl.BlockSpec(shape, lambda *_: (0,) * nd, pipeline_mode=pl.Buffered(1))


def _mod_kernel(c_ref, w_ref, b_ref, o_ref):
    a = _silu(c_ref[...]).astype(BF16)
    o_ref[...] = jnp.dot(a, w_ref[...].astype(BF16), preferred_element_type=F32) + b_ref[...]


def _modulations(cc, w_ada, b_ada):
    rows = cc.shape[0]
    return pl.pallas_call(
        _mod_kernel,
        grid=(N_MOD,),
        in_specs=[pl.BlockSpec((rows, D_MODEL), lambda j: (0, 0)),
                  pl.BlockSpec((D_MODEL, D_MODEL), lambda j: (0, j)),
                  pl.BlockSpec((1, D_MODEL), lambda j: (0, j))],
        out_specs=pl.BlockSpec((rows, D_MODEL), lambda j: (0, j)),
        out_shape=jax.ShapeDtypeStruct((rows, N_MOD * D_MODEL), F32),
        compiler_params=_params(1),
        name="mod",
    )(cc, w_ada, b_ada)


def _prenorm(x, g, shift, scale):
    return (_rms(x, g) * (1.0 + scale) + shift).astype(BF16)


def _rope(blk, cos, sin_signed, first_half):
    partner = jnp.where(first_half, pltpu.roll(blk, LANES - 32, 1), pltpu.roll(blk, 32, 1))
    return blk * cos + partner * sin_signed


def _in_lat_kernel(x_ref, mod_ref, g_ref, w_ref, cq_ref, sq_ref, ck_ref, sk_ref, *out_refs):
    h = _prenorm(x_ref[0], g_ref[...], mod_ref[0, 0:1, :], mod_ref[0, 1:2, :])
    lane = lax.broadcasted_iota(jnp.int32, (1, LANES), 1)
    first_half = (lane & 32) == 0
    for gi, o_ref in enumerate(out_refs):
        w = w_ref[:, gi * GROUP_W:(gi + 1) * GROUP_W].astype(BF16)
        acc = jnp.dot(h, w, preferred_element_type=F32)
        if gi == 0:
            for hh in range(RET_HEADS):
                blk = _rope(acc[:, hh * LANES:(hh + 1) * LANES], cq_ref[...], sq_ref[...], first_half)
                o_ref[0, :, hh * LANES:(hh + 1) * LANES] = blk.astype(BF16)
        elif gi == 1:
            for hh in range(RET_HEADS):
                blk = _rope(acc[:, hh * LANES:(hh + 1) * LANES], ck_ref[...], sk_ref[...], first_half)
                o_ref[0, hh * LANES:(hh + 1) * LANES, :] = blk.T.astype(BF16)
        elif gi == 4:
            q = (acc * (NA_DIM ** -0.5 * LOG2_E)).astype(BF16)
            zero = jnp.zeros((ROW_TILE, LANES), BF16)
            for pair in range(NA_HEADS // 2):
                qp = q[:, pair * LANES:(pair + 1) * LANES]
                first = jnp.where(lane < NA_DIM, qp, zero)
                second = jnp.where(lane < NA_DIM, zero, qp)
                for rr in range(ROW_TILE // GRID_W):
                    tokens = slice(rr * GRID_W, (rr + 1) * GRID_W)
                    o_ref[0, pair, rr * PAIR_ROWS:rr * PAIR_ROWS + GRID_W, :] = first[tokens]
                    o_ref[0, pair, rr * PAIR_ROWS + GRID_W:(rr + 1) * PAIR_ROWS, :] = second[tokens]
        elif gi in _K_GROUPS:
            o_ref[0] = acc.T.astype(BF16)
        else:
            o_ref[0] = acc.astype(BF16)


_CTX_GROUPS = (1, 2, 5, 6)


def _in_ctx_kernel(x_ref, mod_ref, g_ref, w_ref, *out_refs):
    x = jnp.concatenate([x_ref[i] for i in range(CTX_BATCHES)], axis=0)
    h = _prenorm(x, g_ref[...], mod_ref[0, 0:1, :], mod_ref[0, 1:2, :])
    for gi, o_ref in zip(_CTX_GROUPS, out_refs):
        w = w_ref[:, gi * GROUP_W:(gi + 1) * GROUP_W].astype(BF16)
        acc = jnp.dot(h, w, preferred_element_type=F32)
        acc = (acc.T if gi in _K_GROUPS else acc).astype(BF16)
        for i in range(CTX_BATCHES):
            tokens = slice(i * CTX_LEN, (i + 1) * CTX_LEN)
            o_ref[i] = acc[:, tokens] if gi in _K_GROUPS else acc[tokens, :]


def _rope_tables():
    tok = np.arange(SEQ)
    n_freq = RET_DIM // 4
    inv = ROPE_BASE ** (-np.arange(n_freq, dtype=np.float64) / n_freq)
    ang_r = (tok // GRID_W)[:, None] * inv[None, :]
    ang_c = (tok % GRID_W)[:, None] * inv[None, :]
    cos = np.concatenate([np.cos(ang_r), np.cos(ang_r), np.cos(ang_c), np.cos(ang_c)], axis=-1)
    sin = np.concatenate([-np.sin(ang_r), np.sin(ang_r), -np.sin(ang_c), np.sin(ang_c)], axis=-1)
    qs = RET_DIM ** -0.5
    return [jnp.asarray(t, F32) for t in (cos * qs, sin * qs, cos, sin)]


def _in_proj_lat(x, mods, g_pre, w_in):
    batch = x.shape[0]
    tok = lambda b, i: (b, i, 0)
    tab = pl.BlockSpec((ROW_TILE, LANES), lambda b, i: (i, 0))
    out_shapes = [jax.ShapeDtypeStruct((batch, SEQ, GROUP_W), BF16)] * N_GROUPS
    out_specs = [pl.BlockSpec((1, ROW_TILE, GROUP_W), tok)] * N_GROUPS
    for gi in _K_GROUPS:
        out_shapes[gi] = jax.ShapeDtypeStruct((batch, GROUP_W, SEQ), BF16)
        out_specs[gi] = pl.BlockSpec((1, GROUP_W, ROW_TILE), lambda b, i: (b, 0, i))
    n_pairs = NA_HEADS // 2
    out_shapes[4] = jax.ShapeDtypeStruct((batch, n_pairs, GRID_ROWS * PAIR_ROWS, LANES), BF16)
    out_specs[4] = pl.BlockSpec((1, n_pairs, ROW_TILE // GRID_W * PAIR_ROWS, LANES), lambda b, i: (b, 0, i, 0))
    return pl.pallas_call(
        _in_lat_kernel,
        grid=(batch, SEQ // ROW_TILE),
        in_specs=[pl.BlockSpec((1, ROW_TILE, D_MODEL), tok),
                  pl.BlockSpec((1, N_MOD, D_MODEL), lambda b, i: (b, 0, 0)),
                  pl.BlockSpec((1, D_MODEL), lambda b, i: (0, 0)),
                  _resident((D_MODEL, N_GROUPS * GROUP_W)),
                  tab, tab, tab, tab],
        out_specs=out_specs,
        out_shape=out_shapes,
        compiler_params=_params(2),
        name="in_lat",
    )(x, mods, g_pre, w_in, *_rope_tables())


def _in_proj_ctx(ctx, mods_ctx, g_pre, w_in):
    batch = ctx.shape[0]
    tok = lambda b: (b, 0, 0)
    out_shapes = [jax.ShapeDtypeStruct((batch, GROUP_W, CTX_LEN) if gi in _K_GROUPS else (batch, CTX_LEN, GROUP_W),
                                       BF16) for gi in _CTX_GROUPS]
    out_specs = [pl.BlockSpec((CTX_BATCHES, GROUP_W, CTX_LEN) if gi in _K_GROUPS else (CTX_BATCHES, CTX_LEN, GROUP_W),
                              tok) for gi in _CTX_GROUPS]
    return pl.pallas_call(
        _in_ctx_kernel,
        grid=(batch // CTX_BATCHES,),
        in_specs=[pl.BlockSpec((CTX_BATCHES, CTX_LEN, D_MODEL), tok),
                  pl.BlockSpec((1, N_MOD, D_MODEL), lambda b: (0, 0, 0)),
                  pl.BlockSpec((1, D_MODEL), lambda b: (0, 0)),
                  _resident((D_MODEL, N_GROUPS * GROUP_W))],
        out_specs=out_specs,
        out_shape=out_shapes,
        compiler_params=_params(1),
        name="in_ctx",
    )(ctx, mods_ctx, g_pre, w_in)


def _ret_kernel(lg_ref, q_ref, kt_ref, v_ref, g_ref, kct_ref, vc_ref, gn_ref, o_ref,
                dec_ref, wq_ref, wk_ref, upd_ref, st_ref, sp_ref):
    head = pl.program_id(0)
    lgf = lg_ref[0, head]
    lgb = lg_ref[1, head]
    c = RET_CHUNK
    n_chunks = SEQ // c

    @pl.when(pl.program_id(1) == 0)
    def _tables():
        ii = lax.broadcasted_iota(jnp.int32, (c, c), 0)
        jj = lax.broadcasted_iota(jnp.int32, (c, c), 1)
        diff = (ii - jj).astype(F32)
        dec_ref[...] = jnp.exp(jnp.where(diff >= 0, lgf * diff, -lgb * diff))
        ri = lax.broadcasted_iota(jnp.int32, (c, LANES), 0).astype(F32)
        wq_ref[0] = jnp.exp(lgf * (ri + 1.0))
        wq_ref[1] = jnp.exp(lgb * (c - ri))
        ti = lax.broadcasted_iota(jnp.int32, (1, c), 1).astype(F32)
        wk_ref[0] = jnp.exp(lgf * (c - 1.0 - ti))
        wk_ref[1] = jnp.exp(lgb * ti)

    ones = jnp.ones((1, LANES), F32)
    chunk_f = jnp.exp(ones * (lgf * c))
    chunk_b = jnp.exp(ones * (lgb * c))

    def update(kt, v):
        ktf = kt.astype(F32)
        kw = jnp.concatenate([(ktf * wk_ref[0]).astype(BF16), (ktf * wk_ref[1]).astype(BF16)], axis=0)
        return jnp.dot(kw, v, preferred_element_type=F32)

    def chunk(n):
        return slice(n * c, (n + 1) * c)

    batches = range(RET_BATCHES)
    for b in batches:
        for n in range(n_chunks):
            upd_ref[b, n] = update(kt_ref[b, :, chunk(n)], v_ref[b, chunk(n), :])

    for b in batches:
        ctx_upd = update(kct_ref[b], vc_ref[b])
        state = ctx_upd[:RET_DIM]
        for n in range(n_chunks):
            st_ref[b, n, 0:RET_DIM, :] = state.astype(BF16)
            state = chunk_f * state + upd_ref[b, n, 0:RET_DIM, :]
        state = ctx_upd[RET_DIM:]
        for n in reversed(range(n_chunks)):
            st_ref[b, n, RET_DIM:, :] = state.astype(BF16)
            state = chunk_b * state + upd_ref[b, n, RET_DIM:, :]

    def scores(b, n):
        s = jnp.dot(q_ref[b, chunk(n), :], kt_ref[b, :, chunk(n)], preferred_element_type=F32)
        sp_ref[b, n % 2] = (s * dec_ref[...]).astype(BF16)

    def outputs(b, n):
        rows = chunk(n)
        qf = q_ref[b, rows, :].astype(F32)
        qw = jnp.concatenate([(qf * wq_ref[0]).astype(BF16), (qf * wq_ref[1]).astype(BF16)], axis=1)
        o = (jnp.dot(sp_ref[b, n % 2], v_ref[b, rows, :], preferred_element_type=F32)
             + jnp.dot(qw, st_ref[b, n], preferred_element_type=F32))
        d = o - jnp.mean(o, axis=-1, keepdims=True)
        y = d * lax.rsqrt(jnp.mean(d * d, axis=-1, keepdims=True) + NORM_EPS) * gn_ref[...]
        o_ref[b, rows, :] = (y * _silu(g_ref[b, rows, :].astype(F32))).astype(BF16)

    for b in batches:
        scores(b, 0)
    for n in range(n_chunks):
        for b in batches:
            if n + 1 < n_chunks:
                scores(b, n + 1)
        for b in batches:
            outputs(b, n)


def _retention(log_gammas, rq, rkt, rv, rg, crkt, crv, gn_w):
    assert CTX_LEN == RET_CHUNK
    batch = rq.shape[0]
    n_chunks = SEQ // RET_CHUNK
    nb = RET_BATCHES
    lat = pl.BlockSpec((nb, SEQ, LANES), lambda h, b: (b, 0, h))
    cx = pl.BlockSpec((nb, CTX_LEN, LANES), lambda h, b: (b, 0, h))
    return pl.pallas_call(
        _ret_kernel,
        grid=(RET_HEADS, batch // nb),
        in_specs=[pl.BlockSpec(memory_space=pltpu.SMEM), lat,
                  pl.BlockSpec((nb, LANES, SEQ), lambda h, b: (b, h, 0)), lat, lat,
                  pl.BlockSpec((nb, LANES, CTX_LEN), lambda h, b: (b, h, 0)), cx,
                  pl.BlockSpec((1, LANES), lambda h, b: (0, h))],
        out_specs=lat,
        out_shape=jax.ShapeDtypeStruct((batch, SEQ, RET_WIDTH), BF16),
        scratch_shapes=[pltpu.VMEM((RET_CHUNK, RET_CHUNK), F32),
                        pltpu.VMEM((2, RET_CHUNK, LANES), F32),
                        pltpu.VMEM((2, 1, RET_CHUNK), F32),
                        pltpu.VMEM((nb, n_chunks, 2 * RET_DIM, RET_DIM), F32),
                        pltpu.VMEM((nb, n_chunks, 2 * RET_DIM, RET_DIM), BF16),
                        pltpu.VMEM((nb, 2, RET_CHUNK, RET_CHUNK), BF16)],
        compiler_params=_params(2),
        name="ret",
    )(log_gammas, rq, rkt, rv, rg, crkt, crv, gn_w)


N_DR = 2 * NA_KH - 1
N_DC = 2 * NA_KW - 1
PAIR_ROWS = 2 * GRID_W
NA_GROUP = 2
NA_BATCHES = 2


def _na_build_bias(rpb_ref, bias_ref, pair):
    qi = lax.broadcasted_iota(jnp.int32, (GRID_W, LANES), 0)
    li = lax.broadcasted_iota(jnp.int32, (GRID_W, LANES), 1)
    key_col = li & (GRID_W - 1)
    col_start = jnp.clip(qi - NA_KW // 2, 0, GRID_W - NA_KW)
    valid = (key_col >= col_start) & (key_col < col_start + NA_KW)
    code = jnp.where(valid, key_col - qi + (NA_KW - 1), -1)
    first_row = lax.broadcasted_iota(jnp.int32, (1, LANES), 1) < GRID_W

    def pattern(t, carry):
        for hd in range(2):
            head = 2 * pair + hd
            for blk in range(NA_KH // 2):
                base = (head * N_DR + (2 * blk + NA_KH - 1 - t)) * N_DC
                acc = jnp.full((GRID_W, LANES), NEG_INF, F32)
                for b in range(N_DC):
                    val = jnp.where(first_row, rpb_ref[base + b], rpb_ref[base + N_DC + b]) * LOG2_E
                    acc = jnp.where(code == b, val, acc)
                bias_ref[t, hd * GRID_W:(hd + 1) * GRID_W, blk * LANES:(blk + 1) * LANES] = acc
        return carry

    lax.fori_loop(0, N_PATTERNS, pattern, 0)


def _na_kernel(rpb_ref, q2_ref, kt_ref, v_ref, kct_ref, vc_ref, o_ref,
               bias_ref, kt2_ref, s_ref):
    pair = pl.program_id(0)
    low = lax.broadcasted_iota(jnp.int32, (1, LANES), 1) < NA_DIM
    half = NA_KH // 2

    @pl.when(pl.program_id(1) == 0)
    def _bias():
        _na_build_bias(rpb_ref, bias_ref, pair)

    batches = range(NA_BATCHES)
    for bb in batches:
        kt2_ref[bb, 0] = kt_ref[bb]
        kt2_ref[bb, 1, :, 0:SEQ - LANES] = kt_ref[bb, :, GRID_W:SEQ - GRID_W]
        kt2_ref[bb, 1, :, SEQ - LANES:SEQ] = jnp.zeros((LANES, LANES), BF16)

    def band_start(r):
        return jnp.clip(r - half, 0, GRID_ROWS - NA_KH)

    def scores(bb, group, slot):
        rows = group * (NA_GROUP * PAIR_ROWS)
        group_q = pl.ds(pl.multiple_of(rows, NA_GROUP * PAIR_ROWS), NA_GROUP * PAIR_ROWS)
        s_ref[bb, slot, :, NA_BAND:] = jnp.dot(q2_ref[bb, 0, group_q, :], kct_ref[bb], preferred_element_type=F32)
        for j in range(NA_GROUP):
            r = group * NA_GROUP + j
            start = band_start(r)
            pattern = jnp.where(r < half, r, jnp.where(r > GRID_ROWS - half, r - (GRID_ROWS - NA_KH), half))
            lanes = pl.ds(pl.multiple_of((start >> 1) * LANES, LANES), NA_BAND)
            kb = kt2_ref[bb, start & 1, :, lanes]
            q2 = q2_ref[bb, 0, pl.ds(pl.multiple_of(r * PAIR_ROWS, PAIR_ROWS), PAIR_ROWS), :]
            s_ref[bb, slot, j * PAIR_ROWS:(j + 1) * PAIR_ROWS, 0:NA_BAND] = (
                jnp.dot(q2, kb, preferred_element_type=F32) + bias_ref[pattern])

    n_blocks = (NA_BAND + CTX_LEN) // LANES

    def outputs(bb, group, slot):
        probs = []
        for j in range(NA_GROUP):
            rows = slice(j * PAIR_ROWS, (j + 1) * PAIR_ROWS)

            def block(b):
                return s_ref[bb, slot, rows, b * LANES:(b + 1) * LANES]

            m = block(0)
            for b in range(1, n_blocks):
                m = jnp.maximum(m, block(b))
            m = jnp.max(m, axis=-1, keepdims=True)
            p = [jnp.exp2(block(b) - m) for b in range(n_blocks)]
            den = p[0]
            for pb in p[1:]:
                den = den + pb
            den = jnp.sum(den, axis=-1, keepdims=True)
            probs.append((jnp.concatenate([pb.astype(BF16) for pb in p], axis=1), den))
        ctx = jnp.dot(jnp.concatenate([p[:, NA_BAND:] for p, _ in probs], axis=0), vc_ref[bb],
                      preferred_element_type=F32)
        for j, (p, den) in enumerate(probs):
            r = group * NA_GROUP + j
            vb = v_ref[bb, pl.ds(pl.multiple_of(band_start(r) * GRID_W, GRID_W), NA_BAND), :]
            o = (jnp.dot(p[:, 0:NA_BAND], vb, preferred_element_type=F32)
                 + ctx[j * PAIR_ROWS:(j + 1) * PAIR_ROWS]) * (1.0 / den)
            o_ref[bb, pl.ds(pl.multiple_of(r * GRID_W, GRID_W), GRID_W), :] = (
                jnp.where(low, o[:GRID_W], o[GRID_W:]).astype(BF16))

    n_groups = GRID_ROWS // NA_GROUP
    for bb in batches:
        scores(bb, 0, 0)
    for g in range(n_groups):
        for bb in batches:
            if g + 1 < n_groups:
                scores(bb, g + 1, (g + 1) % 2)
        for bb in batches:
            outputs(bb, g, g % 2)


def _neighbourhood_attention(rpb, nq, nkt, nv, cnkt, cnv):
    batch = nq.shape[0]
    n_q2 = GRID_ROWS * PAIR_ROWS
    nb = NA_BATCHES
    lat = pl.BlockSpec((nb, SEQ, LANES), lambda p, b: (b, 0, p))
    cx = pl.BlockSpec((nb, CTX_LEN, LANES), lambda p, b: (b, 0, p))
    return pl.pallas_call(
        _na_kernel,
        grid=(NA_HEADS // 2, batch // nb),
        in_specs=[pl.BlockSpec(memory_space=pltpu.SMEM),
                  pl.BlockSpec((nb, 1, n_q2, LANES), lambda p, b: (b, p, 0, 0)),
                  pl.BlockSpec((nb, LANES, SEQ), lambda p, b: (b, p, 0)), lat,
                  pl.BlockSpec((nb, LANES, CTX_LEN), lambda p, b: (b, p, 0)), cx],
        out_specs=lat,
        out_shape=jax.ShapeDtypeStruct((batch, SEQ, NA_WIDTH), BF16),
        scratch_shapes=[pltpu.VMEM((N_PATTERNS, PAIR_ROWS, NA_BAND), F32),
                        pltpu.VMEM((nb, 2, LANES, SEQ), BF16),
                        pltpu.VMEM((nb, 2, NA_GROUP * PAIR_ROWS, NA_BAND + CTX_LEN), F32)],
        compiler_params=_params(2),
        name="na",
    )(rpb, nq, nkt, nv, cnkt, cnv)


def _out_kernel(x_ref, ret_ref, na_ref, mod_ref, gpm_ref, gpre_ref, gpost_ref, wo_ref, w1_ref, w2_ref, o_ref):
    gt1, sh2, sc2, gt2 = (mod_ref[0, i:i + 1, :] for i in (2, 3, 4, 5))
    subs = [slice(i * SUB_TILE, (i + 1) * SUB_TILE) for i in range(OUT_TILE // SUB_TILE)]
    mix = [jnp.dot(ret_ref[0, rows, :], wo_ref[0:RET_WIDTH, :], preferred_element_type=F32)
           + jnp.dot(na_ref[0, rows, :], wo_ref[RET_WIDTH:, :], preferred_element_type=F32) for rows in subs]
    h2 = []
    for rows, m in zip(subs, mix):
        x1 = x_ref[0, rows, :] + gt1 * _rms(m, gpm_ref[...])
        o_ref[0, rows, :] = x1
        h2.append(_prenorm(x1, gpre_ref[...], sh2, sc2))
    for rows, h in zip(subs, h2):
        acc = jnp.zeros((SUB_TILE, D_MODEL), F32)
        for ci in range(D_FF // FF_CHUNK):
            cols = slice(ci * FF_CHUNK, (ci + 1) * FF_CHUNK)
            a = jnp.maximum(jnp.dot(h, w1_ref[:, cols], preferred_element_type=F32), 0.0)
            acc = acc + jnp.dot((a * a).astype(BF16), w2_ref[cols, :], preferred_element_type=F32)
        o_ref[0, rows, :] = o_ref[0, rows, :] + gt2 * _rms(acc, gpost_ref[...])


def _out_proj_mlp(x, ret_lat, na_lat, mods, g_post_mix, g_pre_mlp, g_post_mlp, w_out, w_mlp1, w_mlp2):
    batch = x.shape[0]
    tok = lambda b, i: (b, i, 0)
    gain = pl.BlockSpec((1, D_MODEL), lambda b, i: (0, 0))
    return pl.pallas_call(
        _out_kernel,
        grid=(batch, SEQ // OUT_TILE),
        in_specs=[pl.BlockSpec((1, OUT_TILE, D_MODEL), tok),
                  pl.BlockSpec((1, OUT_TILE, RET_WIDTH), tok),
                  pl.BlockSpec((1, OUT_TILE, NA_WIDTH), tok),
                  pl.BlockSpec((1, N_MOD, D_MODEL), lambda b, i: (b, 0, 0)),
                  gain, gain, gain,
                  _resident((RET_WIDTH + NA_WIDTH, D_MODEL)),
                  _resident((D_MODEL, D_FF)),
                  _resident((D_FF, D_MODEL))],
        out_specs=pl.BlockSpec((1, OUT_TILE, D_MODEL), tok),
        out_shape=jax.ShapeDtypeStruct(x.shape, F32),
        compiler_params=_params(2),
        name="out_mlp",
    )(x, ret_lat, na_lat, mods, g_post_mix, g_pre_mlp, g_post_mlp, w_out, w_mlp1, w_mlp2)


def kernel(x, c, ctx, c_ctx, w_ada, b_ada, g_pre_mix, g_post_mix, g_pre_mlp, g_post_mlp,
           w_in, ret_decay, ret_gn, na_rpb, w_out, w_mlp1, w_mlp2):
    assert w_in.shape[0] == 1, "single-layer block: the context stream is never updated"
    batch = x.shape[0]
    pad = (-(batch + 1)) % 8
    cc = jnp.concatenate([c, c_ctx[None, :], jnp.zeros((pad, D_MODEL), F32)], axis=0)
    mods = _modulations(cc, w_ada[0], b_ada)
    mods_lat = mods[:batch].reshape(batch, N_MOD, D_MODEL)
    mods_ctx = mods[batch:batch + 1].reshape(1, N_MOD, D_MODEL)

    rq, rk, rv, rg, nq, nkt, nv = _in_proj_lat(x, mods_lat, g_pre_mix, w_in[0])
    crk, crv, cnkt, cnv = _in_proj_ctx(ctx, mods_ctx, g_pre_mix, w_in[0])

    log_gammas = jax.nn.log_sigmoid(ret_decay[0].astype(F32))
    ret_lat = _retention(log_gammas, rq, rk, rv, rg, crk, crv, ret_gn)
    na_lat = _neighbourhood_attention(na_rpb[0].astype(F32).reshape(-1), nq, nkt, nv, cnkt, cnv)

    return _out_proj_mlp(x, ret_lat, na_lat, mods_lat, g_post_mix, g_pre_mlp, g_post_mlp,
                         w_out[0].astype(BF16), w_mlp1[0].astype(BF16), w_mlp2[0].astype(BF16))
```

```python
import numpy as np
import jax
import jax.numpy as jnp
from jax import lax
from jax.experimental import pallas as pl
from jax.experimental.pallas import tpu as pltpu

D_MODEL = 1024
SEQ = 2048
CTX_LEN = 256
GRID_W = 64
GRID_ROWS = SEQ // GRID_W
RET_HEADS = 4
RET_DIM = 128
RET_WIDTH = RET_HEADS * RET_DIM
NA_HEADS = 8
NA_DIM = 64
NA_WIDTH = NA_HEADS * NA_DIM
NA_KH = 8
NA_KW = 16
N_GROUPS = 7
GROUP_W = 512
_K_GROUPS = (1, 5)
D_FF = 4 * D_MODEL
ROPE_BASE = 10000.0
NORM_EPS = 1e-6
N_MOD = 6
NEG_INF = -1e30
LOG2_E = 1.4426950408889634

LANES = 128
ROW_TILE = 1024
CTX_BATCHES = 2
OUT_TILE = 1024
SUB_TILE = 256
FF_CHUNK = 1024
RET_CHUNK = 256
RET_BATCHES = 2
NA_BAND = NA_KH * GRID_W
N_PATTERNS = 8
VMEM_LIMIT = 56 * 1024 * 1024

F32 = jnp.float32
BF16 = jnp.bfloat16


def _silu(x):
    return x * (1.0 / (1.0 + jnp.exp(-x)))


def _rms(x, g):
    return x * lax.rsqrt(jnp.mean(x * x, axis=-1, keepdims=True) + NORM_EPS) * g


def _params(n_axes):
    return pltpu.CompilerParams(dimension_semantics=("arbitrary",) * n_axes,
                                vmem_limit_bytes=VMEM_LIMIT)


def _resident(shape):
    nd = len(shape)
    return pl.BlockSpec(shape, lambda *_: (0,) * nd, pipeline_mode=pl.Buffered(1))


def _mod_kernel(c_ref, w_ref, b_ref, o_ref):
    a = _silu(c_ref[...]).astype(BF16)
    o_ref[...] = jnp.dot(a, w_ref[...].astype(BF16), preferred_element_type=F32) + b_ref[...]


def _modulations(cc, w_ada, b_ada):
    rows = cc.shape[0]
    return pl.pallas_call(
        _mod_kernel,
        grid=(N_MOD,),
        in_specs=[pl.BlockSpec((rows, D_MODEL), lambda j: (0, 0)),
                  pl.BlockSpec((D_MODEL, D_MODEL), lambda j: (0, j)),
                  pl.BlockSpec((1, D_MODEL), lambda j: (0, j))],
        out_specs=pl.BlockSpec((rows, D_MODEL), lambda j: (0, j)),
        out_shape=jax.ShapeDtypeStruct((rows, N_MOD * D_MODEL), F32),
        compiler_params=_params(1),
        name="mod",
    )(cc, w_ada, b_ada)


def _prenorm(x, g, shift, scale):
    return (_rms(x, g) * (1.0 + scale) + shift).astype(BF16)


def _rope(blk, cos, sin_signed, first_half):
    partner = jnp.where(first_half, pltpu.roll(blk, LANES - 32, 1), pltpu.roll(blk, 32, 1))
    return blk * cos + partner * sin_signed


def _in_lat_kernel(x_ref, mod_ref, g_ref, w_ref, cq_ref, sq_ref, ck_ref, sk_ref, *out_refs):
    h = _prenorm(x_ref[0], g_ref[...], mod_ref[0, 0:1, :], mod_ref[0, 1:2, :])
    lane = lax.broadcasted_iota(jnp.int32, (1, LANES), 1)
    first_half = (lane & 32) == 0
    for gi, o_ref in enumerate(out_refs):
        w = w_ref[:, gi * GROUP_W:(gi + 1) * GROUP_W].astype(BF16)
        acc = jnp.dot(h, w, preferred_element_type=F32)
        if gi == 0:
            for hh in range(RET_HEADS):
                blk = _rope(acc[:, hh * LANES:(hh + 1) * LANES], cq_ref[...], sq_ref[...], first_half)
                o_ref[0, :, hh * LANES:(hh + 1) * LANES] = blk.astype(BF16)
        elif gi == 1:
            for hh in range(RET_HEADS):
                blk = _rope(acc[:, hh * LANES:(hh + 1) * LANES], ck_ref[...], sk_ref[...], first_half)
                o_ref[0, hh * LANES:(hh + 1) * LANES, :] = blk.T.astype(BF16)
        elif gi == 4:
            q = (acc * (NA_DIM ** -0.5 * LOG2_E)).astype(BF16)
            zero = jnp.zeros((ROW_TILE, LANES), BF16)
            for pair in range(NA_HEADS // 2):
                qp = q[:, pair * LANES:(pair + 1) * LANES]
                first = jnp.where(lane < NA_DIM, qp, zero)
                second = jnp.where(lane < NA_DIM, zero, qp)
                for rr in range(ROW_TILE // GRID_W):
                    tokens = slice(rr * GRID_W, (rr + 1) * GRID_W)
                    o_ref[0, pair, rr * PAIR_ROWS:rr * PAIR_ROWS + GRID_W, :] = first[tokens]
                    o_ref[0, pair, rr * PAIR_ROWS + GRID_W:(rr + 1) * PAIR_ROWS, :] = second[tokens]
        elif gi in _K_GROUPS:
            o_ref[0] = acc.T.astype(BF16)
        else:
            o_ref[0] = acc.astype(BF16)


_CTX_GROUPS = (1, 2, 5, 6)


def _in_ctx_kernel(x_ref, mod_ref, g_ref, w_ref, *out_refs):
    x = jnp.concatenate([x_ref[i] for i in range(CTX_BATCHES)], axis=0)
    h = _prenorm(x, g_ref[...], mod_ref[0, 0:1, :], mod_ref[0, 1:2, :])
    for gi, o_ref in zip(_CTX_GROUPS, out_refs):
        w = w_ref[:, gi * GROUP_W:(gi + 1) * GROUP_W].astype(BF16)
        acc = jnp.dot(h, w, preferred_element_type=F32)
        acc = (acc.T if gi in _K_GROUPS else acc).astype(BF16)
        for i in range(CTX_BATCHES):
            tokens = slice(i * CTX_LEN, (i + 1) * CTX_LEN)
            o_ref[i] = acc[:, tokens] if gi in _K_GROUPS else acc[tokens, :]


def _rope_tables():
    tok = np.arange(SEQ)
    n_freq = RET_DIM // 4
    inv = ROPE_BASE ** (-np.arange(n_freq, dtype=np.float64) / n_freq)
    ang_r = (tok // GRID_W)[:, None] * inv[None, :]
    ang_c = (tok % GRID_W)[:, None] * inv[None, :]
    cos = np.concatenate([np.cos(ang_r), np.cos(ang_r), np.cos(ang_c), np.cos(ang_c)], axis=-1)
    sin = np.concatenate([-np.sin(ang_r), np.sin(ang_r), -np.sin(ang_c), np.sin(ang_c)], axis=-1)
    qs = RET_DIM ** -0.5
    return [jnp.asarray(t, F32) for t in (cos * qs, sin * qs, cos, sin)]


def _in_proj_lat(x, mods, g_pre, w_in):
    batch = x.shape[0]
    tok = lambda b, i: (b, i, 0)
    tab = pl.BlockSpec((ROW_TILE, LANES), lambda b, i: (i, 0))
    out_shapes = [jax.ShapeDtypeStruct((batch, SEQ, GROUP_W), BF16)] * N_GROUPS
    out_specs = [pl.BlockSpec((1, ROW_TILE, GROUP_W), tok)] * N_GROUPS
    for gi in _K_GROUPS:
        out_shapes[gi] = jax.ShapeDtypeStruct((batch, GROUP_W, SEQ), BF16)
        out_specs[gi] = pl.BlockSpec((1, GROUP_W, ROW_TILE), lambda b, i: (b, 0, i))
    n_pairs = NA_HEADS // 2
    out_shapes[4] = jax.ShapeDtypeStruct((batch, n_pairs, GRID_ROWS * PAIR_ROWS, LANES), BF16)
    out_specs[4] = pl.BlockSpec((1, n_pairs, ROW_TILE // GRID_W * PAIR_ROWS, LANES), lambda b, i: (b, 0, i, 0))
    return pl.pallas_call(
        _in_lat_kernel,
        grid=(batch, SEQ // ROW_TILE),
        in_specs=[pl.BlockSpec((1, ROW_TILE, D_MODEL), tok),
                  pl.BlockSpec((1, N_MOD, D_MODEL), lambda b, i: (b, 0, 0)),
                  pl.BlockSpec((1, D_MODEL), lambda b, i: (0, 0)),
                  _resident((D_MODEL, N_GROUPS * GROUP_W)),
                  tab, tab, tab, tab],
        out_specs=out_specs,
        out_shape=out_shapes,
        compiler_params=_params(2),
        name="in_lat",
    )(x, mods, g_pre, w_in, *_rope_tables())


def _in_proj_ctx(ctx, mods_ctx, g_pre, w_in):
    batch = ctx.shape[0]
    tok = lambda b: (b, 0, 0)
    out_shapes = [jax.ShapeDtypeStruct((batch, GROUP_W, CTX_LEN) if gi in _K_GROUPS else (batch, CTX_LEN, GROUP_W),
                                       BF16) for gi in _CTX_GROUPS]
    out_specs = [pl.BlockSpec((CTX_BATCHES, GROUP_W, CTX_LEN) if gi in _K_GROUPS else (CTX_BATCHES, CTX_LEN, GROUP_W),
                              tok) for gi in _CTX_GROUPS]
    return pl.pallas_call(
        _in_ctx_kernel,
        grid=(batch // CTX_BATCHES,),
        in_specs=[pl.BlockSpec((CTX_BATCHES, CTX_LEN, D_MODEL), tok),
                  pl.BlockSpec((1, N_MOD, D_MODEL), lambda b: (0, 0, 0)),
                  pl.BlockSpec((1, D_MODEL), lambda b: (0, 0)),
                  _resident((D_MODEL, N_GROUPS * GROUP_W))],
        out_specs=out_specs,
        out_shape=out_shapes,
        compiler_params=_params(1),
        name="in_ctx",
    )(ctx, mods_ctx, g_pre, w_in)


def _ret_kernel(lg_ref, q_ref, kt_ref, v_ref, g_ref, kct_ref, vc_ref, gn_ref, wf_ref, o_ref, wb_ref,
                dec_ref, wq_ref, wk_ref, upd_ref, st_ref, sp_ref):
    wb_ref[...] = wf_ref[...].astype(BF16)
    head = pl.program_id(0)
    lgf = lg_ref[0, head]
    lgb = lg_ref[1, head]
    c = RET_CHUNK
    n_chunks = SEQ // c

    @pl.when(pl.program_id(1) == 0)
    def _tables():
        ii = lax.broadcasted_iota(jnp.int32, (c, c), 0)
        jj = lax.broadcasted_iota(jnp.int32, (c, c), 1)
        diff = (ii - jj).astype(F32)
        dec_ref[...] = jnp.exp(jnp.where(diff >= 0, lgf * diff, -lgb * diff))
        ri = lax.broadcasted_iota(jnp.int32, (c, LANES), 0).astype(F32)
        wq_ref[0] = jnp.exp(lgf * (ri + 1.0))
        wq_ref[1] = jnp.exp(lgb * (c - ri))
        ti = lax.broadcasted_iota(jnp.int32, (1, c), 1).astype(F32)
        wk_ref[0] = jnp.exp(lgf * (c - 1.0 - ti))
        wk_ref[1] = jnp.exp(lgb * ti)

    ones = jnp.ones((1, LANES), F32)
    chunk_f = jnp.exp(ones * (lgf * c))
    chunk_b = jnp.exp(ones * (lgb * c))

    def update(kt, v):
        ktf = kt.astype(F32)
        kw = jnp.concatenate([(ktf * wk_ref[0]).astype(BF16), (ktf * wk_ref[1]).astype(BF16)], axis=0)
        return jnp.dot(kw, v, preferred_element_type=F32)

    def chunk(n):
        return slice(n * c, (n + 1) * c)

    batches = range(RET_BATCHES)
    for b in batches:
        for n in range(n_chunks):
            upd_ref[b, n] = update(kt_ref[b, :, chunk(n)], v_ref[b, chunk(n), :])

    for b in batches:
        ctx_upd = update(kct_ref[b], vc_ref[b])
        state = ctx_upd[:RET_DIM]
        for n in range(n_chunks):
            st_ref[b, n, 0:RET_DIM, :] = state.astype(BF16)
            state = chunk_f * state + upd_ref[b, n, 0:RET_DIM, :]
        state = ctx_upd[RET_DIM:]
        for n in reversed(range(n_chunks)):
            st_ref[b, n, RET_DIM:, :] = state.astype(BF16)
            state = chunk_b * state + upd_ref[b, n, RET_DIM:, :]

    def scores(b, n):
        s = jnp.dot(q_ref[b, chunk(n), :], kt_ref[b, :, chunk(n)], preferred_element_type=F32)
        sp_ref[b, n % 2] = (s * dec_ref[...]).astype(BF16)

    def outputs(b, n):
        rows = chunk(n)
        qf = q_ref[b, rows, :].astype(F32)
        qw = jnp.concatenate([(qf * wq_ref[0]).astype(BF16), (qf * wq_ref[1]).astype(BF16)], axis=1)
        o = (jnp.dot(sp_ref[b, n % 2], v_ref[b, rows, :], preferred_element_type=F32)
             + jnp.dot(qw, st_ref[b, n], preferred_element_type=F32))
        d = o - jnp.mean(o, axis=-1, keepdims=True)
        y = d * lax.rsqrt(jnp.mean(d * d, axis=-1, keepdims=True) + NORM_EPS) * gn_ref[...]
        o_ref[b, rows, :] = (y * _silu(g_ref[b, rows, :].astype(F32))).astype(BF16)

    for b in batches:
        scores(b, 0)
    for n in range(n_chunks):
        for b in batches:
            if n + 1 < n_chunks:
                scores(b, n + 1)
        for b in batches:
            outputs(b, n)


def _retention(log_gammas, rq, rkt, rv, rg, crkt, crv, gn_w, w_f32):
    assert CTX_LEN == RET_CHUNK
    batch = rq.shape[0]
    n_chunks = SEQ // RET_CHUNK
    nb = RET_BATCHES
    inner = batch // nb
    lat = pl.BlockSpec((nb, SEQ, LANES), lambda h, b: (b, 0, h))
    cx = pl.BlockSpec((nb, CTX_LEN, LANES), lambda h, b: (b, 0, h))
    w_rows, w_cols = w_f32.shape
    w_chunk = pl.BlockSpec((w_rows, w_cols // (RET_HEADS * inner)), lambda h, b: (0, h * inner + b))
    return pl.pallas_call(
        _ret_kernel,
        grid=(RET_HEADS, inner),
        in_specs=[pl.BlockSpec(memory_space=pltpu.SMEM), lat,
                  pl.BlockSpec((nb, LANES, SEQ), lambda h, b: (b, h, 0)), lat, lat,
                  pl.BlockSpec((nb, LANES, CTX_LEN), lambda h, b: (b, h, 0)), cx,
                  pl.BlockSpec((1, LANES), lambda h, b: (0, h)), w_chunk],
        out_specs=[lat, w_chunk],
        out_shape=[jax.ShapeDtypeStruct((batch, SEQ, RET_WIDTH), BF16), jax.ShapeDtypeStruct(w_f32.shape, BF16)],
        scratch_shapes=[pltpu.VMEM((RET_CHUNK, RET_CHUNK), F32),
                        pltpu.VMEM((2, RET_CHUNK, LANES), F32),
                        pltpu.VMEM((2, 1, RET_CHUNK), F32),
                        pltpu.VMEM((nb, n_chunks, 2 * RET_DIM, RET_DIM), F32),
                        pltpu.VMEM((nb, n_chunks, 2 * RET_DIM, RET_DIM), BF16),
                        pltpu.VMEM((nb, 2, RET_CHUNK, RET_CHUNK), BF16)],
        compiler_params=_params(2),
        name="ret",
    )(log_gammas, rq, rkt, rv, rg, crkt, crv, gn_w, w_f32)


N_DR = 2 * NA_KH - 1
N_DC = 2 * NA_KW - 1
PAIR_ROWS = 2 * GRID_W
NA_GROUP = 2
NA_BATCHES = 2


def _na_build_bias(rpb_ref, bias_ref, pair):
    qi = lax.broadcasted_iota(jnp.int32, (GRID_W, LANES), 0)
    li = lax.broadcasted_iota(jnp.int32, (GRID_W, LANES), 1)
    key_col = li & (GRID_W - 1)
    col_start = jnp.clip(qi - NA_KW // 2, 0, GRID_W - NA_KW)
    valid = (key_col >= col_start) & (key_col < col_start + NA_KW)
    code = jnp.where(valid, key_col - qi + (NA_KW - 1), -1)
    first_row = lax.broadcasted_iota(jnp.int32, (1, LANES), 1) < GRID_W

    def pattern(t, carry):
        for hd in range(2):
            head = 2 * pair + hd
            for blk in range(NA_KH // 2):
                base = (head * N_DR + (2 * blk + NA_KH - 1 - t)) * N_DC
                acc = jnp.full((GRID_W, LANES), NEG_INF, F32)
                for b in range(N_DC):
                    val = jnp.where(first_row, rpb_ref[base + b], rpb_ref[base + N_DC + b]) * LOG2_E
                    acc = jnp.where(code == b, val, acc)
                bias_ref[t, hd * GRID_W:(hd + 1) * GRID_W, blk * LANES:(blk + 1) * LANES] = acc
        return carry

    lax.fori_loop(0, N_PATTERNS, pattern, 0)


def _na_kernel(rpb_ref, q2_ref, kt_ref, v_ref, kct_ref, vc_ref, wf_ref, o_ref, wb_ref,
               bias_ref, kt2_ref, s_ref):
    wb_ref[...] = wf_ref[...].astype(BF16)
    pair = pl.program_id(0)
    low = lax.broadcasted_iota(jnp.int32, (1, LANES), 1) < NA_DIM
    half = NA_KH // 2

    @pl.when(pl.program_id(1) == 0)
    def _bias():
        _na_build_bias(rpb_ref, bias_ref, pair)

    batches = range(NA_BATCHES)
    for bb in batches:
        kt2_ref[bb, 0] = kt_ref[bb]
        kt2_ref[bb, 1, :, 0:SEQ - LANES] = kt_ref[bb, :, GRID_W:SEQ - GRID_W]
        kt2_ref[bb, 1, :, SEQ - LANES:SEQ] = jnp.zeros((LANES, LANES), BF16)

    def band_start(r):
        return jnp.clip(r - half, 0, GRID_ROWS - NA_KH)

    def scores(bb, group, slot):
        rows = group * (NA_GROUP * PAIR_ROWS)
        group_q = pl.ds(pl.multiple_of(rows, NA_GROUP * PAIR_ROWS), NA_GROUP * PAIR_ROWS)
        s_ref[bb, slot, :, NA_BAND:] = jnp.dot(q2_ref[bb, 0, group_q, :], kct_ref[bb], preferred_element_type=F32)
        for j in range(NA_GROUP):
            r = group * NA_GROUP + j
            start = band_start(r)
            pattern = jnp.where(r < half, r, jnp.where(r > GRID_ROWS - half, r - (GRID_ROWS - NA_KH), half))
            lanes = pl.ds(pl.multiple_of((start >> 1) * LANES, LANES), NA_BAND)
            kb = kt2_ref[bb, start & 1, :, lanes]
            q2 = q2_ref[bb, 0, pl.ds(pl.multiple_of(r * PAIR_ROWS, PAIR_ROWS), PAIR_ROWS), :]
            s_ref[bb, slot, j * PAIR_ROWS:(j + 1) * PAIR_ROWS, 0:NA_BAND] = (
                jnp.dot(q2, kb, preferred_element_type=F32) + bias_ref[pattern])

    n_blocks = (NA_BAND + CTX_LEN) // LANES

    def outputs(bb, group, slot):
        probs = []
        for j in range(NA_GROUP):
            rows = slice(j * PAIR_ROWS, (j + 1) * PAIR_ROWS)

            def block(b):
                return s_ref[bb, slot, rows, b * LANES:(b + 1) * LANES]

            m = block(0)
            for b in range(1, n_blocks):
                m = jnp.maximum(m, block(b))
            m = jnp.max(m, axis=-1, keepdims=True)
            p = [jnp.exp2(block(b) - m) for b in range(n_blocks)]
            den = p[0]
            for pb in p[1:]:
                den = den + pb
            den = jnp.sum(den, axis=-1, keepdims=True)
            probs.append((jnp.concatenate([pb.astype(BF16) for pb in p], axis=1), den))
        ctx = jnp.dot(jnp.concatenate([p[:, NA_BAND:] for p, _ in probs], axis=0), vc_ref[bb],
                      preferred_element_type=F32)
        for j, (p, den) in enumerate(probs):
            r = group * NA_GROUP + j
            vb = v_ref[bb, pl.ds(pl.multiple_of(band_start(r) * GRID_W, GRID_W), NA_BAND), :]
            o = (jnp.dot(p[:, 0:NA_BAND], vb, preferred_element_type=F32)
                 + ctx[j * PAIR_ROWS:(j + 1) * PAIR_ROWS]) * (1.0 / den)
            o_ref[bb, pl.ds(pl.multiple_of(r * GRID_W, GRID_W), GRID_W), :] = (
                jnp.where(low, o[:GRID_W], o[GRID_W:]).astype(BF16))

    n_groups = GRID_ROWS // NA_GROUP
    for bb in batches:
        scores(bb, 0, 0)
    for g in range(n_groups):
        for bb in batches:
            if g + 1 < n_groups:
                scores(bb, g + 1, (g + 1) % 2)
        for bb in batches:
            outputs(bb, g, g % 2)


def _neighbourhood_attention(rpb, nq, nkt, nv, cnkt, cnv, w_f32):
    batch = nq.shape[0]
    n_q2 = GRID_ROWS * PAIR_ROWS
    nb = NA_BATCHES
    n_pairs, inner = NA_HEADS // 2, batch // nb
    lat = pl.BlockSpec((nb, SEQ, LANES), lambda p, b: (b, 0, p))
    cx = pl.BlockSpec((nb, CTX_LEN, LANES), lambda p, b: (b, 0, p))
    w_rows, w_cols = w_f32.shape
    w_chunk = pl.BlockSpec((w_rows // (n_pairs * inner), w_cols), lambda p, b: (p * inner + b, 0))
    return pl.pallas_call(
        _na_kernel,
        grid=(n_pairs, inner),
        in_specs=[pl.BlockSpec(memory_space=pltpu.SMEM),
                  pl.BlockSpec((nb, 1, n_q2, LANES), lambda p, b: (b, p, 0, 0)),
                  pl.BlockSpec((nb, LANES, SEQ), lambda p, b: (b, p, 0)), lat,
                  pl.BlockSpec((nb, LANES, CTX_LEN), lambda p, b: (b, p, 0)), cx, w_chunk],
        out_specs=[lat, w_chunk],
        out_shape=[jax.ShapeDtypeStruct((batch, SEQ, NA_WIDTH), BF16), jax.ShapeDtypeStruct(w_f32.shape, BF16)],
        scratch_shapes=[pltpu.VMEM((N_PATTERNS, PAIR_ROWS, NA_BAND), F32),
                        pltpu.VMEM((nb, 2, LANES, SEQ), BF16),
                        pltpu.VMEM((nb, 2, NA_GROUP * PAIR_ROWS, NA_BAND + CTX_LEN), F32)],
        compiler_params=_params(2),
        name="na",
    )(rpb, nq, nkt, nv, cnkt, cnv, w_f32)


def _out_kernel(x_ref, ret_ref, na_ref, mod_ref, gpm_ref, gpre_ref, gpost_ref, wo_ref, w1_ref, w2_ref, o_ref):
    gt1, sh2, sc2, gt2 = (mod_ref[0, i:i + 1, :] for i in (2, 3, 4, 5))
    subs = [slice(i * SUB_TILE, (i + 1) * SUB_TILE) for i in range(OUT_TILE // SUB_TILE)]
    mix = [jnp.dot(ret_ref[0, rows, :], wo_ref[0:RET_WIDTH, :], preferred_element_type=F32)
           + jnp.dot(na_ref[0, rows, :], wo_ref[RET_WIDTH:, :], preferred_element_type=F32) for rows in subs]
    h2 = []
    for rows, m in zip(subs, mix):
        x1 = x_ref[0, rows, :] + gt1 * _rms(m, gpm_ref[...])
        o_ref[0, rows, :] = x1
        h2.append(_prenorm(x1, gpre_ref[...], sh2, sc2))
    for rows, h in zip(subs, h2):
        acc = jnp.zeros((SUB_TILE, D_MODEL), F32)
        for ci in range(D_FF // FF_CHUNK):
            cols = slice(ci * FF_CHUNK, (ci + 1) * FF_CHUNK)
            a = jnp.maximum(jnp.dot(h, w1_ref[:, cols], preferred_element_type=F32), 0.0)
            acc = acc + jnp.dot((a * a).astype(BF16), w2_ref[cols, :], preferred_element_type=F32)
        o_ref[0, rows, :] = o_ref[0, rows, :] + gt2 * _rms(acc, gpost_ref[...])


def _out_proj_mlp(x, ret_lat, na_lat, mods, g_post_mix, g_pre_mlp, g_post_mlp, w_out, w_mlp1, w_mlp2):
    batch = x.shape[0]
    tok = lambda b, i: (b, i, 0)
    gain = pl.BlockSpec((1, D_MODEL), lambda b, i: (0, 0))
    return pl.pallas_call(
        _out_kernel,
        grid=(batch, SEQ // OUT_TILE),
        in_specs=[pl.BlockSpec((1, OUT_TILE, D_MODEL), tok),
                  pl.BlockSpec((1, OUT_TILE, RET_WIDTH), tok),
                  pl.BlockSpec((1, OUT_TILE, NA_WIDTH), tok),
                  pl.BlockSpec((1, N_MOD, D_MODEL), lambda b, i: (b, 0, 0)),
                  gain, gain, gain,
                  _resident((RET_WIDTH + NA_WIDTH, D_MODEL)),
                  _resident((D_MODEL, D_FF)),
                  _resident((D_FF, D_MODEL))],
        out_specs=pl.BlockSpec((1, OUT_TILE, D_MODEL), tok),
        out_shape=jax.ShapeDtypeStruct(x.shape, F32),
        compiler_params=_params(2),
        name="out_mlp",
    )(x, ret_lat, na_lat, mods, g_post_mix, g_pre_mlp, g_post_mlp, w_out, w_mlp1, w_mlp2)


def kernel(x, c, ctx, c_ctx, w_ada, b_ada, g_pre_mix, g_post_mix, g_pre_mlp, g_post_mlp,
           w_in, ret_decay, ret_gn, na_rpb, w_out, w_mlp1, w_mlp2):
    assert w_in.shape[0] == 1, "single-layer block: the context stream is never updated"
    batch = x.shape[0]
    pad = (-(batch + 1)) % 8
    cc = jnp.concatenate([c, c_ctx[None, :], jnp.zeros((pad, D_MODEL), F32)], axis=0)
    mods = _modulations(cc, w_ada[0], b_ada)
    mods_lat = mods[:batch].reshape(batch, N_MOD, D_MODEL)
    mods_ctx = mods[batch:batch + 1].reshape(1, N_MOD, D_MODEL)

    rq, rk, rv, rg, nq, nkt, nv = _in_proj_lat(x, mods_lat, g_pre_mix, w_in[0])
    crk, crv, cnkt, cnv = _in_proj_ctx(ctx, mods_ctx, g_pre_mix, w_in[0])

    log_gammas = jax.nn.log_sigmoid(ret_decay[0].astype(F32))
    ret_lat, w1_b = _retention(log_gammas, rq, rk, rv, rg, crk, crv, ret_gn, w_mlp1[0])
    na_lat, w2_b = _neighbourhood_attention(na_rpb[0].astype(F32).reshape(-1), nq, nkt, nv, cnkt, cnv, w_mlp2[0])

    return _out_proj_mlp(x, ret_lat, na_lat, mods_lat, g_post_mix, g_pre_mlp, g_post_mlp,
                         w_out[0].astype(BF16), w1_b, w2_b)
```

```python
import numpy as np
import jax
import jax.numpy as jnp
from jax import lax
from jax.experimental import pallas as pl
from jax.experimental.pallas import tpu as pltpu

D_MODEL = 1024
SEQ = 2048
CTX_LEN = 256
GRID_W = 64
GRID_ROWS = SEQ // GRID_W
RET_HEADS = 4
RET_DIM = 128
RET_WIDTH = RET_HEADS * RET_DIM
NA_HEADS = 8
NA_DIM = 64
NA_WIDTH = NA_HEADS * NA_DIM
NA_KH = 8
NA_KW = 16
N_GROUPS = 7
GROUP_W = 512
_K_GROUPS = (1, 5)
D_FF = 4 * D_MODEL
ROPE_BASE = 10000.0
NORM_EPS = 1e-6
N_MOD = 6
NEG_INF = -1e30
LOG2_E = 1.4426950408889634

LANES = 128
MOD_STEPS = 3
ROW_TILE = 1024
CTX_BATCHES = 2
OUT_TILE = 1024
SUB_TILE = 256
FF_CHUNK = 1024
RET_CHUNK = 256
RET_BATCHES = 2
NA_BAND = NA_KH * GRID_W
N_PATTERNS = 8
VMEM_LIMIT = 56 * 1024 * 1024

F32 = jnp.float32
BF16 = jnp.bfloat16


def _silu(x):
    return x * (1.0 / (1.0 + jnp.exp(-x)))


def _rms(x, g):
    return x * lax.rsqrt(jnp.mean(x * x, axis=-1, keepdims=True) + NORM_EPS) * g


def _params(n_axes):
    return pltpu.CompilerParams(dimension_semantics=("arbitrary",) * n_axes,
                                vmem_limit_bytes=VMEM_LIMIT)


def _resident(shape):
    nd = len(shape)
    return pl.BlockSpec(shape, lambda *_: (0,) * nd, pipeline_mode=pl.Buffered(1))


def _mod_kernel(c_ref, w_ref, b_ref, o_ref):
    a = _silu(c_ref[...]).astype(BF16)
    o_ref[...] = jnp.dot(a, w_ref[...].astype(BF16), preferred_element_type=F32) + b_ref[...]


def _modulations(cc, w_ada, b_ada):
    rows = cc.shape[0]
    cols = N_MOD * D_MODEL // MOD_STEPS
    return pl.pallas_call(
        _mod_kernel,
        grid=(MOD_STEPS,),
        in_specs=[pl.BlockSpec((rows, D_MODEL), lambda j: (0, 0)),
                  pl.BlockSpec((D_MODEL, cols), lambda j: (0, j)),
                  pl.BlockSpec((1, cols), lambda j: (0, j))],
        out_specs=pl.BlockSpec((rows, cols), lambda j: (0, j)),
        out_shape=jax.ShapeDtypeStruct((rows, N_MOD * D_MODEL), F32),
        compiler_params=_params(1),
        name="mod",
    )(cc, w_ada, b_ada)


def _prenorm(x, g, shift, scale):
    return (_rms(x, g) * (1.0 + scale) + shift).astype(BF16)


def _rope(blk, cos, sin_signed, first_half):
    partner = jnp.where(first_half, pltpu.roll(blk, LANES - 32, 1), pltpu.roll(blk, 32, 1))
    return blk * cos + partner * sin_signed


def _in_lat_kernel(x_ref, mod_ref, g_ref, w_ref, cq_ref, sq_ref, ck_ref, sk_ref, *out_refs):
    h = _prenorm(x_ref[0], g_ref[...], mod_ref[0, 0:1, :], mod_ref[0, 1:2, :])
    lane = lax.broadcasted_iota(jnp.int32, (1, LANES), 1)
    first_half = (lane & 32) == 0
    for gi, o_ref in enumerate(out_refs):
        w = w_ref[:, gi * GROUP_W:(gi + 1) * GROUP_W].astype(BF16)
        acc = jnp.dot(h, w, preferred_element_type=F32)
        if gi == 0:
            for hh in range(RET_HEADS):
                blk = _rope(acc[:, hh * LANES:(hh + 1) * LANES], cq_ref[...], sq_ref[...], first_half)
                o_ref[0, :, hh * LANES:(hh + 1) * LANES] = blk.astype(BF16)
        elif gi == 1:
            for hh in range(RET_HEADS):
                blk = _rope(acc[:, hh * LANES:(hh + 1) * LANES], ck_ref[...], sk_ref[...], first_half)
                o_ref[0, hh * LANES:(hh + 1) * LANES, :] = blk.T.astype(BF16)
        elif gi == 4:
            q = (acc * (NA_DIM ** -0.5 * LOG2_E)).astype(BF16)
            zero = jnp.zeros((ROW_TILE, LANES), BF16)
            for pair in range(NA_HEADS // 2):
                qp = q[:, pair * LANES:(pair + 1) * LANES]
                first = jnp.where(lane < NA_DIM, qp, zero)
                second = jnp.where(lane < NA_DIM, zero, qp)
                for rr in range(ROW_TILE // GRID_W):
                    tokens = slice(rr * GRID_W, (rr + 1) * GRID_W)
                    o_ref[0, pair, rr * PAIR_ROWS:rr * PAIR_ROWS + GRID_W, :] = first[tokens]
                    o_ref[0, pair, rr * PAIR_ROWS + GRID_W:(rr + 1) * PAIR_ROWS, :] = second[tokens]
        elif gi in _K_GROUPS:
            o_ref[0] = acc.T.astype(BF16)
        else:
            o_ref[0] = acc.astype(BF16)


_CTX_GROUPS = (1, 2, 5, 6)


def _in_ctx_kernel(x_ref, mod_ref, g_ref, w_ref, wf_ref, *out_refs):
    *out_refs, wb_ref = out_refs
    wb_ref[...] = wf_ref[...].astype(BF16)
    x = jnp.concatenate([x_ref[i] for i in range(CTX_BATCHES)], axis=0)
    h = _prenorm(x, g_ref[...], mod_ref[0, 0:1, :], mod_ref[0, 1:2, :])
    for gi, o_ref in zip(_CTX_GROUPS, out_refs):
        w = w_ref[:, gi * GROUP_W:(gi + 1) * GROUP_W].astype(BF16)
        acc = jnp.dot(h, w, preferred_element_type=F32)
        acc = (acc.T if gi in _K_GROUPS else acc).astype(BF16)
        for i in range(CTX_BATCHES):
            tokens = slice(i * CTX_LEN, (i + 1) * CTX_LEN)
            o_ref[i] = acc[:, tokens] if gi in _K_GROUPS else acc[tokens, :]


def _rope_tables():
    tok = np.arange(SEQ)
    n_freq = RET_DIM // 4
    inv = ROPE_BASE ** (-np.arange(n_freq, dtype=np.float64) / n_freq)
    ang_r = (tok // GRID_W)[:, None] * inv[None, :]
    ang_c = (tok % GRID_W)[:, None] * inv[None, :]
    cos = np.concatenate([np.cos(ang_r), np.cos(ang_r), np.cos(ang_c), np.cos(ang_c)], axis=-1)
    sin = np.concatenate([-np.sin(ang_r), np.sin(ang_r), -np.sin(ang_c), np.sin(ang_c)], axis=-1)
    qs = RET_DIM ** -0.5
    return [jnp.asarray(t, F32) for t in (cos * qs, sin * qs, cos, sin)]


def _in_proj_lat(x, mods, g_pre, w_in):
    batch = x.shape[0]
    tok = lambda b, i: (b, i, 0)
    tab = pl.BlockSpec((ROW_TILE, LANES), lambda b, i: (i, 0))
    out_shapes = [jax.ShapeDtypeStruct((batch, SEQ, GROUP_W), BF16)] * N_GROUPS
    out_specs = [pl.BlockSpec((1, ROW_TILE, GROUP_W), tok)] * N_GROUPS
    for gi in _K_GROUPS:
        out_shapes[gi] = jax.ShapeDtypeStruct((batch, GROUP_W, SEQ), BF16)
        out_specs[gi] = pl.BlockSpec((1, GROUP_W, ROW_TILE), lambda b, i: (b, 0, i))
    n_pairs = NA_HEADS // 2
    out_shapes[4] = jax.ShapeDtypeStruct((batch, n_pairs, GRID_ROWS * PAIR_ROWS, LANES), BF16)
    out_specs[4] = pl.BlockSpec((1, n_pairs, ROW_TILE // GRID_W * PAIR_ROWS, LANES), lambda b, i: (b, 0, i, 0))
    return pl.pallas_call(
        _in_lat_kernel,
        grid=(batch, SEQ // ROW_TILE),
        in_specs=[pl.BlockSpec((1, ROW_TILE, D_MODEL), tok),
                  pl.BlockSpec((1, N_MOD, D_MODEL), lambda b, i: (b, 0, 0)),
                  pl.BlockSpec((1, D_MODEL), lambda b, i: (0, 0)),
                  _resident((D_MODEL, N_GROUPS * GROUP_W)),
                  tab, tab, tab, tab],
        out_specs=out_specs,
        out_shape=out_shapes,
        compiler_params=_params(2),
        name="in_lat",
    )(x, mods, g_pre, w_in, *_rope_tables())


def _in_proj_ctx(ctx, mods_ctx, g_pre, w_in, w_f32):
    batch = ctx.shape[0]
    tok = lambda b: (b, 0, 0)
    out_shapes = [jax.ShapeDtypeStruct((batch, GROUP_W, CTX_LEN) if gi in _K_GROUPS else (batch, CTX_LEN, GROUP_W),
                                       BF16) for gi in _CTX_GROUPS]
    out_specs = [pl.BlockSpec((CTX_BATCHES, GROUP_W, CTX_LEN) if gi in _K_GROUPS else (CTX_BATCHES, CTX_LEN, GROUP_W),
                              tok) for gi in _CTX_GROUPS]
    n_steps = batch // CTX_BATCHES
    w_chunk = pl.BlockSpec((w_f32.shape[0] // n_steps, w_f32.shape[1]), lambda b: (b, 0))
    return pl.pallas_call(
        _in_ctx_kernel,
        grid=(n_steps,),
        in_specs=[pl.BlockSpec((CTX_BATCHES, CTX_LEN, D_MODEL), tok),
                  pl.BlockSpec((1, N_MOD, D_MODEL), lambda b: (0, 0, 0)),
                  pl.BlockSpec((1, D_MODEL), lambda b: (0, 0)),
                  _resident((D_MODEL, N_GROUPS * GROUP_W)), w_chunk],
        out_specs=out_specs + [w_chunk],
        out_shape=out_shapes + [jax.ShapeDtypeStruct(w_f32.shape, BF16)],
        compiler_params=_params(1),
        name="in_ctx",
    )(ctx, mods_ctx, g_pre, w_in, w_f32)


def _ret_kernel(lg_ref, q_ref, kt_ref, v_ref, g_ref, kct_ref, vc_ref, gn_ref, wf_ref, o_ref, wb_ref,
                dec_ref, wq_ref, wk_ref, upd_ref, st_ref, sp_ref):
    wb_ref[...] = wf_ref[...].astype(BF16)
    head = pl.program_id(0)
    lgf = lg_ref[0, head]
    lgb = lg_ref[1, head]
    c = RET_CHUNK
    n_chunks = SEQ // c

    @pl.when(pl.program_id(1) == 0)
    def _tables():
        ii = lax.broadcasted_iota(jnp.int32, (c, c), 0)
        jj = lax.broadcasted_iota(jnp.int32, (c, c), 1)
        diff = (ii - jj).astype(F32)
        dec_ref[...] = jnp.exp(jnp.where(diff >= 0, lgf * diff, -lgb * diff))
        ri = lax.broadcasted_iota(jnp.int32, (c, LANES), 0).astype(F32)
        wq_ref[0] = jnp.exp(lgf * (ri + 1.0))
        wq_ref[1] = jnp.exp(lgb * (c - ri))
        ti = lax.broadcasted_iota(jnp.int32, (1, c), 1).astype(F32)
        wk_ref[0] = jnp.exp(lgf * (c - 1.0 - ti))
        wk_ref[1] = jnp.exp(lgb * ti)

    ones = jnp.ones((1, LANES), F32)
    chunk_f = jnp.exp(ones * (lgf * c))
    chunk_b = jnp.exp(ones * (lgb * c))

    def update(kt, v):
        ktf = kt.astype(F32)
        kw = jnp.concatenate([(ktf * wk_ref[0]).astype(BF16), (ktf * wk_ref[1]).astype(BF16)], axis=0)
        return jnp.dot(kw, v, preferred_element_type=F32)

    def chunk(n):
        return slice(n * c, (n + 1) * c)

    batches = range(RET_BATCHES)
    for b in batches:
        for n in range(n_chunks):
            upd_ref[b, n] = update(kt_ref[b, :, chunk(n)], v_ref[b, chunk(n), :])

    for b in batches:
        ctx_upd = update(kct_ref[b], vc_ref[b])
        state = ctx_upd[:RET_DIM]
        for n in range(n_chunks):
            st_ref[b, n, 0:RET_DIM, :] = state.astype(BF16)
            state = chunk_f * state + upd_ref[b, n, 0:RET_DIM, :]
        state = ctx_upd[RET_DIM:]
        for n in reversed(range(n_chunks)):
            st_ref[b, n, RET_DIM:, :] = state.astype(BF16)
            state = chunk_b * state + upd_ref[b, n, RET_DIM:, :]

    def scores(b, n):
        s = jnp.dot(q_ref[b, chunk(n), :], kt_ref[b, :, chunk(n)], preferred_element_type=F32)
        sp_ref[b, n % 2] = (s * dec_ref[...]).astype(BF16)

    def outputs(b, n):
        rows = chunk(n)
        qf = q_ref[b, rows, :].astype(F32)
        qw = jnp.concatenate([(qf * wq_ref[0]).astype(BF16), (qf * wq_ref[1]).astype(BF16)], axis=1)
        o = (jnp.dot(sp_ref[b, n % 2], v_ref[b, rows, :], preferred_element_type=F32)
             + jnp.dot(qw, st_ref[b, n], preferred_element_type=F32))
        d = o - jnp.mean(o, axis=-1, keepdims=True)
        y = d * lax.rsqrt(jnp.mean(d * d, axis=-1, keepdims=True) + NORM_EPS) * gn_ref[...]
        o_ref[b, rows, :] = (y * _silu(g_ref[b, rows, :].astype(F32))).astype(BF16)

    for b in batches:
        scores(b, 0)
    for n in range(n_chunks):
        for b in batches:
            if n + 1 < n_chunks:
                scores(b, n + 1)
        for b in batches:
            outputs(b, n)


def _retention(log_gammas, rq, rkt, rv, rg, crkt, crv, gn_w, w_f32):
    assert CTX_LEN == RET_CHUNK
    batch = rq.shape[0]
    n_chunks = SEQ // RET_CHUNK
    nb = RET_BATCHES
    inner = batch // nb
    lat = pl.BlockSpec((nb, SEQ, LANES), lambda h, b: (b, 0, h))
    cx = pl.BlockSpec((nb, CTX_LEN, LANES), lambda h, b: (b, 0, h))
    w_rows, w_cols = w_f32.shape
    w_chunk = pl.BlockSpec((w_rows, w_cols // (RET_HEADS * inner)), lambda h, b: (0, h * inner + b))
    return pl.pallas_call(
        _ret_kernel,
        grid=(RET_HEADS, inner),
        in_specs=[pl.BlockSpec(memory_space=pltpu.SMEM), lat,
                  pl.BlockSpec((nb, LANES, SEQ), lambda h, b: (b, h, 0)), lat, lat,
                  pl.BlockSpec((nb, LANES, CTX_LEN), lambda h, b: (b, h, 0)), cx,
                  pl.BlockSpec((1, LANES), lambda h, b: (0, h)), w_chunk],
        out_specs=[lat, w_chunk],
        out_shape=[jax.ShapeDtypeStruct((batch, SEQ, RET_WIDTH), BF16), jax.ShapeDtypeStruct(w_f32.shape, BF16)],
        scratch_shapes=[pltpu.VMEM((RET_CHUNK, RET_CHUNK), F32),
                        pltpu.VMEM((2, RET_CHUNK, LANES), F32),
                        pltpu.VMEM((2, 1, RET_CHUNK), F32),
                        pltpu.VMEM((nb, n_chunks, 2 * RET_DIM, RET_DIM), F32),
                        pltpu.VMEM((nb, n_chunks, 2 * RET_DIM, RET_DIM), BF16),
                        pltpu.VMEM((nb, 2, RET_CHUNK, RET_CHUNK), BF16)],
        compiler_params=_params(2),
        name="ret",
    )(log_gammas, rq, rkt, rv, rg, crkt, crv, gn_w, w_f32)


N_DR = 2 * NA_KH - 1
N_DC = 2 * NA_KW - 1
PAIR_ROWS = 2 * GRID_W
NA_GROUP = 2
NA_BATCHES = 2


def _na_build_bias(rpb_ref, bias_ref, pair):
    qi = lax.broadcasted_iota(jnp.int32, (GRID_W, LANES), 0)
    li = lax.broadcasted_iota(jnp.int32, (GRID_W, LANES), 1)
    key_col = li & (GRID_W - 1)
    col_start = jnp.clip(qi - NA_KW // 2, 0, GRID_W - NA_KW)
    valid = (key_col >= col_start) & (key_col < col_start + NA_KW)
    code = jnp.where(valid, key_col - qi + (NA_KW - 1), -1)
    first_row = lax.broadcasted_iota(jnp.int32, (1, LANES), 1) < GRID_W

    def pattern(t, carry):
        for hd in range(2):
            head = 2 * pair + hd
            for blk in range(NA_KH // 2):
                base = (head * N_DR + (2 * blk + NA_KH - 1 - t)) * N_DC
                acc = jnp.full((GRID_W, LANES), NEG_INF, F32)
                for b in range(N_DC):
                    val = jnp.where(first_row, rpb_ref[base + b], rpb_ref[base + N_DC + b]) * LOG2_E
                    acc = jnp.where(code == b, val, acc)
                bias_ref[t, hd * GRID_W:(hd + 1) * GRID_W, blk * LANES:(blk + 1) * LANES] = acc
        return carry

    lax.fori_loop(0, N_PATTERNS, pattern, 0)


def _na_kernel(rpb_ref, q2_ref, kt_ref, v_ref, kct_ref, vc_ref, wf_ref, o_ref, wb_ref,
               bias_ref, kt2_ref, s_ref):
    wb_ref[...] = wf_ref[...].astype(BF16)
    pair = pl.program_id(0)
    low = lax.broadcasted_iota(jnp.int32, (1, LANES), 1) < NA_DIM
    half = NA_KH // 2

    @pl.when(pl.program_id(1) == 0)
    def _bias():
        _na_build_bias(rpb_ref, bias_ref, pair)

    batches = range(NA_BATCHES)
    for bb in batches:
        kt2_ref[bb, 0] = kt_ref[bb]
        kt2_ref[bb, 1, :, 0:SEQ - LANES] = kt_ref[bb, :, GRID_W:SEQ - GRID_W]
        kt2_ref[bb, 1, :, SEQ - LANES:SEQ] = jnp.zeros((LANES, LANES), BF16)

    def band_start(r):
        return jnp.clip(r - half, 0, GRID_ROWS - NA_KH)

    def scores(bb, group, slot):
        rows = group * (NA_GROUP * PAIR_ROWS)
        group_q = pl.ds(pl.multiple_of(rows, NA_GROUP * PAIR_ROWS), NA_GROUP * PAIR_ROWS)
        s_ref[bb, slot, :, NA_BAND:] = jnp.dot(q2_ref[bb, 0, group_q, :], kct_ref[bb], preferred_element_type=F32)
        for j in range(NA_GROUP):
            r = group * NA_GROUP + j
            start = band_start(r)
            pattern = jnp.where(r < half, r, jnp.where(r > GRID_ROWS - half, r - (GRID_ROWS - NA_KH), half))
            lanes = pl.ds(pl.multiple_of((start >> 1) * LANES, LANES), NA_BAND)
            kb = kt2_ref[bb, start & 1, :, lanes]
            q2 = q2_ref[bb, 0, pl.ds(pl.multiple_of(r * PAIR_ROWS, PAIR_ROWS), PAIR_ROWS), :]
            s_ref[bb, slot, j * PAIR_ROWS:(j + 1) * PAIR_ROWS, 0:NA_BAND] = (
                jnp.dot(q2, kb, preferred_element_type=F32) + bias_ref[pattern])

    n_blocks = (NA_BAND + CTX_LEN) // LANES

    def outputs(bb, group, slot):
        probs = []
        for j in range(NA_GROUP):
            rows = slice(j * PAIR_ROWS, (j + 1) * PAIR_ROWS)

            def block(b):
                return s_ref[bb, slot, rows, b * LANES:(b + 1) * LANES]

            m = block(0)
            for b in range(1, n_blocks):
                m = jnp.maximum(m, block(b))
            m = jnp.max(m, axis=-1, keepdims=True)
            p = [jnp.exp2(block(b) - m) for b in range(n_blocks)]
            den = p[0]
            for pb in p[1:]:
                den = den + pb
            den = jnp.sum(den, axis=-1, keepdims=True)
            probs.append((jnp.concatenate([pb.astype(BF16) for pb in p], axis=1), den))
        ctx = jnp.dot(jnp.concatenate([p[:, NA_BAND:] for p, _ in probs], axis=0), vc_ref[bb],
                      preferred_element_type=F32)
        for j, (p, den) in enumerate(probs):
            r = group * NA_GROUP + j
            vb = v_ref[bb, pl.ds(pl.multiple_of(band_start(r) * GRID_W, GRID_W), NA_BAND), :]
            o = (jnp.dot(p[:, 0:NA_BAND], vb, preferred_element_type=F32)
                 + ctx[j * PAIR_ROWS:(j + 1) * PAIR_ROWS]) * (1.0 / den)
            o_ref[bb, pl.ds(pl.multiple_of(r * GRID_W, GRID_W), GRID_W), :] = (
                jnp.where(low, o[:GRID_W], o[GRID_W:]).astype(BF16))

    n_groups = GRID_ROWS // NA_GROUP
    for bb in batches:
        scores(bb, 0, 0)
    for g in range(n_groups):
        for bb in batches:
            if g + 1 < n_groups:
                scores(bb, g + 1, (g + 1) % 2)
        for bb in batches:
            outputs(bb, g, g % 2)


def _neighbourhood_attention(rpb, nq, nkt, nv, cnkt, cnv, w_f32):
    batch = nq.shape[0]
    n_q2 = GRID_ROWS * PAIR_ROWS
    nb = NA_BATCHES
    n_pairs, inner = NA_HEADS // 2, batch // nb
    lat = pl.BlockSpec((nb, SEQ, LANES), lambda p, b: (b, 0, p))
    cx = pl.BlockSpec((nb, CTX_LEN, LANES), lambda p, b: (b, 0, p))
    w_rows, w_cols = w_f32.shape
    w_chunk = pl.BlockSpec((w_rows // (n_pairs * inner), w_cols), lambda p, b: (p * inner + b, 0))
    return pl.pallas_call(
        _na_kernel,
        grid=(n_pairs, inner),
        in_specs=[pl.BlockSpec(memory_space=pltpu.SMEM),
                  pl.BlockSpec((nb, 1, n_q2, LANES), lambda p, b: (b, p, 0, 0)),
                  pl.BlockSpec((nb, LANES, SEQ), lambda p, b: (b, p, 0)), lat,
                  pl.BlockSpec((nb, LANES, CTX_LEN), lambda p, b: (b, p, 0)), cx, w_chunk],
        out_specs=[lat, w_chunk],
        out_shape=[jax.ShapeDtypeStruct((batch, SEQ, NA_WIDTH), BF16), jax.ShapeDtypeStruct(w_f32.shape, BF16)],
        scratch_shapes=[pltpu.VMEM((N_PATTERNS, PAIR_ROWS, NA_BAND), F32),
                        pltpu.VMEM((nb, 2, LANES, SEQ), BF16),
                        pltpu.VMEM((nb, 2, NA_GROUP * PAIR_ROWS, NA_BAND + CTX_LEN), F32)],
        compiler_params=_params(2),
        name="na",
    )(rpb, nq, nkt, nv, cnkt, cnv, w_f32)


def _out_kernel(x_ref, ret_ref, na_ref, mod_ref, gpm_ref, gpre_ref, gpost_ref, wo_ref, w1_ref, w2_ref, o_ref):
    gt1, sh2, sc2, gt2 = (mod_ref[0, i:i + 1, :] for i in (2, 3, 4, 5))
    subs = [slice(i * SUB_TILE, (i + 1) * SUB_TILE) for i in range(OUT_TILE // SUB_TILE)]
    mix = [jnp.dot(ret_ref[0, rows, :], wo_ref[0:RET_WIDTH, :], preferred_element_type=F32)
           + jnp.dot(na_ref[0, rows, :], wo_ref[RET_WIDTH:, :], preferred_element_type=F32) for rows in subs]
    h2 = []
    for rows, m in zip(subs, mix):
        x1 = x_ref[0, rows, :] + gt1 * _rms(m, gpm_ref[...])
        o_ref[0, rows, :] = x1
        h2.append(_prenorm(x1, gpre_ref[...], sh2, sc2))
    for rows, h in zip(subs, h2):
        acc = jnp.zeros((SUB_TILE, D_MODEL), F32)
        for ci in range(D_FF // FF_CHUNK):
            cols = slice(ci * FF_CHUNK, (ci + 1) * FF_CHUNK)
            a = jnp.maximum(jnp.dot(h, w1_ref[:, cols], preferred_element_type=F32), 0.0)
            acc = acc + jnp.dot((a * a).astype(BF16), w2_ref[cols, :], preferred_element_type=F32)
        o_ref[0, rows, :] = o_ref[0, rows, :] + gt2 * _rms(acc, gpost_ref[...])


def _out_proj_mlp(x, ret_lat, na_lat, mods, g_post_mix, g_pre_mlp, g_post_mlp, w_out, w_mlp1, w_mlp2):
    batch = x.shape[0]
    tok = lambda b, i: (b, i, 0)
    gain = pl.BlockSpec((1, D_MODEL), lambda b, i: (0, 0))
    return pl.pallas_call(
        _out_kernel,
        grid=(batch, SEQ // OUT_TILE),
        in_specs=[pl.BlockSpec((1, OUT_TILE, D_MODEL), tok),
                  pl.BlockSpec((1, OUT_TILE, RET_WIDTH), tok),
                  pl.BlockSpec((1, OUT_TILE, NA_WIDTH), tok),
                  pl.BlockSpec((1, N_MOD, D_MODEL), lambda b, i: (b, 0, 0)),
                  gain, gain, gain,
                  _resident((RET_WIDTH + NA_WIDTH, D_MODEL)),
                  _resident((D_MODEL, D_FF)),
                  _resident((D_FF, D_MODEL))],
        out_specs=pl.BlockSpec((1, OUT_TILE, D_MODEL), tok),
        out_shape=jax.ShapeDtypeStruct(x.shape, F32),
        compiler_params=_params(2),
        name="out_mlp",
    )(x, ret_lat, na_lat, mods, g_post_mix, g_pre_mlp, g_post_mlp, w_out, w_mlp1, w_mlp2)


def kernel(x, c, ctx, c_ctx, w_ada, b_ada, g_pre_mix, g_post_mix, g_pre_mlp, g_post_mlp,
           w_in, ret_decay, ret_gn, na_rpb, w_out, w_mlp1, w_mlp2):
    assert w_in.shape[0] == 1, "single-layer block: the context stream is never updated"
    batch = x.shape[0]
    pad = (-(batch + 1)) % 8
    cc = jnp.concatenate([c, c_ctx[None, :], jnp.zeros((pad, D_MODEL), F32)], axis=0)
    mods = _modulations(cc, w_ada[0], b_ada)
    mods_lat = mods[:batch].reshape(batch, N_MOD, D_MODEL)
    mods_ctx = mods[batch:batch + 1].reshape(1, N_MOD, D_MODEL)

    rq, rk, rv, rg, nq, nkt, nv = _in_proj_lat(x, mods_lat, g_pre_mix, w_in[0])
    crk, crv, cnkt, cnv, w_out_b = _in_proj_ctx(ctx, mods_ctx, g_pre_mix, w_in[0], w_out[0])

    log_gammas = jax.nn.log_sigmoid(ret_decay[0].astype(F32))
    ret_lat, w1_b = _retention(log_gammas, rq, rk, rv, rg, crk, crv, ret_gn, w_mlp1[0])
    na_lat, w2_b = _neighbourhood_attention(na_rpb[0].astype(F32).reshape(-1), nq, nkt, nv, cnkt, cnv, w_mlp2[0])

    return _out_proj_mlp(x, ret_lat, na_lat, mods_lat, g_post_mix, g_pre_mlp, g_post_mlp,
                         w_out_b, w1_b, w2_b)
```
